```python
import math
import jax, jax.numpy as jnp
from jax import lax
import numpy as np

D_MODEL = 1024
BATCH = 4
SEQ = 4096
DEPTH = 1
DEC_BATCH = 32
DEC_SEQ = 32
PAST_LEN = 2048

CHUNK = 64
Q_BLOCK = 128
PLE_DIM = 256
EPS = 1e-6
A_HEADS = D_MODEL // 128
A_DIM = 64
A_VDIM = 2 * A_DIM
R_HEADS = D_MODEL // 128
R_DK = 64
R_DV = 128
ROPE_BASE = 10000.0
D_FF = -(-8 * D_MODEL // (3 * 256)) * 256
IN_WIDTHS = (
    A_HEADS * 2 * A_DIM,
    A_HEADS * 2 * A_DIM,
    A_HEADS * A_VDIM,
    R_HEADS * R_DK,
    R_HEADS * R_DK,
    R_HEADS * R_DV,
    R_HEADS * R_DV,
    D_MODEL,
    D_MODEL,
)
IN_WIDTH = sum(IN_WIDTHS)

kernel_name = "diffattn_retention_gated_stream_step"


def rmsnorm(x, g):
    xf = x.astype(jnp.float32)
    y = xf * lax.rsqrt(jnp.mean(xf * xf, axis=-1, keepdims=True) + EPS)
    return (y * g.astype(jnp.float32)).astype(x.dtype)


def rotary(x, pos):
    half = x.shape[-1] // 2
    inv_freq = ROPE_BASE ** (-jnp.arange(half, dtype=jnp.float32) / half)
    ang = pos.astype(jnp.float32)[:, None] * inv_freq[None, :]
    cos = jnp.cos(ang)[:, None, :]
    sin = jnp.sin(ang)[:, None, :]
    x1, x2 = x[..., :half], x[..., half:]
    return jnp.concatenate([x1 * cos - x2 * sin, x1 * sin + x2 * cos], axis=-1)


def mix_inputs(x, w_in, g_norm, g_qn, g_kn, pos):
    B, T, _ = x.shape
    h = rmsnorm(x, g_norm)
    z = h @ w_in
    splits = np.cumsum(IN_WIDTHS)[:-1].tolist()
    qa, ka, va, qr, kr, vr, g_ret, gate_a, gate_r = jnp.split(z, splits, axis=-1)
    qa = rmsnorm(qa.reshape(B, T, A_HEADS, 2, A_DIM), g_qn)
    ka = rmsnorm(ka.reshape(B, T, A_HEADS, 2, A_DIM), g_kn)
    va = va.reshape(B, T, A_HEADS, A_VDIM)
    qr = rotary(qr.reshape(B, T, R_HEADS, R_DK).astype(jnp.float32), pos)
    kr = rotary(kr.reshape(B, T, R_HEADS, R_DK).astype(jnp.float32), pos) * (R_DK ** -0.5)
    vr = vr.reshape(B, T, R_HEADS, R_DV).astype(jnp.float32)
    return qa, ka, va, qr, kr, vr, g_ret, gate_a, gate_r


def diff_attn_prompt(qa, ka, va, lam):
    B, S = qa.shape[:2]
    nb = S // Q_BLOCK
    scale = A_DIM ** -0.5
    k_chunk = jnp.arange(S) // CHUNK
    qb = qa.reshape(B, nb, Q_BLOCK, A_HEADS, 2, A_DIM).swapaxes(0, 1)

    def block(args):
        q_blk, bi = args
        s = jnp.einsum('bqhcd,bkhcd->bhcqk', q_blk, ka).astype(jnp.float32) * scale
        q_chunk = (bi * Q_BLOCK + jnp.arange(Q_BLOCK)) // CHUNK
        mask = k_chunk[None, :] <= q_chunk[:, None]
        p = jax.nn.softmax(jnp.where(mask, s, -jnp.inf), axis=-1)
        a = p[:, :, 0] - lam * p[:, :, 1]
        return jnp.einsum('bhqk,bkhe->bqhe', a.astype(va.dtype), va)

    o = lax.map(block, (qb, jnp.arange(nb)))
    return o.swapaxes(0, 1).reshape(B, S, A_HEADS, A_VDIM)


def diff_attn_sample(qa, k_all, v_all, lam):
    s = jnp.einsum('bqhcd,bkhcd->bhcqk', qa, k_all).astype(jnp.float32) * (A_DIM ** -0.5)
    p = jax.nn.softmax(s, axis=-1)
    a = p[:, :, 0] - lam * p[:, :, 1]
    return jnp.einsum('bhqk,bkhe->bqhe', a.astype(v_all.dtype), v_all)


def diff_out(o, g_sub, lam_init):
    B, T = o.shape[:2]
    return (rmsnorm(o, g_sub) * (1.0 - lam_init)).reshape(B, T, A_HEADS * A_VDIM)


def ret_decay(T):
    log_g = jnp.log1p(-jnp.exp2(-5.0 - jnp.arange(R_HEADS, dtype=jnp.float32)))
    i = jnp.arange(T, dtype=jnp.float32)
    diff = i[:, None] - i[None, :]
    d_mat = jnp.where(diff >= 0, jnp.exp(log_g[:, None, None] * jnp.maximum(diff, 0.0)), 0.0)
    q_decay = jnp.exp(log_g[None, :] * (i[:, None] + 1.0))
    k_decay = jnp.exp(log_g[None, :] * (T - 1.0 - i[:, None]))
    chunk_decay = jnp.exp(log_g * T)
    return d_mat, q_decay, k_decay, chunk_decay


def retention_chunk(q, k, v, s0, d_mat, q_decay):
    inner = jnp.einsum('...ihd,...jhd->...hij', q, k) * d_mat
    o = jnp.einsum('...hij,...jhe->...ihe', inner, v)
    return o + jnp.einsum('...ihd,...hde->...ihe', q, s0) * q_decay[:, :, None]


def ret_increment(k, v, k_decay):
    return jnp.einsum('...jhd,...jhe->...hde', k * k_decay[:, :, None], v)


def retention_prompt(q, k, v):
    B, S = q.shape[:2]
    nc = S // CHUNK
    d_mat, q_decay, k_decay, chunk_decay = ret_decay(CHUNK)
    qc = q.reshape(B, nc, CHUNK, R_HEADS, R_DK)
    kc = k.reshape(B, nc, CHUNK, R_HEADS, R_DK)
    vc = v.reshape(B, nc, CHUNK, R_HEADS, R_DV)
    kv = ret_increment(kc, vc, k_decay)

    def step(s, kv_c):
        return chunk_decay[:, None, None] * s + kv_c, s

    s0 = jnp.zeros((B, R_HEADS, R_DK, R_DV), jnp.float32)
    s_final, s_starts = lax.scan(step, s0, kv.swapaxes(0, 1))
    o = retention_chunk(qc, kc, vc, s_starts.swapaxes(0, 1), d_mat, q_decay)
    return o.reshape(B, S, R_HEADS, R_DV), s_final


def retention_sample(q, k, v, s0):
    T = q.shape[1]
    d_mat, q_decay, k_decay, chunk_decay = ret_decay(T)
    o = retention_chunk(q, k, v, s0, d_mat, q_decay)
    s_new = chunk_decay[:, None, None] * s0 + ret_increment(k, v, k_decay)
    return o, s_new


def ret_out(o, g_ret, g_rn):
    B, T = o.shape[:2]
    o = rmsnorm(o, g_rn).reshape(B, T, R_HEADS * R_DV).astype(g_ret.dtype)
    return o * jax.nn.silu(g_ret)


def merge_and_tail(x, o_a, o_r, gate_a, gate_r, p, w_o, g_ffn, w_ff_gate, w_ff_up,
                   w_ff_down, g_ple, w_ple, w_ple_gate):
    mix = jax.nn.sigmoid(gate_a) * o_a + jax.nn.sigmoid(gate_r) * o_r
    x = x + mix @ w_o
    h = rmsnorm(x, g_ffn)
    x = x + (jax.nn.silu(h @ w_ff_gate) * (h @ w_ff_up)) @ w_ff_down
    gate = jax.nn.sigmoid(rmsnorm(x, g_ple) @ w_ple_gate)
    return x + gate * (p @ w_ple)


def setup_inputs(seed: int = 0) -> dict:
    key = jax.random.key(seed)
    ks = jax.random.split(key, 24)
    f32 = jnp.float32

    def nrm(k, shape, s):
        return jax.random.normal(k, shape, f32) * s

    def gain(k, shape):
        return 1.0 + 0.02 * jax.random.normal(k, shape, f32)

    return {
        "x_prompt": nrm(ks[0], (BATCH, SEQ, D_MODEL), 1.0),
        "x_sample": nrm(ks[1], (DEC_BATCH, DEC_SEQ, D_MODEL), 1.0),
        "cache_attn_k": nrm(ks[2], (DEPTH, DEC_BATCH, PAST_LEN, A_HEADS, 2 * A_DIM), 1.0),
        "cache_attn_v": nrm(ks[3], (DEPTH, DEC_BATCH, PAST_LEN, A_HEADS, A_VDIM), 1.0),
        "state_ret": nrm(ks[4], (DEPTH, DEC_BATCH, R_HEADS, R_DK, R_DV), 0.5),
        "p_prompt": nrm(ks[5], (DEPTH, BATCH, SEQ, PLE_DIM), 1.0),
        "p_sample": nrm(ks[6], (DEPTH, DEC_BATCH, DEC_SEQ, PLE_DIM), 1.0),
        "w_in": nrm(ks[7], (DEPTH, D_MODEL, IN_WIDTH), D_MODEL ** -0.5),
        "g_mix_norm": gain(ks[8], (DEPTH, D_MODEL)),
        "g_q_norm": gain(ks[9], (DEPTH, A_DIM)),
        "g_k_norm": gain(ks[10], (DEPTH, A_DIM)),
        "lam_q": nrm(ks[11], (DEPTH, 2, A_DIM), 0.1),
        "lam_k": nrm(ks[12], (DEPTH, 2, A_DIM), 0.1),
        "g_sub_norm": gain(ks[13], (DEPTH, A_VDIM)),
        "g_ret_norm": gain(ks[14], (DEPTH, R_HEADS, R_DV)),
        "w_o": nrm(ks[15], (DEPTH, D_MODEL, D_MODEL), D_MODEL ** -0.5),
        "g_ffn_norm": gain(ks[16], (DEPTH, D_MODEL)),
        "w_ff_gate": nrm(ks[17], (DEPTH, D_MODEL, D_FF), D_MODEL ** -0.5),
        "w_ff_up": nrm(ks[18], (DEPTH, D_MODEL, D_FF), D_MODEL ** -0.5),
        "w_ff_down": nrm(ks[19], (DEPTH, D_FF, D_MODEL), D_FF ** -0.5),
        "g_ple_norm": gain(ks[20], (DEPTH, D_MODEL)),
        "w_ple": nrm(ks[21], (DEPTH, PLE_DIM, D_MODEL), PLE_DIM ** -0.5),
        "w_ple_gate": nrm(ks[22], (DEPTH, D_MODEL, D_MODEL), D_MODEL ** -0.5),
    }


def reference(x_prompt, x_sample, cache_attn_k, cache_attn_v, state_ret, p_prompt, p_sample,
              w_in, g_mix_norm, g_q_norm, g_k_norm, lam_q, lam_k, g_sub_norm, g_ret_norm, w_o,
              g_ffn_norm, w_ff_gate, w_ff_up, w_ff_down, g_ple_norm, w_ple, w_ple_gate):
    S = x_prompt.shape[1]
    T = x_sample.shape[1]
    P = cache_attn_k.shape[2]
    DB = x_sample.shape[0]
    pos_p = jnp.arange(S)
    pos_s = P + jnp.arange(T)
    xp, xs = x_prompt, x_sample
    kp_l, vp_l, sp_l, ks_l, vs_l, ss_l = [], [], [], [], [], []
    for l in range(DEPTH):
        lam_init = 0.8 - 0.6 * math.exp(-0.3 * l)
        lq = lam_q[l].astype(jnp.float32)
        lk = lam_k[l].astype(jnp.float32)
        lam = jnp.exp(jnp.sum(lq[0] * lk[0])) - jnp.exp(jnp.sum(lq[1] * lk[1])) + lam_init

        qa, ka, va, qr, kr, vr, g_ret, gate_a, gate_r = mix_inputs(
            xp, w_in[l], g_mix_norm[l], g_q_norm[l], g_k_norm[l], pos_p)
        o_a = diff_out(diff_attn_prompt(qa, ka, va, lam), g_sub_norm[l], lam_init)
        o_r, s_fin = retention_prompt(qr, kr, vr)
        o_r = ret_out(o_r, g_ret, g_ret_norm[l])
        kp_l.append(ka.reshape(xp.shape[0], S, A_HEADS, 2 * A_DIM).astype(cache_attn_k.dtype))
        vp_l.append(va.astype(cache_attn_v.dtype))
        sp_l.append(s_fin.astype(state_ret.dtype))
        xp = merge_and_tail(xp, o_a, o_r, gate_a, gate_r, p_prompt[l], w_o[l], g_ffn_norm[l],
                            w_ff_gate[l], w_ff_up[l], w_ff_down[l], g_ple_norm[l], w_ple[l],
                            w_ple_gate[l])

        qa, ka, va, qr, kr, vr, g_ret, gate_a, gate_r = mix_inputs(
            xs, w_in[l], g_mix_norm[l], g_q_norm[l], g_k_norm[l], pos_s)
        k_all = jnp.concatenate(
            [cache_attn_k[l].reshape(DB, P, A_HEADS, 2, A_DIM).astype(ka.dtype), ka], axis=1)
        v_all = jnp.concatenate([cache_attn_v[l].astype(va.dtype), va], axis=1)
        o_a = diff_out(diff_attn_sample(qa, k_all, v_all, lam), g_sub_norm[l], lam_init)
        o_r, s_new = retention_sample(qr, kr, vr, state_ret[l].astype(jnp.float32))
        o_r = ret_out(o_r, g_ret, g_ret_norm[l])
        ks_l.append(ka.reshape(DB, T, A_HEADS, 2 * A_DIM).astype(cache_attn_k.dtype))
        vs_l.append(va.astype(cache_attn_v.dtype))
        ss_l.append(s_new.astype(state_ret.dtype))
        xs = merge_and_tail(xs, o_a, o_r, gate_a, gate_r, p_sample[l], w_o[l], g_ffn_norm[l],
                            w_ff_gate[l], w_ff_up[l], w_ff_down[l], g_ple_norm[l], w_ple[l],
                            w_ple_gate[l])

    return (xp, xs, jnp.stack(kp_l), jnp.stack(vp_l), jnp.stack(sp_l),
            jnp.stack(ks_l), jnp.stack(vs_l), jnp.stack(ss_l))
```

```python
import functools
import math

import jax
import jax.numpy as jnp
from jax import lax
from jax.experimental import pallas as pl
from jax.experimental.pallas import tpu as pltpu

F32 = jnp.float32
BF16 = jnp.bfloat16

D_MODEL = 1024
CHUNK = 64
PLE_DIM = 256
EPS = 1e-6
HEADS = 8
HEAD_W = 128
A_DIM = 64
R_DK = 64
ROPE_BASE = 10000.0
D_FF = 2816
N_PAIR = HEADS // 2

OFF_QA, OFF_KA, OFF_VA = 0, 1024, 2048
OFF_QR, OFF_KR, OFF_VR = 3072, 3584, 4096
OFF_GRET, OFF_GA, OFF_GR = 5120, 6144, 7168
IN_WIDTH = 8192

MXU_N = 256
VMEM_LIMIT = 56 * 1024 * 1024
NEG_BIG = -1e30

_NT = (((1,), (1,)), ((), ()))


def _dot(a, b):
    return jnp.dot(a, b, preferred_element_type=F32)


def _dot_nt(a, b):
    return lax.dot_general(a, b, _NT, preferred_element_type=F32)


def _rms_scale(x):
    return x * lax.rsqrt(jnp.mean(x * x, axis=-1, keepdims=True) + EPS)


def _inproj_body(x_ref, w_ref, gmix_ref, gq_ref, gk_ref, cos_ref, sin_ref, gmat_ref,
                 dpair_ref, qdec_ref, kdec_ref, cdec_ref, grn_ref,
                 qa_ref, kf_ref, kb_ref, vf_ref, vb_ref, ra_ref, ga_ref,
                 qs_ref, ks_ref, vs_ref, gs_ref, *, L, get_state, set_state):
    tm = x_ref.shape[0]
    x = x_ref[...]
    h = (_rms_scale(x) * gmix_ref[...]).astype(BF16)
    gmat = gmat_ref[...]

    def proj(off, c, width=MXU_N):
        return _dot(h, w_ref[:, off + c * width: off + (c + 1) * width])

    def group_norm(z, g_ref):
        ss = _dot((z * z).astype(BF16), gmat)
        return (z * lax.rsqrt(ss * (1.0 / A_DIM) + EPS)) * g_ref[...]

    for c in range(D_MODEL // MXU_N):
        sl = slice(c * MXU_N, (c + 1) * MXU_N)
        qn = group_norm(proj(OFF_QA, c), gq_ref)
        qa_ref[:, sl] = (qn * (A_DIM ** -0.5)).astype(BF16)
        kn = group_norm(proj(OFF_KA, c), gk_ref)
        kf_ref[:, sl] = kn
        kb_ref[:, sl] = kn.astype(BF16)
        v = proj(OFF_VA, c)
        vf_ref[:, sl] = v
        vb_ref[:, sl] = v.astype(BF16)
        vs_ref[:, sl] = proj(OFF_VR, c).astype(BF16)
        g_ret = proj(OFF_GRET, c)
        gate_r = proj(OFF_GR, c)
        gs_ref[:, sl] = (g_ret * jax.nn.sigmoid(g_ret)) * jax.nn.sigmoid(gate_r)
        ga_ref[:, sl] = jax.nn.sigmoid(proj(OFF_GA, c)).astype(BF16)

    lane = lax.broadcasted_iota(jnp.int32, (tm, HEAD_W), 1)
    first_half = (lane % R_DK) < (R_DK // 2)
    cos = cos_ref[...]
    sin = sin_ref[...]

    def rotary(z):
        partner = jnp.where(first_half, pltpu.roll(z, HEAD_W - R_DK // 2, axis=1),
                            pltpu.roll(z, R_DK // 2, axis=1))
        return z * cos + partner * sin

    for c in range(2):
        zq = proj(OFF_QR, c)
        zk = proj(OFF_KR, c)
        for half in range(2):
            hs = slice(half * HEAD_W, (half + 1) * HEAD_W)
            ds = slice((2 * c + half) * HEAD_W, (2 * c + half + 1) * HEAD_W)
            qs_ref[:, ds] = rotary(zq[:, hs]).astype(BF16)
            ks_ref[:, ds] = rotary(zk[:, hs]) * (R_DK ** -0.5)

    lane_l = lax.broadcasted_iota(jnp.int32, (L, HEAD_W), 1)
    even = lane_l < R_DK

    def chunk_body(ci, carry):
        r0 = pl.multiple_of(ci * L, L)
        rows = pl.ds(r0, L)
        for g in range(N_PAIR):
            q2 = qs_ref[rows, g * HEAD_W:(g + 1) * HEAD_W]
            k2 = ks_ref[rows, g * HEAD_W:(g + 1) * HEAD_W]
            v2 = vs_ref[rows, 2 * g * HEAD_W:(2 * g + 2) * HEAD_W]
            zero = jnp.zeros_like(q2)
            qq = jnp.concatenate([jnp.where(even, q2, zero), jnp.where(even, zero, q2)], axis=0)
            a = (_dot_nt(qq, k2.astype(BF16)) * dpair_ref[g]).astype(BF16)
            intra = _dot(a, v2)
            state = get_state(ci, g)
            cross = _dot(qq, state.astype(BF16)) * qdec_ref[g]
            kd = (k2 * kdec_ref[g]).T.astype(BF16)
            kv = _dot(kd, v2)
            kv_pair = jnp.concatenate([kv[:R_DK, :HEAD_W], kv[R_DK:, HEAD_W:]], axis=0)
            set_state(ci, g, cdec_ref[g] * state + kv_pair)
            outs = (intra[:L, :HEAD_W] + cross[:L], intra[L:, HEAD_W:] + cross[L:])
            for par, o in enumerate(outs):
                hh = 2 * g + par
                cols = slice(hh * HEAD_W, (hh + 1) * HEAD_W)
                on = _rms_scale(o) * grn_ref[hh:hh + 1, :]
                ra_ref[rows, cols] = (on * gs_ref[rows, cols]).astype(BF16)
        return carry

    lax.fori_loop(0, tm // L, chunk_body, 0)


def _inproj_prompt_kernel(*refs):
    ins, outs = refs[:13], refs[13:]
    (qa_ref, kf_ref, kb_ref, vf_ref, vb_ref, ra_ref, ga_ref, st_ref,
     qs_ref, ks_ref, vs_ref, gs_ref) = outs

    @pl.when(pl.program_id(1) == 0)
    def _():
        st_ref[...] = jnp.zeros_like(st_ref)

    def get_state(ci, g):
        return st_ref[0, g * HEAD_W:(g + 1) * HEAD_W, :]

    def set_state(ci, g, val):
        st_ref[0, g * HEAD_W:(g + 1) * HEAD_W, :] = val

    _inproj_body(*ins, qa_ref, kf_ref, kb_ref, vf_ref, vb_ref, ra_ref, ga_ref,
                 qs_ref, ks_ref, vs_ref, gs_ref, L=CHUNK, get_state=get_state, set_state=set_state)


def _inproj_sample_kernel(*refs, L):
    ins, s0_ref, outs = refs[:13], refs[13], refs[14:]
    (qa_ref, kf_ref, kb_ref, vf_ref, vb_ref, ra_ref, ga_ref, st_ref,
     qs_ref, ks_ref, vs_ref, gs_ref) = outs

    def get_state(ci, g):
        return s0_ref[ci, g * HEAD_W:(g + 1) * HEAD_W, :]

    def set_state(ci, g, val):
        st_ref[ci, g * HEAD_W:(g + 1) * HEAD_W, :] = val

    _inproj_body(*ins, qa_ref, kf_ref, kb_ref, vf_ref, vb_ref, ra_ref, ga_ref,
                 qs_ref, ks_ref, vs_ref, gs_ref, L=L, get_state=get_state, set_state=set_state)


def _ret_tables(T):
    log_g = jnp.log1p(-jnp.exp2(-5.0 - jnp.arange(HEADS, dtype=F32)))
    i = jnp.arange(T, dtype=F32)
    diff = i[:, None] - i[None, :]
    d_mat = jnp.where(diff >= 0, jnp.exp(log_g[:, None, None] * jnp.maximum(diff, 0.0)), 0.0)
    q_decay = jnp.exp(log_g[None, :] * (i[:, None] + 1.0))
    k_decay = jnp.exp(log_g[None, :] * (T - 1.0 - i[:, None]))
    chunk_decay = jnp.exp(log_g * T)
    dpair = d_mat.reshape(N_PAIR, 2 * T, T)
    qdec = jnp.broadcast_to(q_decay.T.reshape(N_PAIR, 2 * T, 1), (N_PAIR, 2 * T, HEAD_W))
    kdec = jnp.repeat(k_decay.reshape(T, N_PAIR, 2), R_DK, axis=2).transpose(1, 0, 2)
    cdec = jnp.broadcast_to(jnp.repeat(chunk_decay.reshape(N_PAIR, 2), R_DK, axis=1)[:, :, None],
                            (N_PAIR, 2 * R_DK, HEAD_W))
    return dpair, qdec, kdec, cdec


def _rope_tables(pos):
    half = R_DK // 2
    inv_freq = ROPE_BASE ** (-jnp.arange(half, dtype=F32) / half)
    ang = pos.astype(F32)[:, None] * inv_freq[None, :]
    cos, sin = jnp.cos(ang), jnp.sin(ang)
    return jnp.tile(cos, (1, 4)), jnp.tile(jnp.concatenate([-sin, sin], axis=1), (1, 2))


def _inproj(x2d, w_in, g_mix, gq, gk, cos_t, sin_t, gmat, tables, g_rn, *, tm, seq, state0):
    n = x2d.shape[0]
    prompt = state0 is None
    dpair, qdec, kdec, cdec = tables
    nt = seq // tm if prompt else 1
    nb = n // seq if prompt else n // tm
    per_tile = tm // seq if not prompt else 1

    const2 = lambda *_: (0, 0)
    const3 = lambda *_: (0, 0, 0)
    row = lambda b, i: (b * nt + i, 0)
    resident = functools.partial(pl.BlockSpec, pipeline_mode=pl.Buffered(1))
    in_specs = [
        pl.BlockSpec((tm, D_MODEL), row),
        resident((D_MODEL, IN_WIDTH), const2),
        resident((1, D_MODEL), const2),
        resident((1, MXU_N), const2),
        resident((1, MXU_N), const2),
        pl.BlockSpec((tm, HEAD_W), (lambda b, i: (i, 0)) if prompt else const2),
        pl.BlockSpec((tm, HEAD_W), (lambda b, i: (i, 0)) if prompt else const2),
        resident((MXU_N, MXU_N), const2),
        resident(dpair.shape, const3),
        resident(qdec.shape, const3),
        resident(kdec.shape, const3),
        resident(cdec.shape, const3),
        resident((HEADS, HEAD_W), const2),
    ]
    args = [x2d, w_in, g_mix, gq, gk, cos_t, sin_t, gmat, dpair, qdec, kdec, cdec, g_rn]
    st_rows = 2 * R_DK * N_PAIR
    if prompt:
        st_spec = pl.BlockSpec((1, st_rows, HEAD_W), lambda b, i: (b, 0, 0))
        st_shape = jax.ShapeDtypeStruct((nb, st_rows, HEAD_W), F32)
        kernel = _inproj_prompt_kernel
        L = CHUNK
    else:
        st_spec = pl.BlockSpec((per_tile, st_rows, HEAD_W), lambda b, i: (b, 0, 0))
        st_shape = jax.ShapeDtypeStruct(state0.shape, F32)
        in_specs.append(st_spec)
        args.append(state0)
        kernel = functools.partial(_inproj_sample_kernel, L=seq)
        L = seq
    tok = lambda dt: jax.ShapeDtypeStruct((n, D_MODEL), dt)
    out_shape = [tok(BF16), tok(F32), tok(BF16), tok(F32), tok(BF16), tok(BF16), tok(BF16), st_shape]
    out_specs = [pl.BlockSpec((tm, D_MODEL), row)] * 7 + [st_spec]
    scratch = [pltpu.VMEM((tm, HEADS * R_DK), BF16), pltpu.VMEM((tm, HEADS * R_DK), F32),
               pltpu.VMEM((tm, D_MODEL), BF16), pltpu.VMEM((tm, D_MODEL), F32)]
    return pl.pallas_call(
        kernel, out_shape=out_shape, grid=(nb, nt), in_specs=in_specs, out_specs=out_specs,
        scratch_shapes=scratch, name="inproj_prompt" if prompt else "inproj_sample",
        compiler_params=pltpu.CompilerParams(dimension_semantics=("arbitrary", "arbitrary"),
                                             vmem_limit_bytes=VMEM_LIMIT),
    )(*args)


def _lam(lamq_ref, lamk_ref, lam_init):
    e = jnp.exp(jnp.sum(lamq_ref[...] * lamk_ref[...], axis=-1, keepdims=True))
    return e[0:1, :] - e[1:2, :] + lam_init


def _split_maps(q):
    lane = lax.broadcasted_iota(jnp.int32, q.shape, 1)
    zero = jnp.zeros_like(q)
    return jnp.concatenate([jnp.where(lane < A_DIM, q, zero), jnp.where(lane < A_DIM, zero, q)], axis=0)


def _merge_out(acc, l, t, lam, gsub_ref, ga, ra, lam_init):
    o = acc[:t] / l[:t] - lam * (acc[t:] / l[t:])
    on = (_rms_scale(o) * gsub_ref[...]) * (1.0 - lam_init)
    return (ga.astype(F32) * on + ra.astype(F32)).astype(BF16)


def _attn_prompt_kernel(q_ref, k_ref, v_ref, ga_ref, ra_ref, lamq_ref, lamk_ref, gsub_ref, o_ref,
                        *, tq, lam_init):
    i = pl.program_id(2)
    qq = _split_maps(q_ref[0])

    def step(j, carry, mask):
        m, l, acc = carry
        rows = pl.ds(pl.multiple_of(j * tq, tq), tq)
        s = _dot_nt(qq, k_ref[0, rows, :])
        if mask is not None:
            s = jnp.where(mask, s, NEG_BIG)
        m_new = jnp.maximum(m, jnp.max(s, axis=-1, keepdims=True))
        alpha = jnp.exp(m - m_new)
        p = jnp.exp(s - m_new)
        l = alpha * l + jnp.sum(p, axis=-1, keepdims=True)
        acc = alpha * acc + _dot(p.astype(BF16), v_ref[0, rows, :])
        return m_new, l, acc

    init = (jnp.full((2 * tq, 1), NEG_BIG, F32), jnp.zeros((2 * tq, 1), F32),
            jnp.zeros((2 * tq, HEAD_W), F32))
    carry = lax.fori_loop(0, i, lambda j, c: step(j, c, None), init)
    r = lax.broadcasted_iota(jnp.int32, (2 * tq, tq), 0)
    c = lax.broadcasted_iota(jnp.int32, (2 * tq, tq), 1)
    diag_mask = (c // CHUNK) <= ((r % tq) // CHUNK)
    _, l, acc = step(i, carry, diag_mask)
    lam = _lam(lamq_ref, lamk_ref, lam_init)
    o_ref[0] = _merge_out(acc, l, tq, lam, gsub_ref, ga_ref[0], ra_ref[0], lam_init)


def _attn_prompt(qa, kb, vb, ga, ra, lam_q, lam_k, g_sub, *, tq, lam_init):
    B, S, _ = qa.shape
    blk = pl.BlockSpec((1, tq, HEAD_W), lambda b, h, i: (b, i, h))
    kv = pl.BlockSpec((1, S, HEAD_W), lambda b, h, i: (b, 0, h))
    small = lambda shape: pl.BlockSpec(shape, lambda b, h, i: (0, 0))
    return pl.pallas_call(
        functools.partial(_attn_prompt_kernel, tq=tq, lam_init=lam_init),
        out_shape=jax.ShapeDtypeStruct((B, S, D_MODEL), BF16),
        grid=(B, HEADS, S // tq),
        in_specs=[blk, kv, kv, blk, blk, small((2, A_DIM)), small((2, A_DIM)), small((1, HEAD_W))],
        out_specs=blk, name="attn_prompt",
        compiler_params=pltpu.CompilerParams(dimension_semantics=("arbitrary",) * 3,
                                             vmem_limit_bytes=VMEM_LIMIT),
    )(qa, kb, vb, ga, ra, lam_q, lam_k, g_sub)


def _attn_sample_kernel(q_ref, kc_ref, vc_ref, kn_ref, vn_ref, ga_ref, ra_ref, lamq_ref, lamk_ref,
                        gsub_ref, o_ref, *, lam_init):
    t = q_ref.shape[1]
    lam = _lam(lamq_ref, lamk_ref, lam_init)
    for h in range(HEADS):
        sl = slice(h * HEAD_W, (h + 1) * HEAD_W)
        qq = _split_maps(q_ref[0, :, sl])
        s_c = _dot_nt(qq, kc_ref[0, :, sl].astype(BF16))
        s_n = _dot_nt(qq, kn_ref[0, :, sl])
        m = jnp.maximum(jnp.max(s_c, axis=-1, keepdims=True), jnp.max(s_n, axis=-1, keepdims=True))
        p_c = jnp.exp(s_c - m)
        p_n = jnp.exp(s_n - m)
        l = jnp.sum(p_c, axis=-1, keepdims=True) + jnp.sum(p_n, axis=-1, keepdims=True)
        acc = _dot(p_c.astype(BF16), vc_ref[0, :, sl].astype(BF16)) + _dot(p_n.astype(BF16), vn_ref[0, :, sl])
        o_ref[0, :, sl] = _merge_out(acc, l, t, lam, gsub_ref, ga_ref[0, :, sl], ra_ref[0, :, sl], lam_init)


def _attn_sample(qa, kc, vc, kn, vn, ga, ra, lam_q, lam_k, g_sub, *, lam_init):
    DB, T, _ = qa.shape
    P = kc.shape[1]
    blk = pl.BlockSpec((1, T, D_MODEL), lambda b: (b, 0, 0))
    cache = pl.BlockSpec((1, P, D_MODEL), lambda b: (b, 0, 0))
    small = lambda shape: pl.BlockSpec(shape, lambda b: (0, 0))
    return pl.pallas_call(
        functools.partial(_attn_sample_kernel, lam_init=lam_init),
        out_shape=jax.ShapeDtypeStruct((DB, T, D_MODEL), BF16),
        grid=(DB,),
        in_specs=[blk, cache, cache, blk, blk, blk, blk, small((2, A_DIM)), small((2, A_DIM)),
                  small((1, HEAD_W))],
        out_specs=blk, name="attn_sample",
        compiler_params=pltpu.CompilerParams(dimension_semantics=("arbitrary",),
                                             vmem_limit_bytes=VMEM_LIMIT),
    )(qa, kc, vc, kn, vn, ga, ra, lam_q, lam_k, g_sub)


def _tail_kernel(x_ref, mix_ref, p_ref, wo_ref, gffn_ref, wg_ref, wu_ref, wd_ref, gple_ref, wple_ref,
                 wpg_ref, y_ref, act_ref):
    x1 = x_ref[...] + _dot(mix_ref[...], wo_ref[...])
    h = (_rms_scale(x1) * gffn_ref[...]).astype(BF16)
    for c in range(D_FF // MXU_N):
        sl = slice(c * MXU_N, (c + 1) * MXU_N)
        g = _dot(h, wg_ref[:, sl])
        u = _dot(h, wu_ref[:, sl])
        act_ref[:, sl] = ((g * jax.nn.sigmoid(g)) * u).astype(BF16)
    x2 = x1 + _dot(act_ref[...], wd_ref[...])
    h3 = (_rms_scale(x2) * gple_ref[...]).astype(BF16)
    gate = jax.nn.sigmoid(_dot(h3, wpg_ref[...]))
    y_ref[...] = x2 + gate * _dot(p_ref[...].astype(BF16), wple_ref[...])


def _tail(x2d, mix2d, p2d, w_o, g_ffn, w_g, w_u, w_d, g_ple, w_ple, w_pg, *, tm, name):
    n = x2d.shape[0]
    row = lambda i: (i, 0)
    const = lambda i: (0, 0)
    resident = functools.partial(pl.BlockSpec, pipeline_mode=pl.Buffered(1))
    return pl.pallas_call(
        _tail_kernel, out_shape=jax.ShapeDtypeStruct((n, D_MODEL), F32), grid=(n // tm,),
        in_specs=[pl.BlockSpec((tm, D_MODEL), row), pl.BlockSpec((tm, D_MODEL), row),
                  pl.BlockSpec((tm, PLE_DIM), row),
                  resident((D_MODEL, D_MODEL), const), resident((1, D_MODEL), const),
                  resident((D_MODEL, D_FF), const), resident((D_MODEL, D_FF), const),
                  resident((D_FF, D_MODEL), const), resident((1, D_MODEL), const),
                  resident((PLE_DIM, D_MODEL), const), resident((D_MODEL, D_MODEL), const)],
        out_specs=pl.BlockSpec((tm, D_MODEL), row),
        scratch_shapes=[pltpu.VMEM((tm, D_FF), BF16)], name=name,
        compiler_params=pltpu.CompilerParams(dimension_semantics=("arbitrary",),
                                             vmem_limit_bytes=VMEM_LIMIT),
    )(x2d, mix2d, p2d, w_o, g_ffn, w_g, w_u, w_d, g_ple, w_ple, w_pg)


def kernel(x_prompt, x_sample, cache_attn_k, cache_attn_v, state_ret, p_prompt, p_sample, w_in, g_mix_norm, g_q_norm, g_k_norm, lam_q, lam_k, g_sub_norm, g_ret_norm, w_o, g_ffn_norm, w_ff_gate, w_ff_up, w_ff_down, g_ple_norm, w_ple, w_ple_gate):
    B, S, D = x_prompt.shape
    DB, T, _ = x_sample.shape
    P = cache_attn_k.shape[2]
    depth = w_in.shape[0]
    assert depth == 1 and D == D_MODEL and S % 256 == 0 and 256 % T == 0
    l = 0
    lam_init = 0.8 - 0.6 * math.exp(-0.3 * l)
    tm = 256

    w_in_b = w_in[l].astype(BF16)
    tail_w = (w_o[l].astype(BF16), g_ffn_norm[l][None, :], w_ff_gate[l].astype(BF16),
              w_ff_up[l].astype(BF16), w_ff_down[l].astype(BF16), g_ple_norm[l][None, :],
              w_ple[l].astype(BF16), w_ple_gate[l].astype(BF16))
    g_mix = g_mix_norm[l][None, :]
    gq = jnp.tile(g_q_norm[l], MXU_N // A_DIM)[None, :]
    gk = jnp.tile(g_k_norm[l], MXU_N // A_DIM)[None, :]
    g_sub = g_sub_norm[l][None, :]
    g_rn = g_ret_norm[l]
    grp = jnp.arange(MXU_N) // A_DIM
    gmat = (grp[:, None] == grp[None, :]).astype(BF16)

    cos_p, sin_p = _rope_tables(jnp.arange(S))
    xp2 = x_prompt.reshape(B * S, D)
    qa, kf, kb, vf, vb, ra, ga, st_p = _inproj(
        xp2, w_in_b, g_mix, gq, gk, cos_p, sin_p, gmat, _ret_tables(CHUNK), g_rn,
        tm=tm, seq=S, state0=None)
    r3 = lambda a: a.reshape(B, S, D)
    mix_p = _attn_prompt(r3(qa), r3(kb), r3(vb), r3(ga), r3(ra), lam_q[l], lam_k[l], g_sub,
                         tq=256, lam_init=lam_init)
    y_p = _tail(xp2, mix_p.reshape(B * S, D), p_prompt[l].reshape(B * S, PLE_DIM), *tail_w,
                tm=tm, name="tail_prompt")

    cos_s, sin_s = _rope_tables(P + jnp.arange(T))
    reps = tm // T
    xs2 = x_sample.reshape(DB * T, D)
    st0 = state_ret[l].reshape(DB, HEADS * R_DK, HEAD_W)
    qa_s, kf_s, kb_s, vf_s, vb_s, ra_s, ga_s, st_s = _inproj(
        xs2, w_in_b, g_mix, gq, gk, jnp.tile(cos_s, (reps, 1)), jnp.tile(sin_s, (reps, 1)), gmat,
        _ret_tables(T), g_rn, tm=tm, seq=T, state0=st0)
    s3 = lambda a: a.reshape(DB, T, D)
    mix_s = _attn_sample(s3(qa_s), cache_attn_k[l].reshape(DB, P, D), cache_attn_v[l].reshape(DB, P, D),
                         s3(kb_s), s3(vb_s), s3(ga_s), s3(ra_s), lam_q[l], lam_k[l], g_sub,
                         lam_init=lam_init)
    y_s = _tail(xs2, mix_s.reshape(DB * T, D), p_sample[l].reshape(DB * T, PLE_DIM), *tail_w,
                tm=tm, name="tail_sample")

    return (y_p.reshape(B, S, D), y_s.reshape(DB, T, D),
            kf.reshape(1, B, S, HEADS, HEAD_W), vf.reshape(1, B, S, HEADS, HEAD_W),
            st_p.reshape(1, B, HEADS, R_DK, HEAD_W),
            kf_s.reshape(1, DB, T, HEADS, HEAD_W), vf_s.reshape(1, DB, T, HEADS, HEAD_W),
            st_s.reshape(1, DB, HEADS, R_DK, HEAD_W))
```

```python
import functools
import math

import jax
import jax.numpy as jnp
from jax import lax
from jax.experimental import pallas as pl
from jax.experimental.pallas import tpu as pltpu

F32 = jnp.float32
BF16 = jnp.bfloat16

D_MODEL = 1024
CHUNK = 64
PLE_DIM = 256
EPS = 1e-6
HEADS = 8
HEAD_W = 128
A_DIM = 64
R_DK = 64
ROPE_BASE = 10000.0
D_FF = 2816
N_PAIR = HEADS // 2

OFF_QA, OFF_KA, OFF_VA = 0, 1024, 2048
OFF_QR, OFF_KR, OFF_VR = 3072, 3584, 4096
OFF_GRET, OFF_GA, OFF_GR = 5120, 6144, 7168
IN_WIDTH = 8192

MXU_N = 256
VMEM_LIMIT = 56 * 1024 * 1024
NEG_BIG = -1e30
Q_SCALE = (A_DIM ** -0.5) * math.log2(math.e)

_NT = (((1,), (1,)), ((), ()))


def _dot(a, b):
    return jnp.dot(a, b, preferred_element_type=F32)


def _dot_nt(a, b):
    return lax.dot_general(a, b, _NT, preferred_element_type=F32)


def _rms_scale(x):
    return x * lax.rsqrt(jnp.mean(x * x, axis=-1, keepdims=True) + EPS)


def _inproj_body(x_ref, w_ref, gmix_ref, gq_ref, gk_ref, cos_ref, sin_ref, gmat_ref,
                 dpair_ref, qdec_ref, kdec_ref, cdec_ref, grn_ref,
                 kf_ref, kb_ref, vf_ref, ra_ref, ga_ref,
                 qs_ref, ks_ref, vs_ref, gs_ref, *, L, get_state, set_state, store_q, store_v):
    tm = x_ref.shape[0]
    x = x_ref[...]
    h = (_rms_scale(x) * gmix_ref[...]).astype(BF16)
    gmat = gmat_ref[...]

    def proj(off, c, width=MXU_N):
        return _dot(h, w_ref[:, off + c * width: off + (c + 1) * width])

    def group_norm(z, g_ref):
        ss = _dot((z * z).astype(BF16), gmat)
        return (z * lax.rsqrt(ss * (1.0 / A_DIM) + EPS)) * g_ref[...]

    for c in range(D_MODEL // MXU_N):
        sl = slice(c * MXU_N, (c + 1) * MXU_N)
        qn = group_norm(proj(OFF_QA, c), gq_ref)
        store_q(sl, qn * Q_SCALE)
        kn = group_norm(proj(OFF_KA, c), gk_ref)
        kf_ref[:, sl] = kn
        kb_ref[:, sl] = kn.astype(BF16)
        v = proj(OFF_VA, c)
        vf_ref[:, sl] = v
        store_v(sl, v)
        vs_ref[:, sl] = proj(OFF_VR, c).astype(BF16)
        g_ret = proj(OFF_GRET, c)
        gate_r = proj(OFF_GR, c)
        gs_ref[:, sl] = (g_ret * jax.nn.sigmoid(g_ret)) * jax.nn.sigmoid(gate_r)
        ga_ref[:, sl] = jax.nn.sigmoid(proj(OFF_GA, c)).astype(BF16)

    lane = lax.broadcasted_iota(jnp.int32, (tm, HEAD_W), 1)
    first_half = (lane % R_DK) < (R_DK // 2)
    cos = cos_ref[...]
    sin = sin_ref[...]

    def rotary(z):
        partner = jnp.where(first_half, pltpu.roll(z, HEAD_W - R_DK // 2, axis=1),
                            pltpu.roll(z, R_DK // 2, axis=1))
        return z * cos + partner * sin

    for c in range(2):
        zq = proj(OFF_QR, c)
        zk = proj(OFF_KR, c)
        for half in range(2):
            hs = slice(half * HEAD_W, (half + 1) * HEAD_W)
            ds = slice((2 * c + half) * HEAD_W, (2 * c + half + 1) * HEAD_W)
            qs_ref[:, ds] = rotary(zq[:, hs]).astype(BF16)
            ks_ref[:, ds] = rotary(zk[:, hs]) * (R_DK ** -0.5)

    lane_l = lax.broadcasted_iota(jnp.int32, (L, HEAD_W), 1)
    even = lane_l < R_DK

    def chunk_body(ci, carry):
        r0 = pl.multiple_of(ci * L, L)
        rows = pl.ds(r0, L)
        for g in range(N_PAIR):
            q2 = qs_ref[rows, g * HEAD_W:(g + 1) * HEAD_W]
            k2 = ks_ref[rows, g * HEAD_W:(g + 1) * HEAD_W]
            v2 = vs_ref[rows, 2 * g * HEAD_W:(2 * g + 2) * HEAD_W]
            zero = jnp.zeros_like(q2)
            qq = jnp.concatenate([jnp.where(even, q2, zero), jnp.where(even, zero, q2)], axis=0)
            a = (_dot_nt(qq, k2.astype(BF16)) * dpair_ref[g]).astype(BF16)
            intra = _dot(a, v2)
            state = get_state(ci, g)
            cross = _dot(qq, state.astype(BF16)) * qdec_ref[g]
            kd = (k2 * kdec_ref[g]).T.astype(BF16)
            kv = _dot(kd, v2)
            kv_pair = jnp.concatenate([kv[:R_DK, :HEAD_W], kv[R_DK:, HEAD_W:]], axis=0)
            set_state(ci, g, cdec_ref[g] * state + kv_pair)
            outs = (intra[:L, :HEAD_W] + cross[:L], intra[L:, HEAD_W:] + cross[L:])
            for par, o in enumerate(outs):
                hh = 2 * g + par
                cols = slice(hh * HEAD_W, (hh + 1) * HEAD_W)
                on = _rms_scale(o) * grn_ref[hh:hh + 1, :]
                ra_ref[rows, cols] = (on * gs_ref[rows, cols]).astype(BF16)
        return carry

    lax.fori_loop(0, tm // L, chunk_body, 0)


def _inproj_prompt_kernel(*refs):
    ins, outs = refs[:13], refs[13:]
    (qt_ref, kf_ref, kb_ref, vf_ref, vt_ref, ra_ref, ga_ref, st_ref,
     qs_ref, ks_ref, vs_ref, gs_ref) = outs

    @pl.when(pl.program_id(1) == 0)
    def _():
        st_ref[...] = jnp.zeros_like(st_ref)

    def get_state(ci, g):
        return st_ref[0, g * HEAD_W:(g + 1) * HEAD_W, :]

    def set_state(ci, g, val):
        st_ref[0, g * HEAD_W:(g + 1) * HEAD_W, :] = val

    def store_q(sl, val):
        qt_ref[0, sl, :] = val.T.astype(BF16)

    def store_v(sl, val):
        vt_ref[0, 0, sl, :] = val.T.astype(BF16)

    _inproj_body(*ins, kf_ref, kb_ref, vf_ref, ra_ref, ga_ref, qs_ref, ks_ref, vs_ref, gs_ref,
                 L=CHUNK, get_state=get_state, set_state=set_state, store_q=store_q, store_v=store_v)


def _inproj_sample_kernel(*refs, L):
    ins, s0_ref, outs = refs[:13], refs[13], refs[14:]
    (qa_ref, kf_ref, kb_ref, vf_ref, vb_ref, ra_ref, ga_ref, st_ref,
     qs_ref, ks_ref, vs_ref, gs_ref) = outs

    def get_state(ci, g):
        return s0_ref[ci, g * HEAD_W:(g + 1) * HEAD_W, :]

    def set_state(ci, g, val):
        st_ref[ci, g * HEAD_W:(g + 1) * HEAD_W, :] = val

    def store_q(sl, val):
        qa_ref[:, sl] = val.astype(BF16)

    def store_v(sl, val):
        vb_ref[:, sl] = val.astype(BF16)

    _inproj_body(*ins, kf_ref, kb_ref, vf_ref, ra_ref, ga_ref, qs_ref, ks_ref, vs_ref, gs_ref,
                 L=L, get_state=get_state, set_state=set_state, store_q=store_q, store_v=store_v)


def _ret_tables(T):
    log_g = jnp.log1p(-jnp.exp2(-5.0 - jnp.arange(HEADS, dtype=F32)))
    i = jnp.arange(T, dtype=F32)
    diff = i[:, None] - i[None, :]
    d_mat = jnp.where(diff >= 0, jnp.exp(log_g[:, None, None] * jnp.maximum(diff, 0.0)), 0.0)
    q_decay = jnp.exp(log_g[None, :] * (i[:, None] + 1.0))
    k_decay = jnp.exp(log_g[None, :] * (T - 1.0 - i[:, None]))
    chunk_decay = jnp.exp(log_g * T)
    dpair = d_mat.reshape(N_PAIR, 2 * T, T)
    qdec = jnp.broadcast_to(q_decay.T.reshape(N_PAIR, 2 * T, 1), (N_PAIR, 2 * T, HEAD_W))
    kdec = jnp.repeat(k_decay.reshape(T, N_PAIR, 2), R_DK, axis=2).transpose(1, 0, 2)
    cdec = jnp.broadcast_to(jnp.repeat(chunk_decay.reshape(N_PAIR, 2), R_DK, axis=1)[:, :, None],
                            (N_PAIR, 2 * R_DK, HEAD_W))
    return dpair, qdec, kdec, cdec


def _rope_tables(pos):
    half = R_DK // 2
    inv_freq = ROPE_BASE ** (-jnp.arange(half, dtype=F32) / half)
    ang = pos.astype(F32)[:, None] * inv_freq[None, :]
    cos, sin = jnp.cos(ang), jnp.sin(ang)
    return jnp.tile(cos, (1, 4)), jnp.tile(jnp.concatenate([-sin, sin], axis=1), (1, 2))


def _inproj(x2d, w_in, g_mix, gq, gk, cos_t, sin_t, gmat, tables, g_rn, *, tm, seq, state0):
    n = x2d.shape[0]
    prompt = state0 is None
    dpair, qdec, kdec, cdec = tables
    nt = seq // tm if prompt else 1
    nb = n // seq if prompt else n // tm
    per_tile = tm // seq if not prompt else 1

    const2 = lambda *_: (0, 0)
    const3 = lambda *_: (0, 0, 0)
    row = lambda b, i: (b * nt + i, 0)
    resident = functools.partial(pl.BlockSpec, pipeline_mode=pl.Buffered(1))
    in_specs = [
        pl.BlockSpec((tm, D_MODEL), row),
        resident((D_MODEL, IN_WIDTH), const2),
        resident((1, D_MODEL), const2),
        resident((1, MXU_N), const2),
        resident((1, MXU_N), const2),
        pl.BlockSpec((tm, HEAD_W), (lambda b, i: (i, 0)) if prompt else const2),
        pl.BlockSpec((tm, HEAD_W), (lambda b, i: (i, 0)) if prompt else const2),
        resident((MXU_N, MXU_N), const2),
        resident(dpair.shape, const3),
        resident(qdec.shape, const3),
        resident(kdec.shape, const3),
        resident(cdec.shape, const3),
        resident((HEADS, HEAD_W), const2),
    ]
    args = [x2d, w_in, g_mix, gq, gk, cos_t, sin_t, gmat, dpair, qdec, kdec, cdec, g_rn]
    st_rows = 2 * R_DK * N_PAIR
    if prompt:
        st_spec = pl.BlockSpec((1, st_rows, HEAD_W), lambda b, i: (b, 0, 0))
        st_shape = jax.ShapeDtypeStruct((nb, st_rows, HEAD_W), F32)
        kernel = _inproj_prompt_kernel
        L = CHUNK
    else:
        st_spec = pl.BlockSpec((per_tile, st_rows, HEAD_W), lambda b, i: (b, 0, 0))
        st_shape = jax.ShapeDtypeStruct(state0.shape, F32)
        in_specs.append(st_spec)
        args.append(state0)
        kernel = functools.partial(_inproj_sample_kernel, L=seq)
        L = seq
    tok = lambda dt: jax.ShapeDtypeStruct((n, D_MODEL), dt)
    tok_spec = pl.BlockSpec((tm, D_MODEL), row)
    out_shape = [tok(BF16), tok(F32), tok(BF16), tok(F32), tok(BF16), tok(BF16), tok(BF16), st_shape]
    out_specs = [tok_spec] * 7 + [st_spec]
    if prompt:
        out_shape[0] = jax.ShapeDtypeStruct((nb, D_MODEL, seq), BF16)
        out_specs[0] = pl.BlockSpec((1, D_MODEL, tm), lambda b, i: (b, 0, i))
        out_shape[4] = jax.ShapeDtypeStruct((nb, nt, D_MODEL, tm), BF16)
        out_specs[4] = pl.BlockSpec((1, 1, D_MODEL, tm), lambda b, i: (b, i, 0, 0))
    scratch = [pltpu.VMEM((tm, HEADS * R_DK), BF16), pltpu.VMEM((tm, HEADS * R_DK), F32),
               pltpu.VMEM((tm, D_MODEL), BF16), pltpu.VMEM((tm, D_MODEL), F32)]
    return pl.pallas_call(
        kernel, out_shape=out_shape, grid=(nb, nt), in_specs=in_specs, out_specs=out_specs,
        scratch_shapes=scratch, name="inproj_prompt" if prompt else "inproj_sample",
        compiler_params=pltpu.CompilerParams(dimension_semantics=("arbitrary", "arbitrary"),
                                             vmem_limit_bytes=VMEM_LIMIT),
    )(*args)


def _lam(lamq_ref, lamk_ref, lam_init):
    e = jnp.exp(jnp.sum(lamq_ref[...] * lamk_ref[...], axis=-1, keepdims=True))
    return e[0:1, :] - e[1:2, :] + lam_init


def _split_maps(q):
    lane = lax.broadcasted_iota(jnp.int32, q.shape, 1)
    zero = jnp.zeros_like(q)
    return jnp.concatenate([jnp.where(lane < A_DIM, q, zero), jnp.where(lane < A_DIM, zero, q)], axis=0)


def _merge_out(o, gsub_ref, ga, ra, lam_init):
    on = (_rms_scale(o) * gsub_ref[...]) * (1.0 - lam_init)
    return (ga.astype(F32) * on + ra.astype(F32)).astype(BF16)


def _attn_prompt_kernel(qt_ref, k_ref, vt_ref, ga_ref, ra_ref, lamq_ref, lamk_ref, gsub_ref, o_ref,
                        acc_ref, s0_ref, s1_ref, p0_ref, p1_ref, *, tq, tk, lam_init):
    assert tq == 2 * tk
    i = pl.program_id(2)
    qt = qt_ref[0]
    feat = lax.broadcasted_iota(jnp.int32, qt.shape, 0)
    zero = jnp.zeros_like(qt)
    qqt = jnp.concatenate([jnp.where(feat < A_DIM, qt, zero), jnp.where(feat < A_DIM, zero, qt)],
                          axis=1)
    s_refs, p_refs = (s0_ref, s1_ref), (p0_ref, p1_ref)

    def scores(t, slot):
        rows = pl.ds(pl.multiple_of(t * tk, tk), tk)
        s_refs[slot][...] = _dot(k_ref[0, rows, :], qqt)

    def softmax(slot, carry, mask):
        m, l, _ = carry
        st = s_refs[slot][...]
        if mask is not None:
            st = jnp.where(mask, st, NEG_BIG)
        m_new = jnp.maximum(m, jnp.max(st, axis=0, keepdims=True))
        alpha = jnp.exp2(m - m_new)
        p = jnp.exp2(st - m_new)
        l = alpha * l + jnp.sum(p, axis=0, keepdims=True)
        p_refs[slot][...] = p.astype(BF16)
        return m_new, l, alpha

    def fold_values(t, slot, alpha):
        acc_ref[...] = alpha * acc_ref[...] + _dot(vt_ref[0, t], p_refs[slot][...])

    def step(t, slot, carry, mask=None, last=False):
        fold_values(jnp.maximum(t - 1, 0), 1 - slot, carry[2])
        if not last:
            scores(t + 1, 1 - slot)
        return softmax(slot, carry, mask)

    acc_ref[...] = jnp.zeros_like(acc_ref)
    p1_ref[...] = jnp.zeros_like(p1_ref)
    scores(0, 0)
    carry = (jnp.full((1, 2 * tq), NEG_BIG, F32), jnp.zeros((1, 2 * tq), F32),
             jnp.ones((1, 2 * tq), F32))

    def pair(jj, c):
        return step(2 * jj + 1, 1, step(2 * jj, 0, c))

    carry = lax.fori_loop(0, i, pair, carry)
    key = lax.broadcasted_iota(jnp.int32, (tk, 2 * tq), 0)
    qry = lax.broadcasted_iota(jnp.int32, (tk, 2 * tq), 1) % tq
    carry = step(2 * i, 0, carry, mask=(key // CHUNK) <= (qry // CHUNK))
    carry = step(2 * i + 1, 1, carry, mask=((tk + key) // CHUNK) <= (qry // CHUNK), last=True)
    _, l, alpha = carry
    fold_values(2 * i + 1, 1, alpha)
    lam = _lam(lamq_ref, lamk_ref, lam_init)
    acc = acc_ref[...]
    ot = acc[:, :tq] / l[:, :tq] - lam * (acc[:, tq:] / l[:, tq:])
    o_ref[0] = _merge_out(ot.T, gsub_ref, ga_ref[0], ra_ref[0], lam_init)


def _attn_prompt(qt, kb, vt, ga, ra, lam_q, lam_k, g_sub, *, tq, lam_init):
    B, S, _ = kb.shape
    nt, tk = vt.shape[1], vt.shape[3]
    blk = pl.BlockSpec((1, tq, HEAD_W), lambda b, h, i: (b, i, h))
    small = lambda shape: pl.BlockSpec(shape, lambda b, h, i: (0, 0))
    return pl.pallas_call(
        functools.partial(_attn_prompt_kernel, tq=tq, tk=tk, lam_init=lam_init),
        out_shape=jax.ShapeDtypeStruct((B, S, D_MODEL), BF16),
        grid=(B, HEADS, S // tq),
        in_specs=[pl.BlockSpec((1, HEAD_W, tq), lambda b, h, i: (b, h, i)),
                  pl.BlockSpec((1, S, HEAD_W), lambda b, h, i: (b, 0, h)),
                  pl.BlockSpec((1, nt, HEAD_W, tk), lambda b, h, i: (b, 0, h, 0)),
                  blk, blk, small((2, A_DIM)), small((2, A_DIM)), small((1, HEAD_W))],
        out_specs=blk, name="attn_prompt",
        scratch_shapes=[pltpu.VMEM((HEAD_W, 2 * tq), F32),
                        pltpu.VMEM((tk, 2 * tq), F32), pltpu.VMEM((tk, 2 * tq), F32),
                        pltpu.VMEM((tk, 2 * tq), BF16), pltpu.VMEM((tk, 2 * tq), BF16)],
        compiler_params=pltpu.CompilerParams(dimension_semantics=("arbitrary",) * 3,
                                             vmem_limit_bytes=VMEM_LIMIT),
    )(qt, kb, vt, ga, ra, lam_q, lam_k, g_sub)


def _attn_sample_kernel(q_ref, kc_ref, vc_ref, kn_ref, vn_ref, ga_ref, ra_ref, lamq_ref, lamk_ref,
                        gsub_ref, o_ref, *, lam_init):
    t = q_ref.shape[1]
    n_past = kc_ref.shape[1] // HEADS
    lam = _lam(lamq_ref, lamk_ref, lam_init)
    for h in range(HEADS):
        sl = slice(h * HEAD_W, (h + 1) * HEAD_W)
        head_rows = pl.ds(h, n_past, stride=HEADS)
        qq = _split_maps(q_ref[0, :, sl])
        s_c = _dot_nt(qq, kc_ref[0, head_rows, :].astype(BF16))
        s_n = _dot_nt(qq, kn_ref[0, :, sl])
        m = jnp.maximum(jnp.max(s_c, axis=-1, keepdims=True), jnp.max(s_n, axis=-1, keepdims=True))
        p_c = jnp.exp2(s_c - m)
        p_n = jnp.exp2(s_n - m)
        l = jnp.sum(p_c, axis=-1, keepdims=True) + jnp.sum(p_n, axis=-1, keepdims=True)
        acc = (_dot(p_c.astype(BF16), vc_ref[0, head_rows, :].astype(BF16))
               + _dot(p_n.astype(BF16), vn_ref[0, :, sl]))
        o = acc[:t] / l[:t] - lam * (acc[t:] / l[t:])
        o_ref[0, :, sl] = _merge_out(o, gsub_ref, ga_ref[0, :, sl], ra_ref[0, :, sl], lam_init)


def _attn_sample(qa, kc, vc, kn, vn, ga, ra, lam_q, lam_k, g_sub, *, lam_init):
    DB, T, _ = qa.shape
    blk = pl.BlockSpec((1, T, D_MODEL), lambda b: (b, 0, 0))
    cache = pl.BlockSpec((1,) + kc.shape[1:], lambda b: (b, 0, 0))
    small = lambda shape: pl.BlockSpec(shape, lambda b: (0, 0))
    return pl.pallas_call(
        functools.partial(_attn_sample_kernel, lam_init=lam_init),
        out_shape=jax.ShapeDtypeStruct((DB, T, D_MODEL), BF16),
        grid=(DB,),
        in_specs=[blk, cache, cache, blk, blk, blk, blk, small((2, A_DIM)), small((2, A_DIM)),
                  small((1, HEAD_W))],
        out_specs=blk, name="attn_sample",
        compiler_params=pltpu.CompilerParams(dimension_semantics=("arbitrary",),
                                             vmem_limit_bytes=VMEM_LIMIT),
    )(qa, kc, vc, kn, vn, ga, ra, lam_q, lam_k, g_sub)


def _tail_kernel(x_ref, mix_ref, p_ref, wo_ref, gffn_ref, wg_ref, wu_ref, wd_ref, gple_ref, wple_ref,
                 wpg_ref, y_ref, act_ref):
    x1 = x_ref[...] + _dot(mix_ref[...], wo_ref[...])
    h = (_rms_scale(x1) * gffn_ref[...]).astype(BF16)
    for c in range(D_FF // MXU_N):
        sl = slice(c * MXU_N, (c + 1) * MXU_N)
        g = _dot(h, wg_ref[:, sl])
        u = _dot(h, wu_ref[:, sl])
        act_ref[:, sl] = ((g * jax.nn.sigmoid(g)) * u).astype(BF16)
    x2 = x1 + _dot(act_ref[...], wd_ref[...])
    h3 = (_rms_scale(x2) * gple_ref[...]).astype(BF16)
    gate = jax.nn.sigmoid(_dot(h3, wpg_ref[...]))
    y_ref[...] = x2 + gate * _dot(p_ref[...].astype(BF16), wple_ref[...])


def _tail(x2d, mix2d, p2d, w_o, g_ffn, w_g, w_u, w_d, g_ple, w_ple, w_pg, *, tm, name):
    n = x2d.shape[0]
    row = lambda i: (i, 0)
    const = lambda i: (0, 0)
    resident = functools.partial(pl.BlockSpec, pipeline_mode=pl.Buffered(1))
    return pl.pallas_call(
        _tail_kernel, out_shape=jax.ShapeDtypeStruct((n, D_MODEL), F32), grid=(n // tm,),
        in_specs=[pl.BlockSpec((tm, D_MODEL), row), pl.BlockSpec((tm, D_MODEL), row),
                  pl.BlockSpec((tm, PLE_DIM), row),
                  resident((D_MODEL, D_MODEL), const), resident((1, D_MODEL), const),
                  resident((D_MODEL, D_FF), const), resident((D_MODEL, D_FF), const),
                  resident((D_FF, D_MODEL), const), resident((1, D_MODEL), const),
                  resident((PLE_DIM, D_MODEL), const), resident((D_MODEL, D_MODEL), const)],
        out_specs=pl.BlockSpec((tm, D_MODEL), row),
        scratch_shapes=[pltpu.VMEM((tm, D_FF), BF16)], name=name,
        compiler_params=pltpu.CompilerParams(dimension_semantics=("arbitrary",),
                                             vmem_limit_bytes=VMEM_LIMIT),
    )(x2d, mix2d, p2d, w_o, g_ffn, w_g, w_u, w_d, g_ple, w_ple, w_pg)


def kernel(x_prompt, x_sample, cache_attn_k, cache_attn_v, state_ret, p_prompt, p_sample, w_in, g_mix_norm, g_q_norm, g_k_norm, lam_q, lam_k, g_sub_norm, g_ret_norm, w_o, g_ffn_norm, w_ff_gate, w_ff_up, w_ff_down, g_ple_norm, w_ple, w_ple_gate):
    B, S, D = x_prompt.shape
    DB, T, _ = x_sample.shape
    P = cache_attn_k.shape[2]
    depth = w_in.shape[0]
    assert depth == 1 and D == D_MODEL and S % 512 == 0 and 256 % T == 0
    l = 0
    lam_init = 0.8 - 0.6 * math.exp(-0.3 * l)
    tm = 256

    w_in_b = w_in[l].astype(BF16)
    tail_w = (w_o[l].astype(BF16), g_ffn_norm[l][None, :], w_ff_gate[l].astype(BF16),
              w_ff_up[l].astype(BF16), w_ff_down[l].astype(BF16), g_ple_norm[l][None, :],
              w_ple[l].astype(BF16), w_ple_gate[l].astype(BF16))
    g_mix = g_mix_norm[l][None, :]
    gq = jnp.tile(g_q_norm[l], MXU_N // A_DIM)[None, :]
    gk = jnp.tile(g_k_norm[l], MXU_N // A_DIM)[None, :]
    g_sub = g_sub_norm[l][None, :]
    g_rn = g_ret_norm[l]
    grp = jnp.arange(MXU_N) // A_DIM
    gmat = (grp[:, None] == grp[None, :]).astype(BF16)

    cos_p, sin_p = _rope_tables(jnp.arange(S))
    xp2 = x_prompt.reshape(B * S, D)
    qt, kf, kb, vf, vt, ra, ga, st_p = _inproj(
        xp2, w_in_b, g_mix, gq, gk, cos_p, sin_p, gmat, _ret_tables(CHUNK), g_rn,
        tm=tm, seq=S, state0=None)
    r3 = lambda a: a.reshape(B, S, D)
    mix_p = _attn_prompt(qt, r3(kb), vt, r3(ga), r3(ra), lam_q[l], lam_k[l], g_sub,
                         tq=512, lam_init=lam_init)
    y_p = _tail(xp2, mix_p.reshape(B * S, D), p_prompt[l].reshape(B * S, PLE_DIM), *tail_w,
                tm=tm, name="tail_prompt")

    cos_s, sin_s = _rope_tables(P + jnp.arange(T))
    reps = tm // T
    xs2 = x_sample.reshape(DB * T, D)
    st0 = state_ret[l].reshape(DB, HEADS * R_DK, HEAD_W)
    qa_s, kf_s, kb_s, vf_s, vb_s, ra_s, ga_s, st_s = _inproj(
        xs2, w_in_b, g_mix, gq, gk, jnp.tile(cos_s, (reps, 1)), jnp.tile(sin_s, (reps, 1)), gmat,
        _ret_tables(T), g_rn, tm=tm, seq=T, state0=st0)
    s3 = lambda a: a.reshape(DB, T, D)
    mix_s = _attn_sample(s3(qa_s), cache_attn_k[l].reshape(DB, P * HEADS, HEAD_W),
                         cache_attn_v[l].reshape(DB, P * HEADS, HEAD_W),
                         s3(kb_s), s3(vb_s), s3(ga_s), s3(ra_s), lam_q[l], lam_k[l], g_sub,
                         lam_init=lam_init)
    y_s = _tail(xs2, mix_s.reshape(DB * T, D), p_sample[l].reshape(DB * T, PLE_DIM), *tail_w,
                tm=tm, name="tail_sample")

    return (y_p.reshape(B, S, D), y_s.reshape(DB, T, D),
            kf.reshape(1, B, S, HEADS, HEAD_W), vf.reshape(1, B, S, HEADS, HEAD_W),
            st_p.reshape(1, B, HEADS, R_DK, HEAD_W),
            kf_s.reshape(1, DB, T, HEADS, HEAD_W), vf_s.reshape(1, DB, T, HEADS, HEAD_W),
            st_s.reshape(1, DB, HEADS, R_DK, HEAD_W))
```

```python
import functools
import math

import jax
import jax.numpy as jnp
from jax import lax
from jax.experimental import pallas as pl
from jax.experimental.pallas import tpu as pltpu

F32 = jnp.float32
BF16 = jnp.bfloat16

D_MODEL = 1024
CHUNK = 64
PLE_DIM = 256
EPS = 1e-6
HEADS = 8
HEAD_W = 128
A_DIM = 64
R_DK = 64
ROPE_BASE = 10000.0
D_FF = 2816
N_PAIR = HEADS // 2

OFF_QA, OFF_KA, OFF_VA = 0, 1024, 2048
OFF_QR, OFF_KR, OFF_VR = 3072, 3584, 4096
OFF_GRET, OFF_GA, OFF_GR = 5120, 6144, 7168
IN_WIDTH = 8192

MXU_N = 256
VMEM_LIMIT = 56 * 1024 * 1024
NEG_BIG = -1e30
SUM_ROWS = 16
ATT_TK = 256
EXP2_SAFE_RANGE = 60.0
Q_SCALE = (A_DIM ** -0.5) * math.log2(math.e)

_NT = (((1,), (1,)), ((), ()))


def _dot(a, b):
    return jnp.dot(a, b, preferred_element_type=F32)


def _dot_nt(a, b):
    return lax.dot_general(a, b, _NT, preferred_element_type=F32)


def _rms_scale(x):
    return x * lax.rsqrt(jnp.mean(x * x, axis=-1, keepdims=True) + EPS)


def _inproj_body(x_ref, w_ref, gmix_ref, gq_ref, gk_ref, cos_ref, sin_ref, gmat_ref,
                 dpair_ref, qdec_ref, kdec_ref, cdec_ref, grn_ref,
                 kf_ref, kb_ref, vf_ref, ra_ref, ga_ref,
                 qs_ref, ks_ref, vs_ref, gs_ref, *, L, get_state, set_state, store_q, store_v,
                 unroll_chunks):
    tm = x_ref.shape[0]
    x = x_ref[...]
    h = (_rms_scale(x) * gmix_ref[...]).astype(BF16)
    gmat = gmat_ref[...]

    def proj(off, c, width=MXU_N):
        return _dot(h, w_ref[:, off + c * width: off + (c + 1) * width])

    def group_norm(z, g_ref):
        ss = _dot((z * z).astype(BF16), gmat)
        return (z * lax.rsqrt(ss * (1.0 / A_DIM) + EPS)) * g_ref[...]

    for c in range(D_MODEL // MXU_N):
        sl = slice(c * MXU_N, (c + 1) * MXU_N)
        qn = group_norm(proj(OFF_QA, c), gq_ref)
        store_q(sl, qn * Q_SCALE)
        kn = group_norm(proj(OFF_KA, c), gk_ref)
        kf_ref[:, sl] = kn
        kb_ref[:, sl] = kn.astype(BF16)
        v = proj(OFF_VA, c)
        vf_ref[:, sl] = v
        store_v(sl, v)
        vs_ref[:, sl] = proj(OFF_VR, c).astype(BF16)
        g_ret = proj(OFF_GRET, c)
        gate_r = proj(OFF_GR, c)
        gs_ref[:, sl] = (g_ret * jax.nn.sigmoid(g_ret)) * jax.nn.sigmoid(gate_r)
        ga_ref[:, sl] = jax.nn.sigmoid(proj(OFF_GA, c)).astype(BF16)

    lane = lax.broadcasted_iota(jnp.int32, (tm, HEAD_W), 1)
    first_half = (lane % R_DK) < (R_DK // 2)
    cos = cos_ref[...]
    sin = sin_ref[...]

    def rotary(z):
        partner = jnp.where(first_half, pltpu.roll(z, HEAD_W - R_DK // 2, axis=1),
                            pltpu.roll(z, R_DK // 2, axis=1))
        return z * cos + partner * sin

    for c in range(2):
        zq = proj(OFF_QR, c)
        zk = proj(OFF_KR, c)
        for half in range(2):
            hs = slice(half * HEAD_W, (half + 1) * HEAD_W)
            ds = slice((2 * c + half) * HEAD_W, (2 * c + half + 1) * HEAD_W)
            qs_ref[:, ds] = rotary(zq[:, hs]).astype(BF16)
            ks_ref[:, ds] = rotary(zk[:, hs]) * (R_DK ** -0.5)

    lane_l = lax.broadcasted_iota(jnp.int32, (L, HEAD_W), 1)
    even = lane_l < R_DK

    def chunk_body(ci, carry):
        r0 = pl.multiple_of(ci * L, L)
        rows = pl.ds(r0, L)
        for g in range(N_PAIR):
            q2 = qs_ref[rows, g * HEAD_W:(g + 1) * HEAD_W]
            k2 = ks_ref[rows, g * HEAD_W:(g + 1) * HEAD_W]
            v2 = vs_ref[rows, 2 * g * HEAD_W:(2 * g + 2) * HEAD_W]
            zero = jnp.zeros_like(q2)
            qq = jnp.concatenate([jnp.where(even, q2, zero), jnp.where(even, zero, q2)], axis=0)
            a = (_dot_nt(qq, k2.astype(BF16)) * dpair_ref[g]).astype(BF16)
            intra = _dot(a, v2)
            state = get_state(ci, g)
            cross = _dot(qq, state.astype(BF16)) * qdec_ref[g]
            kd = (k2 * kdec_ref[g]).T.astype(BF16)
            kv = _dot(kd, v2)
            kv_pair = jnp.concatenate([kv[:R_DK, :HEAD_W], kv[R_DK:, HEAD_W:]], axis=0)
            set_state(ci, g, cdec_ref[g] * state + kv_pair)
            outs = (intra[:L, :HEAD_W] + cross[:L], intra[L:, HEAD_W:] + cross[L:])
            for par, o in enumerate(outs):
                hh = 2 * g + par
                cols = slice(hh * HEAD_W, (hh + 1) * HEAD_W)
                on = _rms_scale(o) * grn_ref[hh:hh + 1, :]
                ra_ref[rows, cols] = (on * gs_ref[rows, cols]).astype(BF16)
        return carry

    lax.fori_loop(0, tm // L, chunk_body, 0, unroll=unroll_chunks)


def _inproj_prompt_kernel(*refs):
    ins, outs = refs[:13], refs[13:]
    (qt_ref, kf_ref, kb_ref, vf_ref, vt_ref, ra_ref, ga_ref, st_ref,
     qs_ref, ks_ref, vs_ref, gs_ref) = outs

    @pl.when(pl.program_id(1) == 0)
    def _():
        st_ref[...] = jnp.zeros_like(st_ref)

    def get_state(ci, g):
        return st_ref[0, g * HEAD_W:(g + 1) * HEAD_W, :]

    def set_state(ci, g, val):
        st_ref[0, g * HEAD_W:(g + 1) * HEAD_W, :] = val

    def store_q(sl, val):
        qt_ref[0, sl, :] = val.T.astype(BF16)

    def store_v(sl, val):
        for kb_i in range(val.shape[0] // ATT_TK):
            vt_ref[0, kb_i, sl, :] = val[kb_i * ATT_TK:(kb_i + 1) * ATT_TK].T.astype(BF16)

    _inproj_body(*ins, kf_ref, kb_ref, vf_ref, ra_ref, ga_ref, qs_ref, ks_ref, vs_ref, gs_ref,
                 L=CHUNK, get_state=get_state, set_state=set_state, store_q=store_q, store_v=store_v,
                 unroll_chunks=True)


def _inproj_sample_kernel(*refs, L):
    ins, s0_ref, outs = refs[:13], refs[13], refs[14:]
    (qa_ref, kf_ref, kb_ref, vf_ref, vb_ref, ra_ref, ga_ref, st_ref,
     qs_ref, ks_ref, vs_ref, gs_ref) = outs

    def get_state(ci, g):
        return s0_ref[ci, g * HEAD_W:(g + 1) * HEAD_W, :]

    def set_state(ci, g, val):
        st_ref[ci, g * HEAD_W:(g + 1) * HEAD_W, :] = val

    def store_q(sl, val):
        qa_ref[:, sl] = val.astype(BF16)

    def store_v(sl, val):
        vb_ref[:, sl] = val.astype(BF16)

    _inproj_body(*ins, kf_ref, kb_ref, vf_ref, ra_ref, ga_ref, qs_ref, ks_ref, vs_ref, gs_ref,
                 L=L, get_state=get_state, set_state=set_state, store_q=store_q, store_v=store_v,
                 unroll_chunks=False)


def _ret_tables(T):
    log_g = jnp.log1p(-jnp.exp2(-5.0 - jnp.arange(HEADS, dtype=F32)))
    i = jnp.arange(T, dtype=F32)
    diff = i[:, None] - i[None, :]
    d_mat = jnp.where(diff >= 0, jnp.exp(log_g[:, None, None] * jnp.maximum(diff, 0.0)), 0.0)
    q_decay = jnp.exp(log_g[None, :] * (i[:, None] + 1.0))
    k_decay = jnp.exp(log_g[None, :] * (T - 1.0 - i[:, None]))
    chunk_decay = jnp.exp(log_g * T)
    dpair = d_mat.reshape(N_PAIR, 2 * T, T)
    qdec = jnp.broadcast_to(q_decay.T.reshape(N_PAIR, 2 * T, 1), (N_PAIR, 2 * T, HEAD_W))
    kdec = jnp.repeat(k_decay.reshape(T, N_PAIR, 2), R_DK, axis=2).transpose(1, 0, 2)
    cdec = jnp.broadcast_to(jnp.repeat(chunk_decay.reshape(N_PAIR, 2), R_DK, axis=1)[:, :, None],
                            (N_PAIR, 2 * R_DK, HEAD_W))
    return dpair, qdec, kdec, cdec


def _rope_tables(pos):
    half = R_DK // 2
    inv_freq = ROPE_BASE ** (-jnp.arange(half, dtype=F32) / half)
    ang = pos.astype(F32)[:, None] * inv_freq[None, :]
    cos, sin = jnp.cos(ang), jnp.sin(ang)
    return jnp.tile(cos, (1, 4)), jnp.tile(jnp.concatenate([-sin, sin], axis=1), (1, 2))


def _inproj(x2d, w_in, g_mix, gq, gk, cos_t, sin_t, gmat, tables, g_rn, *, tm, seq, state0):
    n = x2d.shape[0]
    prompt = state0 is None
    dpair, qdec, kdec, cdec = tables
    nt = seq // tm if prompt else 1
    nb = n // seq if prompt else n // tm
    per_tile = tm // seq if not prompt else 1

    const2 = lambda *_: (0, 0)
    const3 = lambda *_: (0, 0, 0)
    row = lambda b, i: (b * nt + i, 0)
    resident = functools.partial(pl.BlockSpec, pipeline_mode=pl.Buffered(1))
    in_specs = [
        pl.BlockSpec((tm, D_MODEL), row),
        resident((D_MODEL, IN_WIDTH), const2),
        resident((1, D_MODEL), const2),
        resident((1, MXU_N), const2),
        resident((1, MXU_N), const2),
        pl.BlockSpec((tm, HEAD_W), (lambda b, i: (i, 0)) if prompt else const2),
        pl.BlockSpec((tm, HEAD_W), (lambda b, i: (i, 0)) if prompt else const2),
        resident((MXU_N, MXU_N), const2),
        resident(dpair.shape, const3),
        resident(qdec.shape, const3),
        resident(kdec.shape, const3),
        resident(cdec.shape, const3),
        resident((HEADS, HEAD_W), const2),
    ]
    args = [x2d, w_in, g_mix, gq, gk, cos_t, sin_t, gmat, dpair, qdec, kdec, cdec, g_rn]
    st_rows = 2 * R_DK * N_PAIR
    if prompt:
        st_spec = pl.BlockSpec((1, st_rows, HEAD_W), lambda b, i: (b, 0, 0))
        st_shape = jax.ShapeDtypeStruct((nb, st_rows, HEAD_W), F32)
        kernel = _inproj_prompt_kernel
        L = CHUNK
    else:
        st_spec = pl.BlockSpec((per_tile, st_rows, HEAD_W), lambda b, i: (b, 0, 0))
        st_shape = jax.ShapeDtypeStruct(state0.shape, F32)
        in_specs.append(st_spec)
        args.append(state0)
        kernel = functools.partial(_inproj_sample_kernel, L=seq)
        L = seq
    tok = lambda dt: jax.ShapeDtypeStruct((n, D_MODEL), dt)
    tok_spec = pl.BlockSpec((tm, D_MODEL), row)
    out_shape = [tok(BF16), tok(F32), tok(BF16), tok(F32), tok(BF16), tok(BF16), tok(BF16), st_shape]
    out_specs = [tok_spec] * 7 + [st_spec]
    if prompt:
        out_shape[0] = jax.ShapeDtypeStruct((nb, D_MODEL, seq), BF16)
        out_specs[0] = pl.BlockSpec((1, D_MODEL, tm), lambda b, i: (b, 0, i))
        out_shape[4] = jax.ShapeDtypeStruct((nb, seq // ATT_TK, D_MODEL, ATT_TK), BF16)
        out_specs[4] = pl.BlockSpec((1, tm // ATT_TK, D_MODEL, ATT_TK), lambda b, i: (b, i, 0, 0))
    scratch = [pltpu.VMEM((tm, HEADS * R_DK), BF16), pltpu.VMEM((tm, HEADS * R_DK), F32),
               pltpu.VMEM((tm, D_MODEL), BF16), pltpu.VMEM((tm, D_MODEL), F32)]
    return pl.pallas_call(
        kernel, out_shape=out_shape, grid=(nb, nt), in_specs=in_specs, out_specs=out_specs,
        scratch_shapes=scratch, name="inproj_prompt" if prompt else "inproj_sample",
        compiler_params=pltpu.CompilerParams(dimension_semantics=("arbitrary", "arbitrary"),
                                             vmem_limit_bytes=VMEM_LIMIT),
    )(*args)


def _lam(lamq_ref, lamk_ref, lam_init):
    e = jnp.exp(jnp.sum(lamq_ref[...] * lamk_ref[...], axis=-1, keepdims=True))
    return e[0:1, :] - e[1:2, :] + lam_init


def _split_maps(q):
    lane = lax.broadcasted_iota(jnp.int32, q.shape, 1)
    zero = jnp.zeros_like(q)
    return jnp.concatenate([jnp.where(lane < A_DIM, q, zero), jnp.where(lane < A_DIM, zero, q)], axis=0)


def _merge_out(o, gsub_ref, ga, ra, lam_init):
    on = (_rms_scale(o) * gsub_ref[...]) * (1.0 - lam_init)
    return (ga.astype(F32) * on + ra.astype(F32)).astype(BF16)


def _attn_prompt_body(qt_ref, k_ref, vt_ref, ga_ref, ra_ref, lamq_ref, lamk_ref, gsub_ref, o_ref,
                      acc_ref, s0_ref, s1_ref, p0_ref, p1_ref, *, tq, tk, lam_init, bounded):
    assert tq == 2 * tk
    i = pl.program_id(2)
    qt = qt_ref[0]
    feat = lax.broadcasted_iota(jnp.int32, qt.shape, 0)
    zero = jnp.zeros_like(qt)
    qqt = jnp.concatenate([jnp.where(feat < A_DIM, qt, zero), jnp.where(feat < A_DIM, zero, qt)],
                          axis=1)
    s_refs, p_refs = (s0_ref, s1_ref), (p0_ref, p1_ref)

    def scores(t, slot):
        rows = pl.ds(pl.multiple_of(t * tk, tk), tk)
        s_refs[slot][...] = _dot(k_ref[0, rows, :], qqt)

    def softmax(slot, carry, mask):
        st = s_refs[slot][...]
        if mask is not None:
            st = jnp.where(mask, st, NEG_BIG)
        if bounded:
            p_refs[slot][...] = jnp.exp2(st).astype(BF16)
            return carry
        m_new = jnp.maximum(carry[0], jnp.max(st, axis=0, keepdims=True))
        p_refs[slot][...] = jnp.exp2(st - m_new).astype(BF16)
        return m_new, jnp.exp2(carry[0] - m_new)

    ones_rows = jnp.ones((SUM_ROWS, tk), BF16)

    def fold_values(t, slot, alpha):
        vt1 = jnp.concatenate([vt_ref[0, t], ones_rows], axis=0)
        pv = _dot(vt1, p_refs[slot][...])
        acc_ref[...] = acc_ref[...] + pv if bounded else alpha * acc_ref[...] + pv

    def step(t, slot, carry, mask=None, last=False):
        fold_values(jnp.maximum(t - 1, 0), 1 - slot, carry[1])
        if not last:
            scores(t + 1, 1 - slot)
        return softmax(slot, carry, mask)

    acc_ref[...] = jnp.zeros_like(acc_ref)
    p1_ref[...] = jnp.zeros_like(p1_ref)
    scores(0, 0)
    carry = (jnp.full((1, 2 * tq), NEG_BIG, F32), jnp.ones((1, 2 * tq), F32))

    def pair(jj, c):
        return step(2 * jj + 1, 1, step(2 * jj, 0, c))

    def quad(jj, c):
        return pair(2 * jj + 1, pair(2 * jj, c))

    carry = lax.fori_loop(0, lax.shift_right_logical(i, 1), quad, carry)
    carry = lax.cond((i & 1) == 1, lambda c: pair(i - 1, c), lambda c: c, carry)
    key = lax.broadcasted_iota(jnp.int32, (tk, 2 * tq), 0)
    qry = lax.broadcasted_iota(jnp.int32, (tk, 2 * tq), 1) % tq
    carry = step(2 * i, 0, carry, mask=(key // CHUNK) <= (qry // CHUNK))
    carry = step(2 * i + 1, 1, carry, mask=((tk + key) // CHUNK) <= (qry // CHUNK), last=True)
    fold_values(2 * i + 1, 1, carry[1])
    lam = _lam(lamq_ref, lamk_ref, lam_init)
    acc = acc_ref[:HEAD_W, :]
    l = acc_ref[HEAD_W:HEAD_W + 1, :]
    ot = acc[:, :tq] / l[:, :tq] - lam * (acc[:, tq:] / l[:, tq:])
    o_ref[0] = _merge_out(ot.T, gsub_ref, ga_ref[0], ra_ref[0], lam_init)


def _attn_prompt_kernel(bound_ref, *refs, **static):
    bounded = bound_ref[0, 0] <= EXP2_SAFE_RANGE
    pl.when(bounded)(lambda: _attn_prompt_body(*refs, bounded=True, **static))
    pl.when(jnp.logical_not(bounded))(lambda: _attn_prompt_body(*refs, bounded=False, **static))


def _attn_prompt(bound, qt, kb, vt, ga, ra, lam_q, lam_k, g_sub, *, tq, lam_init):
    B, S, _ = kb.shape
    nt, tk = vt.shape[1], vt.shape[3]
    blk = pl.BlockSpec((1, tq, HEAD_W), lambda b, h, i: (b, i, h))
    small = lambda shape: pl.BlockSpec(shape, lambda b, h, i: (0, 0))
    return pl.pallas_call(
        functools.partial(_attn_prompt_kernel, tq=tq, tk=tk, lam_init=lam_init),
        out_shape=jax.ShapeDtypeStruct((B, S, D_MODEL), BF16),
        grid=(B, HEADS, S // tq),
        in_specs=[pl.BlockSpec(memory_space=pltpu.SMEM),
                  pl.BlockSpec((1, HEAD_W, tq), lambda b, h, i: (b, h, i)),
                  pl.BlockSpec((1, S, HEAD_W), lambda b, h, i: (b, 0, h)),
                  pl.BlockSpec((1, nt, HEAD_W, tk), lambda b, h, i: (b, 0, h, 0)),
                  blk, blk, small((2, A_DIM)), small((2, A_DIM)), small((1, HEAD_W))],
        out_specs=blk, name="attn_prompt",
        scratch_shapes=[pltpu.VMEM((HEAD_W + SUM_ROWS, 2 * tq), F32),
                        pltpu.VMEM((tk, 2 * tq), F32), pltpu.VMEM((tk, 2 * tq), F32),
                        pltpu.VMEM((tk, 2 * tq), BF16), pltpu.VMEM((tk, 2 * tq), BF16)],
        compiler_params=pltpu.CompilerParams(dimension_semantics=("arbitrary",) * 3,
                                             vmem_limit_bytes=VMEM_LIMIT),
    )(bound, qt, kb, vt, ga, ra, lam_q, lam_k, g_sub)


def _attn_sample_kernel(q_ref, kc_ref, vc_ref, kn_ref, vn_ref, ga_ref, ra_ref, lamq_ref, lamk_ref,
                        gsub_ref, o_ref, *, lam_init):
    t = q_ref.shape[1]
    n_past = kc_ref.shape[1] // HEADS
    lam = _lam(lamq_ref, lamk_ref, lam_init)
    for h in range(HEADS):
        sl = slice(h * HEAD_W, (h + 1) * HEAD_W)
        head_rows = pl.ds(h, n_past, stride=HEADS)
        qq = _split_maps(q_ref[0, :, sl])
        s_c = _dot_nt(qq, kc_ref[0, head_rows, :].astype(BF16))
        s_n = _dot_nt(qq, kn_ref[0, :, sl])
        m = jnp.maximum(jnp.max(s_c, axis=-1, keepdims=True), jnp.max(s_n, axis=-1, keepdims=True))
        p_c = jnp.exp2(s_c - m)
        p_n = jnp.exp2(s_n - m)
        l = jnp.sum(p_c, axis=-1, keepdims=True) + jnp.sum(p_n, axis=-1, keepdims=True)
        acc = (_dot(p_c.astype(BF16), vc_ref[0, head_rows, :].astype(BF16))
               + _dot(p_n.astype(BF16), vn_ref[0, :, sl]))
        o = acc[:t] / l[:t] - lam * (acc[t:] / l[t:])
        o_ref[0, :, sl] = _merge_out(o, gsub_ref, ga_ref[0, :, sl], ra_ref[0, :, sl], lam_init)


def _attn_sample(qa, kc, vc, kn, vn, ga, ra, lam_q, lam_k, g_sub, *, lam_init):
    DB, T, _ = qa.shape
    blk = pl.BlockSpec((1, T, D_MODEL), lambda b: (b, 0, 0))
    cache = pl.BlockSpec((1,) + kc.shape[1:], lambda b: (b, 0, 0))
    small = lambda shape: pl.BlockSpec(shape, lambda b: (0, 0))
    return pl.pallas_call(
        functools.partial(_attn_sample_kernel, lam_init=lam_init),
        out_shape=jax.ShapeDtypeStruct((DB, T, D_MODEL), BF16),
        grid=(DB,),
        in_specs=[blk, cache, cache, blk, blk, blk, blk, small((2, A_DIM)), small((2, A_DIM)),
                  small((1, HEAD_W))],
        out_specs=blk, name="attn_sample",
        compiler_params=pltpu.CompilerParams(dimension_semantics=("arbitrary",),
                                             vmem_limit_bytes=VMEM_LIMIT),
    )(qa, kc, vc, kn, vn, ga, ra, lam_q, lam_k, g_sub)


def _tail_kernel(x_ref, mix_ref, p_ref, wo_ref, gffn_ref, wg_ref, wu_ref, wd_ref, gple_ref, wple_ref,
                 wpg_ref, y_ref, act_ref):
    x1 = x_ref[...] + _dot(mix_ref[...], wo_ref[...])
    h = (_rms_scale(x1) * gffn_ref[...]).astype(BF16)
    for c in range(D_FF // MXU_N):
        sl = slice(c * MXU_N, (c + 1) * MXU_N)
        g = _dot(h, wg_ref[:, sl])
        u = _dot(h, wu_ref[:, sl])
        act_ref[:, sl] = ((g * jax.nn.sigmoid(g)) * u).astype(BF16)
    x2 = x1 + _dot(act_ref[...], wd_ref[...])
    h3 = (_rms_scale(x2) * gple_ref[...]).astype(BF16)
    gate = jax.nn.sigmoid(_dot(h3, wpg_ref[...]))
    y_ref[...] = x2 + gate * _dot(p_ref[...].astype(BF16), wple_ref[...])


def _tail(x2d, mix2d, p2d, w_o, g_ffn, w_g, w_u, w_d, g_ple, w_ple, w_pg, *, tm, name):
    n = x2d.shape[0]
    row = lambda i: (i, 0)
    const = lambda i: (0, 0)
    resident = functools.partial(pl.BlockSpec, pipeline_mode=pl.Buffered(1))
    return pl.pallas_call(
        _tail_kernel, out_shape=jax.ShapeDtypeStruct((n, D_MODEL), F32), grid=(n // tm,),
        in_specs=[pl.BlockSpec((tm, D_MODEL), row), pl.BlockSpec((tm, D_MODEL), row),
                  pl.BlockSpec((tm, PLE_DIM), row),
                  resident((D_MODEL, D_MODEL), const), resident((1, D_MODEL), const),
                  resident((D_MODEL, D_FF), const), resident((D_MODEL, D_FF), const),
                  resident((D_FF, D_MODEL), const), resident((1, D_MODEL), const),
                  resident((PLE_DIM, D_MODEL), const), resident((D_MODEL, D_MODEL), const)],
        out_specs=pl.BlockSpec((tm, D_MODEL), row),
        scratch_shapes=[pltpu.VMEM((tm, D_FF), BF16)], name=name,
        compiler_params=pltpu.CompilerParams(dimension_semantics=("arbitrary",),
                                             vmem_limit_bytes=VMEM_LIMIT),
    )(x2d, mix2d, p2d, w_o, g_ffn, w_g, w_u, w_d, g_ple, w_ple, w_pg)


def kernel(x_prompt, x_sample, cache_attn_k, cache_attn_v, state_ret, p_prompt, p_sample, w_in, g_mix_norm, g_q_norm, g_k_norm, lam_q, lam_k, g_sub_norm, g_ret_norm, w_o, g_ffn_norm, w_ff_gate, w_ff_up, w_ff_down, g_ple_norm, w_ple, w_ple_gate):
    B, S, D = x_prompt.shape
    DB, T, _ = x_sample.shape
    P = cache_attn_k.shape[2]
    depth = w_in.shape[0]
    assert depth == 1 and D == D_MODEL and S % 512 == 0 and 256 % T == 0
    l = 0
    lam_init = 0.8 - 0.6 * math.exp(-0.3 * l)
    tm = 256

    w_in_b = w_in[l].astype(BF16)
    tail_w = (w_o[l].astype(BF16), g_ffn_norm[l][None, :], w_ff_gate[l].astype(BF16),
              w_ff_up[l].astype(BF16), w_ff_down[l].astype(BF16), g_ple_norm[l][None, :],
              w_ple[l].astype(BF16), w_ple_gate[l].astype(BF16))
    g_mix = g_mix_norm[l][None, :]
    gq = jnp.tile(g_q_norm[l], MXU_N // A_DIM)[None, :]
    gk = jnp.tile(g_k_norm[l], MXU_N // A_DIM)[None, :]
    g_sub = g_sub_norm[l][None, :]
    g_rn = g_ret_norm[l]
    grp = jnp.arange(MXU_N) // A_DIM
    gmat = (grp[:, None] == grp[None, :]).astype(BF16)

    cos_p, sin_p = _rope_tables(jnp.arange(S))
    xp2 = x_prompt.reshape(B * S, D)
    qt, kf, kb, vf, vt, ra, ga, st_p = _inproj(
        xp2, w_in_b, g_mix, gq, gk, cos_p, sin_p, gmat, _ret_tables(CHUNK), g_rn,
        tm=512, seq=S, state0=None)
    r3 = lambda a: a.reshape(B, S, D)
    score_bound = ((A_DIM * Q_SCALE * 1.01) * jnp.max(jnp.abs(g_q_norm[l]))
                   * jnp.max(jnp.abs(g_k_norm[l]))).reshape(1, 1)
    mix_p = _attn_prompt(score_bound, qt, r3(kb), vt, r3(ga), r3(ra), lam_q[l], lam_k[l], g_sub,
                         tq=512, lam_init=lam_init)
    y_p = _tail(xp2, mix_p.reshape(B * S, D), p_prompt[l].reshape(B * S, PLE_DIM), *tail_w,
                tm=tm, name="tail_prompt")

    cos_s, sin_s = _rope_tables(P + jnp.arange(T))
    reps = tm // T
    xs2 = x_sample.reshape(DB * T, D)
    st0 = state_ret[l].reshape(DB, HEADS * R_DK, HEAD_W)
    qa_s, kf_s, kb_s, vf_s, vb_s, ra_s, ga_s, st_s = _inproj(
        xs2, w_in_b, g_mix, gq, gk, jnp.tile(cos_s, (reps, 1)), jnp.tile(sin_s, (reps, 1)), gmat,
        _ret_tables(T), g_rn, tm=tm, seq=T, state0=st0)
    s3 = lambda a: a.reshape(DB, T, D)
    mix_s = _attn_sample(s3(qa_s), cache_attn_k[l].reshape(DB, P * HEADS, HEAD_W),
                         cache_attn_v[l].reshape(DB, P * HEADS, HEAD_W),
                         s3(kb_s), s3(vb_s), s3(ga_s), s3(ra_s), lam_q[l], lam_k[l], g_sub,
                         lam_init=lam_init)
    y_s = _tail(xs2, mix_s.reshape(DB * T, D), p_sample[l].reshape(DB * T, PLE_DIM), *tail_w,
                tm=tm, name="tail_sample")

    return (y_p.reshape(B, S, D), y_s.reshape(DB, T, D),
            kf.reshape(1, B, S, HEADS, HEAD_W), vf.reshape(1, B, S, HEADS, HEAD_W),
            st_p.reshape(1, B, HEADS, R_DK, HEAD_W),
            kf_s.reshape(1, DB, T, HEADS, HEAD_W), vf_s.reshape(1, DB, T, HEADS, HEAD_W),
            st_s.reshape(1, DB, HEADS, R_DK, HEAD_W))
```

```python
import functools
import math

import jax
import jax.numpy as jnp
from jax import lax
from jax.experimental import pallas as pl
from jax.experimental.pallas import tpu as pltpu

F32 = jnp.float32
BF16 = jnp.bfloat16

D_MODEL = 1024
CHUNK = 64
PLE_DIM = 256
EPS = 1e-6
HEADS = 8
HEAD_W = 128
A_DIM = 64
R_DK = 64
ROPE_BASE = 10000.0
D_FF = 2816
N_PAIR = HEADS // 2

OFF_QA, OFF_KA, OFF_VA = 0, 1024, 2048
OFF_QR, OFF_KR, OFF_VR = 3072, 3584, 4096
OFF_GRET, OFF_GA, OFF_GR = 5120, 6144, 7168
IN_WIDTH = 8192

MXU_N = 256
PROJ_N = 2 * MXU_N
VMEM_LIMIT = 56 * 1024 * 1024
NEG_BIG = -1e30
SUM_ROWS = 16
ATT_TK = 256
EXP2_SAFE_RANGE = 60.0
Q_SCALE = (A_DIM ** -0.5) * math.log2(math.e)

_NT = (((1,), (1,)), ((), ()))


def _dot(a, b):
    return jnp.dot(a, b, preferred_element_type=F32)


def _dot_nt(a, b):
    return lax.dot_general(a, b, _NT, preferred_element_type=F32)


def _rms_scale(x):
    return x * lax.rsqrt(jnp.mean(x * x, axis=-1, keepdims=True) + EPS)


def _inproj_body(x_ref, w_ref, gmix_ref, gq_ref, gk_ref, cos_ref, sin_ref, gmat_ref,
                 dpair_ref, qdec_ref, kdec_ref, cdec_ref, grn_ref,
                 kf_ref, kb_ref, vf_ref, ra_ref, ga_ref,
                 qs_ref, ks_ref, vs_ref, gs_ref, *, L, get_state, set_state, store_q, store_v,
                 interleave):
    tm = x_ref.shape[0]
    x = x_ref[...]
    h = (_rms_scale(x) * gmix_ref[...]).astype(BF16)
    gmat = gmat_ref[...]

    def proj(off, c):
        return _dot(h, w_ref[:, off + c * PROJ_N: off + (c + 1) * PROJ_N])

    def group_norm(z, g_ref):
        zz = (z * z).astype(BF16)
        ss = jnp.concatenate([_dot(zz[:, j * MXU_N:(j + 1) * MXU_N], gmat)
                              for j in range(PROJ_N // MXU_N)], axis=1)
        return (z * lax.rsqrt(ss * (1.0 / A_DIM) + EPS)) * g_ref[...]

    for c in range(D_MODEL // PROJ_N):
        sl = slice(c * PROJ_N, (c + 1) * PROJ_N)
        vs_ref[:, sl] = proj(OFF_VR, c).astype(BF16)
        g_ret = proj(OFF_GRET, c)
        gate_r = proj(OFF_GR, c)
        gs_ref[:, sl] = (g_ret * jax.nn.sigmoid(g_ret)) * jax.nn.sigmoid(gate_r)

    lane = lax.broadcasted_iota(jnp.int32, (tm, HEAD_W), 1)
    first_half = (lane % R_DK) < (R_DK // 2)
    cos = cos_ref[...]
    sin = sin_ref[...]

    def rotary(z):
        partner = jnp.where(first_half, pltpu.roll(z, HEAD_W - R_DK // 2, axis=1),
                            pltpu.roll(z, R_DK // 2, axis=1))
        return z * cos + partner * sin

    assert HEADS * R_DK == PROJ_N
    zq = proj(OFF_QR, 0)
    zk = proj(OFF_KR, 0)
    for g in range(N_PAIR):
        hs = slice(g * HEAD_W, (g + 1) * HEAD_W)
        qs_ref[:, hs] = rotary(zq[:, hs]).astype(BF16)
        ks_ref[:, hs] = rotary(zk[:, hs]) * (R_DK ** -0.5)

    lane_l = lax.broadcasted_iota(jnp.int32, (L, HEAD_W), 1)
    even = lane_l < R_DK

    def chunk_body(ci, carry):
        r0 = ci * L if isinstance(ci, int) else pl.multiple_of(ci * L, L)
        rows = pl.ds(r0, L)
        for g in range(N_PAIR):
            q2 = qs_ref[rows, g * HEAD_W:(g + 1) * HEAD_W]
            k2 = ks_ref[rows, g * HEAD_W:(g + 1) * HEAD_W]
            v2 = vs_ref[rows, 2 * g * HEAD_W:(2 * g + 2) * HEAD_W]
            zero = jnp.zeros_like(q2)
            qq = jnp.concatenate([jnp.where(even, q2, zero), jnp.where(even, zero, q2)], axis=0)
            a = (_dot_nt(qq, k2.astype(BF16)) * dpair_ref[g]).astype(BF16)
            intra = _dot(a, v2)
            state = get_state(ci, g)
            cross = _dot(qq, state.astype(BF16)) * qdec_ref[g]
            kd = (k2 * kdec_ref[g]).T.astype(BF16)
            kv = _dot(kd, v2)
            kv_pair = jnp.concatenate([kv[:R_DK, :HEAD_W], kv[R_DK:, HEAD_W:]], axis=0)
            set_state(ci, g, cdec_ref[g] * state + kv_pair)
            outs = (intra[:L, :HEAD_W] + cross[:L], intra[L:, HEAD_W:] + cross[L:])
            for par, o in enumerate(outs):
                hh = 2 * g + par
                cols = slice(hh * HEAD_W, (hh + 1) * HEAD_W)
                on = _rms_scale(o) * grn_ref[hh:hh + 1, :]
                ra_ref[rows, cols] = (on * gs_ref[rows, cols]).astype(BF16)
        return carry

    def q_unit(sl, c):
        store_q(sl, group_norm(proj(OFF_QA, c), gq_ref) * Q_SCALE)

    def k_unit(sl, c):
        kn = group_norm(proj(OFF_KA, c), gk_ref)
        kf_ref[:, sl] = kn
        kb_ref[:, sl] = kn.astype(BF16)

    def v_unit(sl, c):
        v = proj(OFF_VA, c)
        vf_ref[:, sl] = v
        store_v(sl, v)

    def gate_unit(sl, c):
        ga_ref[:, sl] = jax.nn.sigmoid(proj(OFF_GA, c)).astype(BF16)

    units = [functools.partial(u, slice(c * PROJ_N, (c + 1) * PROJ_N), c)
             for c in range(D_MODEL // PROJ_N) for u in (q_unit, k_unit, v_unit, gate_unit)]
    n_chunks = tm // L
    if interleave:
        per_chunk = -(-len(units) // n_chunks)
        for ci in range(n_chunks):
            chunk_body(ci, 0)
            for u in units[ci * per_chunk:(ci + 1) * per_chunk]:
                u()
    else:
        lax.fori_loop(0, n_chunks, chunk_body, 0)
        for u in units:
            u()


def _inproj_prompt_kernel(*refs):
    ins, outs = refs[:13], refs[13:]
    (qt_ref, kf_ref, kb_ref, vf_ref, vt_ref, ra_ref, ga_ref, st_ref,
     qs_ref, ks_ref, vs_ref, gs_ref) = outs

    @pl.when(pl.program_id(1) == 0)
    def _():
        st_ref[...] = jnp.zeros_like(st_ref)

    def get_state(ci, g):
        return st_ref[0, g * HEAD_W:(g + 1) * HEAD_W, :]

    def set_state(ci, g, val):
        st_ref[0, g * HEAD_W:(g + 1) * HEAD_W, :] = val

    def store_q(sl, val):
        qt_ref[0, 0, sl, :] = val.T.astype(BF16)

    def store_v(sl, val):
        for kb_i in range(val.shape[0] // ATT_TK):
            vt_ref[0, kb_i, sl, :] = val[kb_i * ATT_TK:(kb_i + 1) * ATT_TK].T.astype(BF16)

    _inproj_body(*ins, kf_ref, kb_ref, vf_ref, ra_ref, ga_ref, qs_ref, ks_ref, vs_ref, gs_ref,
                 L=CHUNK, get_state=get_state, set_state=set_state, store_q=store_q, store_v=store_v,
                 interleave=True)


def _inproj_sample_kernel(*refs, L):
    ins, s0_ref, outs = refs[:13], refs[13], refs[14:]
    (qa_ref, kf_ref, kb_ref, vf_ref, vb_ref, ra_ref, ga_ref, st_ref,
     qs_ref, ks_ref, vs_ref, gs_ref) = outs

    def get_state(ci, g):
        return s0_ref[ci, g * HEAD_W:(g + 1) * HEAD_W, :]

    def set_state(ci, g, val):
        st_ref[ci, g * HEAD_W:(g + 1) * HEAD_W, :] = val

    def store_q(sl, val):
        qa_ref[:, sl] = val.astype(BF16)

    def store_v(sl, val):
        vb_ref[:, sl] = val.astype(BF16)

    _inproj_body(*ins, kf_ref, kb_ref, vf_ref, ra_ref, ga_ref, qs_ref, ks_ref, vs_ref, gs_ref,
                 L=L, get_state=get_state, set_state=set_state, store_q=store_q, store_v=store_v,
                 interleave=False)


def _ret_tables(T):
    log_g = jnp.log1p(-jnp.exp2(-5.0 - jnp.arange(HEADS, dtype=F32)))
    i = jnp.arange(T, dtype=F32)
    diff = i[:, None] - i[None, :]
    d_mat = jnp.where(diff >= 0, jnp.exp(log_g[:, None, None] * jnp.maximum(diff, 0.0)), 0.0)
    q_decay = jnp.exp(log_g[None, :] * (i[:, None] + 1.0))
    k_decay = jnp.exp(log_g[None, :] * (T - 1.0 - i[:, None]))
    chunk_decay = jnp.exp(log_g * T)
    dpair = d_mat.reshape(N_PAIR, 2 * T, T)
    qdec = jnp.broadcast_to(q_decay.T.reshape(N_PAIR, 2 * T, 1), (N_PAIR, 2 * T, HEAD_W))
    kdec = jnp.repeat(k_decay.reshape(T, N_PAIR, 2), R_DK, axis=2).transpose(1, 0, 2)
    cdec = jnp.broadcast_to(jnp.repeat(chunk_decay.reshape(N_PAIR, 2), R_DK, axis=1)[:, :, None],
                            (N_PAIR, 2 * R_DK, HEAD_W))
    return dpair, qdec, kdec, cdec


def _rope_tables(pos):
    half = R_DK // 2
    inv_freq = ROPE_BASE ** (-jnp.arange(half, dtype=F32) / half)
    ang = pos.astype(F32)[:, None] * inv_freq[None, :]
    cos, sin = jnp.cos(ang), jnp.sin(ang)
    return jnp.tile(cos, (1, 4)), jnp.tile(jnp.concatenate([-sin, sin], axis=1), (1, 2))


def _inproj(x2d, w_in, g_mix, gq, gk, cos_t, sin_t, gmat, tables, g_rn, *, tm, seq, state0):
    n = x2d.shape[0]
    prompt = state0 is None
    dpair, qdec, kdec, cdec = tables
    nt = seq // tm if prompt else 1
    nb = n // seq if prompt else n // tm
    per_tile = tm // seq if not prompt else 1

    const2 = lambda *_: (0, 0)
    const3 = lambda *_: (0, 0, 0)
    row = lambda b, i: (b * nt + i, 0)
    resident = functools.partial(pl.BlockSpec, pipeline_mode=pl.Buffered(1))
    in_specs = [
        pl.BlockSpec((tm, D_MODEL), row),
        resident((D_MODEL, IN_WIDTH), const2),
        resident((1, D_MODEL), const2),
        resident((1, PROJ_N), const2),
        resident((1, PROJ_N), const2),
        pl.BlockSpec((tm, HEAD_W), (lambda b, i: (i, 0)) if prompt else const2),
        pl.BlockSpec((tm, HEAD_W), (lambda b, i: (i, 0)) if prompt else const2),
        resident((MXU_N, MXU_N), const2),
        resident(dpair.shape, const3),
        resident(qdec.shape, const3),
        resident(kdec.shape, const3),
        resident(cdec.shape, const3),
        resident((HEADS, HEAD_W), const2),
    ]
    args = [x2d, w_in, g_mix, gq, gk, cos_t, sin_t, gmat, dpair, qdec, kdec, cdec, g_rn]
    st_rows = 2 * R_DK * N_PAIR
    if prompt:
        st_spec = pl.BlockSpec((1, st_rows, HEAD_W), lambda b, i: (b, 0, 0))
        st_shape = jax.ShapeDtypeStruct((nb, st_rows, HEAD_W), F32)
        kernel = _inproj_prompt_kernel
        L = CHUNK
    else:
        st_spec = pl.BlockSpec((per_tile, st_rows, HEAD_W), lambda b, i: (b, 0, 0))
        st_shape = jax.ShapeDtypeStruct(state0.shape, F32)
        in_specs.append(st_spec)
        args.append(state0)
        kernel = functools.partial(_inproj_sample_kernel, L=seq)
        L = seq
    tok = lambda dt: jax.ShapeDtypeStruct((n, D_MODEL), dt)
    tok_spec = pl.BlockSpec((tm, D_MODEL), row)
    out_shape = [tok(BF16), tok(F32), tok(BF16), tok(F32), tok(BF16), tok(BF16), tok(BF16), st_shape]
    out_specs = [tok_spec] * 7 + [st_spec]
    if prompt:
        out_shape[0] = jax.ShapeDtypeStruct((nb, nt, D_MODEL, tm), BF16)
        out_specs[0] = pl.BlockSpec((1, 1, D_MODEL, tm), lambda b, i: (b, i, 0, 0))
        out_shape[4] = jax.ShapeDtypeStruct((nb, seq // ATT_TK, D_MODEL, ATT_TK), BF16)
        out_specs[4] = pl.BlockSpec((1, tm // ATT_TK, D_MODEL, ATT_TK), lambda b, i: (b, i, 0, 0))
    scratch = [pltpu.VMEM((tm, HEADS * R_DK), BF16), pltpu.VMEM((tm, HEADS * R_DK), F32),
               pltpu.VMEM((tm, D_MODEL), BF16), pltpu.VMEM((tm, D_MODEL), F32)]
    return pl.pallas_call(
        kernel, out_shape=out_shape, grid=(nb, nt), in_specs=in_specs, out_specs=out_specs,
        scratch_shapes=scratch, name="inproj_prompt" if prompt else "inproj_sample",
        compiler_params=pltpu.CompilerParams(dimension_semantics=("arbitrary", "arbitrary"),
                                             vmem_limit_bytes=VMEM_LIMIT),
    )(*args)


def _lam(lamq_ref, lamk_ref, lam_init):
    e = jnp.exp(jnp.sum(lamq_ref[...] * lamk_ref[...], axis=-1, keepdims=True))
    return e[0:1, :] - e[1:2, :] + lam_init


def _split_maps(q):
    lane = lax.broadcasted_iota(jnp.int32, q.shape, 1)
    zero = jnp.zeros_like(q)
    return jnp.concatenate([jnp.where(lane < A_DIM, q, zero), jnp.where(lane < A_DIM, zero, q)], axis=0)


def _merge_out(o, gsub_ref, ga, ra, lam_init):
    on = (_rms_scale(o) * gsub_ref[...]) * (1.0 - lam_init)
    return (ga.astype(F32) * on + ra.astype(F32)).astype(BF16)


def _attn_prompt_body(*refs, tq, **static):
    def q_block(i, carry):
        _attn_prompt_q_block(i, *refs, tq=tq, **static)
        return carry

    n_q = refs[1].shape[1] // tq
    lax.fori_loop(0, n_q, q_block, 0)


def _attn_prompt_q_block(i, qt_ref, k_ref, vt_ref, ga_ref, ra_ref, lamq_ref, lamk_ref, gsub_ref, o_ref,
                         acc_ref, s0_ref, s1_ref, p0_ref, p1_ref, *, tq, tk, lam_init, bounded):
    assert tq == 2 * tk
    qt = qt_ref[0, i]
    feat = lax.broadcasted_iota(jnp.int32, qt.shape, 0)
    zero = jnp.zeros_like(qt)
    qqt = jnp.concatenate([jnp.where(feat < A_DIM, qt, zero), jnp.where(feat < A_DIM, zero, qt)],
                          axis=1)
    s_refs, p_refs = (s0_ref, s1_ref), (p0_ref, p1_ref)

    def scores(t, slot):
        rows = pl.ds(pl.multiple_of(t * tk, tk), tk)
        s_refs[slot][...] = _dot(k_ref[0, rows, :], qqt)

    def softmax(slot, carry, mask):
        st = s_refs[slot][...]
        if mask is not None:
            st = jnp.where(mask, st, NEG_BIG)
        if bounded:
            p_refs[slot][...] = jnp.exp2(st).astype(BF16)
            return carry
        m_new = jnp.maximum(carry[0], jnp.max(st, axis=0, keepdims=True))
        p_refs[slot][...] = jnp.exp2(st - m_new).astype(BF16)
        return m_new, jnp.exp2(carry[0] - m_new)

    ones_rows = jnp.ones((SUM_ROWS, tk), BF16)

    def fold_values(t, slot, alpha):
        vt1 = jnp.concatenate([vt_ref[0, t], ones_rows], axis=0)
        pv = _dot(vt1, p_refs[slot][...])
        acc_ref[...] = acc_ref[...] + pv if bounded else alpha * acc_ref[...] + pv

    def step(t, slot, carry, mask=None, last=False):
        fold_values(jnp.maximum(t - 1, 0), 1 - slot, carry[1])
        if not last:
            scores(t + 1, 1 - slot)
        return softmax(slot, carry, mask)

    acc_ref[...] = jnp.zeros_like(acc_ref)
    p1_ref[...] = jnp.zeros_like(p1_ref)
    scores(0, 0)
    carry = (jnp.full((1, 2 * tq), NEG_BIG, F32), jnp.ones((1, 2 * tq), F32))

    def pair(jj, c):
        return step(2 * jj + 1, 1, step(2 * jj, 0, c))

    def quad(jj, c):
        return pair(2 * jj + 1, pair(2 * jj, c))

    carry = lax.fori_loop(0, lax.shift_right_logical(i, 1), quad, carry)
    carry = lax.cond((i & 1) == 1, lambda c: pair(i - 1, c), lambda c: c, carry)
    key = lax.broadcasted_iota(jnp.int32, (tk, 2 * tq), 0)
    qry = lax.broadcasted_iota(jnp.int32, (tk, 2 * tq), 1) % tq
    carry = step(2 * i, 0, carry, mask=(key // CHUNK) <= (qry // CHUNK))
    carry = step(2 * i + 1, 1, carry, mask=((tk + key) // CHUNK) <= (qry // CHUNK), last=True)
    fold_values(2 * i + 1, 1, carry[1])
    lam = _lam(lamq_ref, lamk_ref, lam_init)
    acc = acc_ref[:HEAD_W, :]
    l = acc_ref[HEAD_W:HEAD_W + 1, :]
    ot = acc[:, :tq] / l[:, :tq] - lam * (acc[:, tq:] / l[:, tq:])
    q_rows = pl.ds(pl.multiple_of(i * tq, tq), tq)
    o_ref[0, q_rows, :] = _merge_out(ot.T, gsub_ref, ga_ref[0, q_rows, :], ra_ref[0, q_rows, :], lam_init)


def _attn_prompt_kernel(bound_ref, *refs, **static):
    bounded = bound_ref[0, 0] <= EXP2_SAFE_RANGE
    pl.when(bounded)(lambda: _attn_prompt_body(*refs, bounded=True, **static))
    pl.when(jnp.logical_not(bounded))(lambda: _attn_prompt_body(*refs, bounded=False, **static))


def _attn_prompt(bound, qt, kb, vt, ga, ra, lam_q, lam_k, g_sub, *, tq, lam_init):
    B, S, _ = kb.shape
    nt, tk = vt.shape[1], vt.shape[3]
    assert qt.shape[1:] == (S // tq, D_MODEL, tq)
    blk = pl.BlockSpec((1, S, HEAD_W), lambda b, h: (b, 0, h))
    small = lambda shape: pl.BlockSpec(shape, lambda b, h: (0, 0))
    return pl.pallas_call(
        functools.partial(_attn_prompt_kernel, tq=tq, tk=tk, lam_init=lam_init),
        out_shape=jax.ShapeDtypeStruct((B, S, D_MODEL), BF16),
        grid=(B, HEADS),
        in_specs=[pl.BlockSpec(memory_space=pltpu.SMEM),
                  pl.BlockSpec((1, S // tq, HEAD_W, tq), lambda b, h: (b, 0, h, 0)),
                  blk,
                  pl.BlockSpec((1, nt, HEAD_W, tk), lambda b, h: (b, 0, h, 0)),
                  blk, blk, small((2, A_DIM)), small((2, A_DIM)), small((1, HEAD_W))],
        out_specs=blk, name="attn_prompt",
        scratch_shapes=[pltpu.VMEM((HEAD_W + SUM_ROWS, 2 * tq), F32),
                        pltpu.VMEM((tk, 2 * tq), F32), pltpu.VMEM((tk, 2 * tq), F32),
                        pltpu.VMEM((tk, 2 * tq), BF16), pltpu.VMEM((tk, 2 * tq), BF16)],
        compiler_params=pltpu.CompilerParams(dimension_semantics=("arbitrary",) * 2,
                                             vmem_limit_bytes=VMEM_LIMIT),
    )(bound, qt, kb, vt, ga, ra, lam_q, lam_k, g_sub)


def _attn_sample_kernel(q_ref, kc_ref, vc_ref, kn_ref, vn_ref, ga_ref, ra_ref, lamq_ref, lamk_ref,
                        gsub_ref, o_ref, *, lam_init):
    t = q_ref.shape[1]
    n_past = kc_ref.shape[1] // HEADS
    lam = _lam(lamq_ref, lamk_ref, lam_init)
    for h in range(HEADS):
        sl = slice(h * HEAD_W, (h + 1) * HEAD_W)
        head_rows = pl.ds(h, n_past, stride=HEADS)
        qq = _split_maps(q_ref[0, :, sl])
        s_c = _dot_nt(qq, kc_ref[0, head_rows, :].astype(BF16))
        s_n = _dot_nt(qq, kn_ref[0, :, sl])
        m = jnp.maximum(jnp.max(s_c, axis=-1, keepdims=True), jnp.max(s_n, axis=-1, keepdims=True))
        p_c = jnp.exp2(s_c - m)
        p_n = jnp.exp2(s_n - m)
        l = jnp.sum(p_c, axis=-1, keepdims=True) + jnp.sum(p_n, axis=-1, keepdims=True)
        acc = (_dot(p_c.astype(BF16), vc_ref[0, head_rows, :].astype(BF16))
               + _dot(p_n.astype(BF16), vn_ref[0, :, sl]))
        o = acc[:t] / l[:t] - lam * (acc[t:] / l[t:])
        o_ref[0, :, sl] = _merge_out(o, gsub_ref, ga_ref[0, :, sl], ra_ref[0, :, sl], lam_init)


def _attn_sample(qa, kc, vc, kn, vn, ga, ra, lam_q, lam_k, g_sub, *, lam_init):
    DB, T, _ = qa.shape
    blk = pl.BlockSpec((1, T, D_MODEL), lambda b: (b, 0, 0))
    cache = pl.BlockSpec((1,) + kc.shape[1:], lambda b: (b, 0, 0))
    small = lambda shape: pl.BlockSpec(shape, lambda b: (0, 0))
    return pl.pallas_call(
        functools.partial(_attn_sample_kernel, lam_init=lam_init),
        out_shape=jax.ShapeDtypeStruct((DB, T, D_MODEL), BF16),
        grid=(DB,),
        in_specs=[blk, cache, cache, blk, blk, blk, blk, small((2, A_DIM)), small((2, A_DIM)),
                  small((1, HEAD_W))],
        out_specs=blk, name="attn_sample",
        compiler_params=pltpu.CompilerParams(dimension_semantics=("arbitrary",),
                                             vmem_limit_bytes=VMEM_LIMIT),
    )(qa, kc, vc, kn, vn, ga, ra, lam_q, lam_k, g_sub)


def _tail_kernel(x_ref, mix_ref, p_ref, wo_ref, gffn_ref, wg_ref, wu_ref, wd_ref, gple_ref, wple_ref,
                 wpg_ref, y_ref, act_ref):
    x1 = x_ref[...] + _dot(mix_ref[...], wo_ref[...])
    h = (_rms_scale(x1) * gffn_ref[...]).astype(BF16)
    for c in range(D_FF // MXU_N):
        sl = slice(c * MXU_N, (c + 1) * MXU_N)
        g = _dot(h, wg_ref[:, sl])
        u = _dot(h, wu_ref[:, sl])
        act_ref[:, sl] = ((g * jax.nn.sigmoid(g)) * u).astype(BF16)
    x2 = x1 + _dot(act_ref[...], wd_ref[...])
    h3 = (_rms_scale(x2) * gple_ref[...]).astype(BF16)
    gate = jax.nn.sigmoid(_dot(h3, wpg_ref[...]))
    y_ref[...] = x2 + gate * _dot(p_ref[...].astype(BF16), wple_ref[...])


def _tail(x2d, mix2d, p2d, w_o, g_ffn, w_g, w_u, w_d, g_ple, w_ple, w_pg, *, tm, name):
    n = x2d.shape[0]
    row = lambda i: (i, 0)
    const = lambda i: (0, 0)
    resident = functools.partial(pl.BlockSpec, pipeline_mode=pl.Buffered(1))
    return pl.pallas_call(
        _tail_kernel, out_shape=jax.ShapeDtypeStruct((n, D_MODEL), F32), grid=(n // tm,),
        in_specs=[pl.BlockSpec((tm, D_MODEL), row), pl.BlockSpec((tm, D_MODEL), row),
                  pl.BlockSpec((tm, PLE_DIM), row),
                  resident((D_MODEL, D_MODEL), const), resident((1, D_MODEL), const),
                  resident((D_MODEL, D_FF), const), resident((D_MODEL, D_FF), const),
                  resident((D_FF, D_MODEL), const), resident((1, D_MODEL), const),
                  resident((PLE_DIM, D_MODEL), const), resident((D_MODEL, D_MODEL), const)],
        out_specs=pl.BlockSpec((tm, D_MODEL), row),
        scratch_shapes=[pltpu.VMEM((tm, D_FF), BF16)], name=name,
        compiler_params=pltpu.CompilerParams(dimension_semantics=("arbitrary",),
                                             vmem_limit_bytes=VMEM_LIMIT),
    )(x2d, mix2d, p2d, w_o, g_ffn, w_g, w_u, w_d, g_ple, w_ple, w_pg)


def kernel(x_prompt, x_sample, cache_attn_k, cache_attn_v, state_ret, p_prompt, p_sample, w_in, g_mix_norm, g_q_norm, g_k_norm, lam_q, lam_k, g_sub_norm, g_ret_norm, w_o, g_ffn_norm, w_ff_gate, w_ff_up, w_ff_down, g_ple_norm, w_ple, w_ple_gate):
    B, S, D = x_prompt.shape
    DB, T, _ = x_sample.shape
    P = cache_attn_k.shape[2]
    depth = w_in.shape[0]
    assert depth == 1 and D == D_MODEL and S % 512 == 0 and 256 % T == 0
    l = 0
    lam_init = 0.8 - 0.6 * math.exp(-0.3 * l)
    tm = 256

    w_in_b = w_in[l].astype(BF16)
    tail_w = (w_o[l].astype(BF16), g_ffn_norm[l][None, :], w_ff_gate[l].astype(BF16),
              w_ff_up[l].astype(BF16), w_ff_down[l].astype(BF16), g_ple_norm[l][None, :],
              w_ple[l].astype(BF16), w_ple_gate[l].astype(BF16))
    g_mix = g_mix_norm[l][None, :]
    gq = jnp.tile(g_q_norm[l], PROJ_N // A_DIM)[None, :]
    gk = jnp.tile(g_k_norm[l], PROJ_N // A_DIM)[None, :]
    g_sub = g_sub_norm[l][None, :]
    g_rn = g_ret_norm[l]
    grp = jnp.arange(MXU_N) // A_DIM
    gmat = (grp[:, None] == grp[None, :]).astype(BF16)

    cos_p, sin_p = _rope_tables(jnp.arange(S))
    xp2 = x_prompt.reshape(B * S, D)
    qt, kf, kb, vf, vt, ra, ga, st_p = _inproj(
        xp2, w_in_b, g_mix, gq, gk, cos_p, sin_p, gmat, _ret_tables(CHUNK), g_rn,
        tm=512, seq=S, state0=None)
    r3 = lambda a: a.reshape(B, S, D)
    score_bound = ((A_DIM * Q_SCALE * 1.01) * jnp.max(jnp.abs(g_q_norm[l]))
                   * jnp.max(jnp.abs(g_k_norm[l]))).reshape(1, 1)
    mix_p = _attn_prompt(score_bound, qt, r3(kb), vt, r3(ga), r3(ra), lam_q[l], lam_k[l], g_sub,
                         tq=512, lam_init=lam_init)
    y_p = _tail(xp2, mix_p.reshape(B * S, D), p_prompt[l].reshape(B * S, PLE_DIM), *tail_w,
                tm=tm, name="tail_prompt")

    cos_s, sin_s = _rope_tables(P + jnp.arange(T))
    reps = tm // T
    xs2 = x_sample.reshape(DB * T, D)
    st0 = state_ret[l].reshape(DB, HEADS * R_DK, HEAD_W)
    qa_s, kf_s, kb_s, vf_s, vb_s, ra_s, ga_s, st_s = _inproj(
        xs2, w_in_b, g_mix, gq, gk, jnp.tile(cos_s, (reps, 1)), jnp.tile(sin_s, (reps, 1)), gmat,
        _ret_tables(T), g_rn, tm=tm, seq=T, state0=st0)
    s3 = lambda a: a.reshape(DB, T, D)
    mix_s = _attn_sample(s3(qa_s), cache_attn_k[l].reshape(DB, P * HEADS, HEAD_W),
                         cache_attn_v[l].reshape(DB, P * HEADS, HEAD_W),
                         s3(kb_s), s3(vb_s), s3(ga_s), s3(ra_s), lam_q[l], lam_k[l], g_sub,
                         lam_init=lam_init)
    y_s = _tail(xs2, mix_s.reshape(DB * T, D), p_sample[l].reshape(DB * T, PLE_DIM), *tail_w,
                tm=tm, name="tail_sample")

    return (y_p.reshape(B, S, D), y_s.reshape(DB, T, D),
            kf.reshape(1, B, S, HEADS, HEAD_W), vf.reshape(1, B, S, HEADS, HEAD_W),
            st_p.reshape(1, B, HEADS, R_DK, HEAD_W),
            kf_s.reshape(1, DB, T, HEADS, HEAD_W), vf_s.reshape(1, DB, T, HEADS, HEAD_W),
            st_s.reshape(1, DB, HEADS, R_DK, HEAD_W))
```

```python
import functools
import math

import jax
import jax.numpy as jnp
from jax import lax
from jax.experimental import pallas as pl
from jax.experimental.pallas import tpu as pltpu

F32 = jnp.float32
BF16 = jnp.bfloat16

D_MODEL = 1024
CHUNK = 64
PLE_DIM = 256
EPS = 1e-6
HEADS = 8
HEAD_W = 128
A_DIM = 64
R_DK = 64
ROPE_BASE = 10000.0
D_FF = 2816
N_PAIR = HEADS // 2

OFF_QA, OFF_KA, OFF_VA = 0, 1024, 2048
OFF_QR, OFF_KR, OFF_VR = 3072, 3584, 4096
OFF_GRET, OFF_GA, OFF_GR = 5120, 6144, 7168
IN_WIDTH = 8192

MXU_N = 256
PROJ_N = 2 * MXU_N
VMEM_LIMIT = 56 * 1024 * 1024
NEG_BIG = -1e30
SUM_ROWS = 16
ATT_TK = 256
EXP2_SAFE_RANGE = 60.0
Q_SCALE = (A_DIM ** -0.5) * math.log2(math.e)

_NT = (((1,), (1,)), ((), ()))


def _dot(a, b):
    return jnp.dot(a, b, preferred_element_type=F32)


def _dot_nt(a, b):
    return lax.dot_general(a, b, _NT, preferred_element_type=F32)


def _rms_scale(x):
    return x * lax.rsqrt(jnp.mean(x * x, axis=-1, keepdims=True) + EPS)


def _inproj_body(x_ref, w_ref, gmix_ref, gq_ref, gk_ref, cos_ref, sin_ref, gmat_ref,
                 dpair_ref, qdec_ref, kdec_ref, cdec_ref, grn_ref,
                 kf_ref, kb_ref, vf_ref, ra_ref, ga_ref,
                 qs_ref, ks_ref, vs_ref, gs_ref, *, L, get_state, set_state, store_q, store_v,
                 interleave):
    tm = x_ref.shape[0]
    x = x_ref[...]
    h = (_rms_scale(x) * gmix_ref[...]).astype(BF16)
    gmat = gmat_ref[...]

    def proj(off, c):
        return _dot(h, w_ref[:, off + c * PROJ_N: off + (c + 1) * PROJ_N])

    def group_norm(z, g_ref):
        zz = (z * z).astype(BF16)
        ss = jnp.concatenate([_dot(zz[:, j * MXU_N:(j + 1) * MXU_N], gmat)
                              for j in range(PROJ_N // MXU_N)], axis=1)
        return (z * lax.rsqrt(ss * (1.0 / A_DIM) + EPS)) * g_ref[...]

    for c in range(D_MODEL // PROJ_N):
        sl = slice(c * PROJ_N, (c + 1) * PROJ_N)
        vs_ref[:, sl] = proj(OFF_VR, c).astype(BF16)
        g_ret = proj(OFF_GRET, c)
        gate_r = proj(OFF_GR, c)
        gs_ref[:, sl] = (g_ret * jax.nn.sigmoid(g_ret)) * jax.nn.sigmoid(gate_r)

    lane = lax.broadcasted_iota(jnp.int32, (tm, HEAD_W), 1)
    first_half = (lane % R_DK) < (R_DK // 2)
    cos = cos_ref[...]
    sin = sin_ref[...]

    def rotary(z):
        partner = jnp.where(first_half, pltpu.roll(z, HEAD_W - R_DK // 2, axis=1),
                            pltpu.roll(z, R_DK // 2, axis=1))
        return z * cos + partner * sin

    assert HEADS * R_DK == PROJ_N
    zq = proj(OFF_QR, 0)
    zk = proj(OFF_KR, 0)
    for g in range(N_PAIR):
        hs = slice(g * HEAD_W, (g + 1) * HEAD_W)
        qs_ref[:, hs] = rotary(zq[:, hs]).astype(BF16)
        ks_ref[:, hs] = rotary(zk[:, hs]) * (R_DK ** -0.5)

    lane_l = lax.broadcasted_iota(jnp.int32, (L, HEAD_W), 1)
    even = lane_l < R_DK

    def chunk_body(ci, carry):
        r0 = ci * L if isinstance(ci, int) else pl.multiple_of(ci * L, L)
        rows = pl.ds(r0, L)
        for g in range(N_PAIR):
            q2 = qs_ref[rows, g * HEAD_W:(g + 1) * HEAD_W]
            k2 = ks_ref[rows, g * HEAD_W:(g + 1) * HEAD_W]
            v2 = vs_ref[rows, 2 * g * HEAD_W:(2 * g + 2) * HEAD_W]
            zero = jnp.zeros_like(q2)
            qq = jnp.concatenate([jnp.where(even, q2, zero), jnp.where(even, zero, q2)], axis=0)
            a = (_dot_nt(qq, k2.astype(BF16)) * dpair_ref[g]).astype(BF16)
            intra = _dot(a, v2)
            state = get_state(ci, g)
            cross = _dot(qq, state.astype(BF16)) * qdec_ref[g]
            kd = (k2 * kdec_ref[g]).T.astype(BF16)
            kv = _dot(kd, v2)
            kv_pair = jnp.concatenate([kv[:R_DK, :HEAD_W], kv[R_DK:, HEAD_W:]], axis=0)
            set_state(ci, g, cdec_ref[g] * state + kv_pair)
            outs = (intra[:L, :HEAD_W] + cross[:L], intra[L:, HEAD_W:] + cross[L:])
            for par, o in enumerate(outs):
                hh = 2 * g + par
                cols = slice(hh * HEAD_W, (hh + 1) * HEAD_W)
                on = _rms_scale(o) * grn_ref[hh:hh + 1, :]
                ra_ref[rows, cols] = (on * gs_ref[rows, cols]).astype(BF16)
        return carry

    def q_unit(sl, c):
        store_q(sl, group_norm(proj(OFF_QA, c), gq_ref) * Q_SCALE)

    def store_cache_layout(ref, c, val):
        for j in range(PROJ_N // HEAD_W):
            head = c * (PROJ_N // HEAD_W) + j
            ref[pl.ds(head, tm, stride=HEADS), :] = val[:, j * HEAD_W:(j + 1) * HEAD_W]

    def k_unit(sl, c):
        kn = group_norm(proj(OFF_KA, c), gk_ref)
        store_cache_layout(kf_ref, c, kn)
        kb_ref[:, sl] = kn.astype(BF16)

    def v_unit(sl, c):
        v = proj(OFF_VA, c)
        store_cache_layout(vf_ref, c, v)
        store_v(sl, v)

    def gate_unit(sl, c):
        ga_ref[:, sl] = jax.nn.sigmoid(proj(OFF_GA, c)).astype(BF16)

    units = [functools.partial(u, slice(c * PROJ_N, (c + 1) * PROJ_N), c)
             for c in range(D_MODEL // PROJ_N) for u in (q_unit, k_unit, v_unit, gate_unit)]
    n_chunks = tm // L
    if interleave:
        per_chunk = -(-len(units) // n_chunks)
        for ci in range(n_chunks):
            chunk_body(ci, 0)
            for u in units[ci * per_chunk:(ci + 1) * per_chunk]:
                u()
    else:
        lax.fori_loop(0, n_chunks, chunk_body, 0)
        for u in units:
            u()


def _inproj_prompt_kernel(*refs):
    ins, outs = refs[:13], refs[13:]
    (qt_ref, kf_ref, kb_ref, vf_ref, vt_ref, ra_ref, ga_ref, st_ref,
     qs_ref, ks_ref, vs_ref, gs_ref) = outs

    @pl.when(pl.program_id(1) == 0)
    def _():
        st_ref[...] = jnp.zeros_like(st_ref)

    def get_state(ci, g):
        return st_ref[0, g * HEAD_W:(g + 1) * HEAD_W, :]

    def set_state(ci, g, val):
        st_ref[0, g * HEAD_W:(g + 1) * HEAD_W, :] = val

    def store_q(sl, val):
        qt_ref[0, 0, sl, :] = val.T.astype(BF16)

    def store_v(sl, val):
        for kb_i in range(val.shape[0] // ATT_TK):
            vt_ref[0, kb_i, sl, :] = val[kb_i * ATT_TK:(kb_i + 1) * ATT_TK].T.astype(BF16)

    _inproj_body(*ins, kf_ref, kb_ref, vf_ref, ra_ref, ga_ref, qs_ref, ks_ref, vs_ref, gs_ref,
                 L=CHUNK, get_state=get_state, set_state=set_state, store_q=store_q, store_v=store_v,
                 interleave=True)


def _inproj_sample_kernel(*refs, L):
    ins, s0_ref, outs = refs[:13], refs[13], refs[14:]
    (qa_ref, kf_ref, kb_ref, vf_ref, vb_ref, ra_ref, ga_ref, st_ref,
     qs_ref, ks_ref, vs_ref, gs_ref) = outs

    def get_state(ci, g):
        return s0_ref[ci, g * HEAD_W:(g + 1) * HEAD_W, :]

    def set_state(ci, g, val):
        st_ref[ci, g * HEAD_W:(g + 1) * HEAD_W, :] = val

    def store_q(sl, val):
        qa_ref[:, sl] = val.astype(BF16)

    def store_v(sl, val):
        vb_ref[:, sl] = val.astype(BF16)

    _inproj_body(*ins, kf_ref, kb_ref, vf_ref, ra_ref, ga_ref, qs_ref, ks_ref, vs_ref, gs_ref,
                 L=L, get_state=get_state, set_state=set_state, store_q=store_q, store_v=store_v,
                 interleave=True)


def _ret_tables(T):
    log_g = jnp.log1p(-jnp.exp2(-5.0 - jnp.arange(HEADS, dtype=F32)))
    i = jnp.arange(T, dtype=F32)
    diff = i[:, None] - i[None, :]
    d_mat = jnp.where(diff >= 0, jnp.exp(log_g[:, None, None] * jnp.maximum(diff, 0.0)), 0.0)
    q_decay = jnp.exp(log_g[None, :] * (i[:, None] + 1.0))
    k_decay = jnp.exp(log_g[None, :] * (T - 1.0 - i[:, None]))
    chunk_decay = jnp.exp(log_g * T)
    dpair = d_mat.reshape(N_PAIR, 2 * T, T)
    qdec = jnp.broadcast_to(q_decay.T.reshape(N_PAIR, 2 * T, 1), (N_PAIR, 2 * T, HEAD_W))
    kdec = jnp.repeat(k_decay.reshape(T, N_PAIR, 2), R_DK, axis=2).transpose(1, 0, 2)
    cdec = jnp.broadcast_to(jnp.repeat(chunk_decay.reshape(N_PAIR, 2), R_DK, axis=1)[:, :, None],
                            (N_PAIR, 2 * R_DK, HEAD_W))
    return dpair, qdec, kdec, cdec


def _rope_tables(pos):
    half = R_DK // 2
    inv_freq = ROPE_BASE ** (-jnp.arange(half, dtype=F32) / half)
    ang = pos.astype(F32)[:, None] * inv_freq[None, :]
    cos, sin = jnp.cos(ang), jnp.sin(ang)
    return jnp.tile(cos, (1, 4)), jnp.tile(jnp.concatenate([-sin, sin], axis=1), (1, 2))


def _inproj(x2d, w_in, g_mix, gq, gk, cos_t, sin_t, gmat, tables, g_rn, *, tm, seq, state0):
    n = x2d.shape[0]
    prompt = state0 is None
    dpair, qdec, kdec, cdec = tables
    nt = seq // tm if prompt else 1
    nb = n // seq if prompt else n // tm
    per_tile = tm // seq if not prompt else 1

    const2 = lambda *_: (0, 0)
    const3 = lambda *_: (0, 0, 0)
    row = lambda b, i: (b * nt + i, 0)
    resident = functools.partial(pl.BlockSpec, pipeline_mode=pl.Buffered(1))
    in_specs = [
        pl.BlockSpec((tm, D_MODEL), row),
        resident((D_MODEL, IN_WIDTH), const2),
        resident((1, D_MODEL), const2),
        resident((1, PROJ_N), const2),
        resident((1, PROJ_N), const2),
        pl.BlockSpec((tm, HEAD_W), (lambda b, i: (i, 0)) if prompt else const2),
        pl.BlockSpec((tm, HEAD_W), (lambda b, i: (i, 0)) if prompt else const2),
        resident((MXU_N, MXU_N), const2),
        resident(dpair.shape, const3),
        resident(qdec.shape, const3),
        resident(kdec.shape, const3),
        resident(cdec.shape, const3),
        resident((HEADS, HEAD_W), const2),
    ]
    args = [x2d, w_in, g_mix, gq, gk, cos_t, sin_t, gmat, dpair, qdec, kdec, cdec, g_rn]
    st_rows = 2 * R_DK * N_PAIR
    if prompt:
        st_spec = pl.BlockSpec((1, st_rows, HEAD_W), lambda b, i: (b, 0, 0))
        st_shape = jax.ShapeDtypeStruct((nb, st_rows, HEAD_W), F32)
        kernel = _inproj_prompt_kernel
        L = CHUNK
    else:
        st_spec = pl.BlockSpec((per_tile, st_rows, HEAD_W), lambda b, i: (b, 0, 0))
        st_shape = jax.ShapeDtypeStruct(state0.shape, F32)
        in_specs.append(st_spec)
        args.append(state0)
        kernel = functools.partial(_inproj_sample_kernel, L=seq)
        L = seq
    tok = lambda dt: jax.ShapeDtypeStruct((n, D_MODEL), dt)
    tok_spec = pl.BlockSpec((tm, D_MODEL), row)
    cache = jax.ShapeDtypeStruct((n * HEADS, HEAD_W), F32)
    cache_spec = pl.BlockSpec((tm * HEADS, HEAD_W), row)
    out_shape = [tok(BF16), cache, tok(BF16), cache, tok(BF16), tok(BF16), tok(BF16), st_shape]
    out_specs = [tok_spec, cache_spec, tok_spec, cache_spec, tok_spec, tok_spec, tok_spec, st_spec]
    if prompt:
        out_shape[0] = jax.ShapeDtypeStruct((nb, nt, D_MODEL, tm), BF16)
        out_specs[0] = pl.BlockSpec((1, 1, D_MODEL, tm), lambda b, i: (b, i, 0, 0))
        out_shape[4] = jax.ShapeDtypeStruct((nb, seq // ATT_TK, D_MODEL, ATT_TK), BF16)
        out_specs[4] = pl.BlockSpec((1, tm // ATT_TK, D_MODEL, ATT_TK), lambda b, i: (b, i, 0, 0))
    scratch = [pltpu.VMEM((tm, HEADS * R_DK), BF16), pltpu.VMEM((tm, HEADS * R_DK), F32),
               pltpu.VMEM((tm, D_MODEL), BF16), pltpu.VMEM((tm, D_MODEL), F32)]
    return pl.pallas_call(
        kernel, out_shape=out_shape, grid=(nb, nt), in_specs=in_specs, out_specs=out_specs,
        scratch_shapes=scratch, name="inproj_prompt" if prompt else "inproj_sample",
        compiler_params=pltpu.CompilerParams(dimension_semantics=("arbitrary", "arbitrary"),
                                             vmem_limit_bytes=VMEM_LIMIT),
    )(*args)


def _lam(lamq_ref, lamk_ref, lam_init):
    e = jnp.exp(jnp.sum(lamq_ref[...] * lamk_ref[...], axis=-1, keepdims=True))
    return e[0:1, :] - e[1:2, :] + lam_init


def _split_maps(q):
    lane = lax.broadcasted_iota(jnp.int32, q.shape, 1)
    zero = jnp.zeros_like(q)
    return jnp.concatenate([jnp.where(lane < A_DIM, q, zero), jnp.where(lane < A_DIM, zero, q)], axis=0)


def _merge_out(o, gsub_ref, ga, ra, lam_init):
    on = (_rms_scale(o) * gsub_ref[...]) * (1.0 - lam_init)
    return (ga.astype(F32) * on + ra.astype(F32)).astype(BF16)


def _attn_prompt_body(*refs, tq, **static):
    def q_block(i, carry):
        _attn_prompt_q_block(i, *refs, tq=tq, **static)
        return carry

    n_q = refs[1].shape[1] // tq
    lax.fori_loop(0, n_q, q_block, 0)


def _attn_prompt_q_block(i, qt_ref, k_ref, vt_ref, ga_ref, ra_ref, lamq_ref, lamk_ref, gsub_ref, o_ref,
                         acc_ref, s0_ref, s1_ref, p0_ref, p1_ref, *, tq, tk, lam_init, bounded):
    assert tq == 2 * tk
    qt = qt_ref[0, i]
    feat = lax.broadcasted_iota(jnp.int32, qt.shape, 0)
    zero = jnp.zeros_like(qt)
    qqt = jnp.concatenate([jnp.where(feat < A_DIM, qt, zero), jnp.where(feat < A_DIM, zero, qt)],
                          axis=1)
    s_refs, p_refs = (s0_ref, s1_ref), (p0_ref, p1_ref)

    def scores(t, slot):
        rows = pl.ds(pl.multiple_of(t * tk, tk), tk)
        s_refs[slot][...] = _dot(k_ref[0, rows, :], qqt)

    def softmax(slot, carry, mask):
        st = s_refs[slot][...]
        if mask is not None:
            st = jnp.where(mask, st, NEG_BIG)
        if bounded:
            p_refs[slot][...] = jnp.exp2(st).astype(BF16)
            return carry
        m_new = jnp.maximum(carry[0], jnp.max(st, axis=0, keepdims=True))
        p_refs[slot][...] = jnp.exp2(st - m_new).astype(BF16)
        return m_new, jnp.exp2(carry[0] - m_new)

    ones_rows = jnp.ones((SUM_ROWS, tk), BF16)

    def fold_values(t, slot, alpha):
        vt1 = jnp.concatenate([vt_ref[0, t], ones_rows], axis=0)
        pv = _dot(vt1, p_refs[slot][...])
        acc_ref[...] = acc_ref[...] + pv if bounded else alpha * acc_ref[...] + pv

    def step(t, slot, carry, mask=None, last=False):
        fold_values(jnp.maximum(t - 1, 0), 1 - slot, carry[1])
        if not last:
            scores(t + 1, 1 - slot)
        return softmax(slot, carry, mask)

    acc_ref[...] = jnp.zeros_like(acc_ref)
    p1_ref[...] = jnp.zeros_like(p1_ref)
    scores(0, 0)
    carry = (jnp.full((1, 2 * tq), NEG_BIG, F32), jnp.ones((1, 2 * tq), F32))

    def pair(jj, c):
        return step(2 * jj + 1, 1, step(2 * jj, 0, c))

    def quad(jj, c):
        return pair(2 * jj + 1, pair(2 * jj, c))

    carry = lax.fori_loop(0, lax.shift_right_logical(i, 1), quad, carry)
    carry = lax.cond((i & 1) == 1, lambda c: pair(i - 1, c), lambda c: c, carry)
    key = lax.broadcasted_iota(jnp.int32, (tk, 2 * tq), 0)
    qry = lax.broadcasted_iota(jnp.int32, (tk, 2 * tq), 1) % tq
    carry = step(2 * i, 0, carry, mask=(key // CHUNK) <= (qry // CHUNK))
    carry = step(2 * i + 1, 1, carry, mask=((tk + key) // CHUNK) <= (qry // CHUNK), last=True)
    fold_values(2 * i + 1, 1, carry[1])
    lam = _lam(lamq_ref, lamk_ref, lam_init)
    acc = acc_ref[:HEAD_W, :]
    l = acc_ref[HEAD_W:HEAD_W + 1, :]
    ot = acc[:, :tq] / l[:, :tq] - lam * (acc[:, tq:] / l[:, tq:])
    q_rows = pl.ds(pl.multiple_of(i * tq, tq), tq)
    o_ref[0, q_rows, :] = _merge_out(ot.T, gsub_ref, ga_ref[0, q_rows, :], ra_ref[0, q_rows, :], lam_init)


def _attn_prompt_kernel(bound_ref, *refs, **static):
    bounded = bound_ref[0, 0] <= EXP2_SAFE_RANGE
    pl.when(bounded)(lambda: _attn_prompt_body(*refs, bounded=True, **static))
    pl.when(jnp.logical_not(bounded))(lambda: _attn_prompt_body(*refs, bounded=False, **static))


def _attn_prompt(bound, qt, kb, vt, ga, ra, lam_q, lam_k, g_sub, *, tq, lam_init):
    B, S, _ = kb.shape
    nt, tk = vt.shape[1], vt.shape[3]
    assert qt.shape[1:] == (S // tq, D_MODEL, tq)
    blk = pl.BlockSpec((1, S, HEAD_W), lambda b, h: (b, 0, h))
    small = lambda shape: pl.BlockSpec(shape, lambda b, h: (0, 0))
    return pl.pallas_call(
        functools.partial(_attn_prompt_kernel, tq=tq, tk=tk, lam_init=lam_init),
        out_shape=jax.ShapeDtypeStruct((B, S, D_MODEL), BF16),
        grid=(B, HEADS),
        in_specs=[pl.BlockSpec(memory_space=pltpu.SMEM),
                  pl.BlockSpec((1, S // tq, HEAD_W, tq), lambda b, h: (b, 0, h, 0)),
                  blk,
                  pl.BlockSpec((1, nt, HEAD_W, tk), lambda b, h: (b, 0, h, 0)),
                  blk, blk, small((2, A_DIM)), small((2, A_DIM)), small((1, HEAD_W))],
        out_specs=blk, name="attn_prompt",
        scratch_shapes=[pltpu.VMEM((HEAD_W + SUM_ROWS, 2 * tq), F32),
                        pltpu.VMEM((tk, 2 * tq), F32), pltpu.VMEM((tk, 2 * tq), F32),
                        pltpu.VMEM((tk, 2 * tq), BF16), pltpu.VMEM((tk, 2 * tq), BF16)],
        compiler_params=pltpu.CompilerParams(dimension_semantics=("arbitrary",) * 2,
                                             vmem_limit_bytes=VMEM_LIMIT),
    )(bound, qt, kb, vt, ga, ra, lam_q, lam_k, g_sub)


def _attn_sample_kernel(q_ref, kc_ref, vc_ref, kn_ref, vn_ref, ga_ref, ra_ref, lamq_ref, lamk_ref,
                        gsub_ref, o_ref, *, lam_init):
    t = q_ref.shape[1]
    half = HEADS // 2
    n_mixed = kc_ref.shape[1] // half
    lam = _lam(lamq_ref, lamk_ref, lam_init)
    row = lax.broadcasted_iota(jnp.int32, (4 * t, n_mixed), 0)
    col = lax.broadcasted_iota(jnp.int32, (4 * t, n_mixed), 1)
    own_head = (col & 1) == (row >= 2 * t).astype(jnp.int32)
    for h in range(half):
        heads = (h, h + half)
        sls = [slice(hh * HEAD_W, (hh + 1) * HEAD_W) for hh in heads]
        pair_rows = pl.ds(h, n_mixed, stride=half)
        qqs = [_split_maps(q_ref[0, :, sl]) for sl in sls]
        s_c = _dot_nt(jnp.concatenate(qqs, axis=0), kc_ref[0, pair_rows, :].astype(BF16))
        s_c = jnp.where(own_head, s_c, NEG_BIG)
        s_n = jnp.concatenate([_dot_nt(qq, kn_ref[0, :, sl]) for qq, sl in zip(qqs, sls)], axis=0)
        m = jnp.maximum(jnp.max(s_c, axis=-1, keepdims=True), jnp.max(s_n, axis=-1, keepdims=True))
        p_c = jnp.exp2(s_c - m)
        p_n = jnp.exp2(s_n - m)
        l = jnp.sum(p_c, axis=-1, keepdims=True) + jnp.sum(p_n, axis=-1, keepdims=True)
        acc = _dot(p_c.astype(BF16), vc_ref[0, pair_rows, :].astype(BF16))
        for j, sl in enumerate(sls):
            r0 = 2 * t * j
            a = acc[r0:r0 + 2 * t] + _dot(p_n[r0:r0 + 2 * t].astype(BF16), vn_ref[0, :, sl])
            lj = l[r0:r0 + 2 * t]
            o = a[:t] / lj[:t] - lam * (a[t:] / lj[t:])
            o_ref[0, :, sl] = _merge_out(o, gsub_ref, ga_ref[0, :, sl], ra_ref[0, :, sl], lam_init)


def _attn_sample(qa, kc, vc, kn, vn, ga, ra, lam_q, lam_k, g_sub, *, lam_init):
    DB, T, _ = qa.shape
    blk = pl.BlockSpec((1, T, D_MODEL), lambda b: (b, 0, 0))
    cache = pl.BlockSpec((1,) + kc.shape[1:], lambda b: (b, 0, 0))
    small = lambda shape: pl.BlockSpec(shape, lambda b: (0, 0))
    return pl.pallas_call(
        functools.partial(_attn_sample_kernel, lam_init=lam_init),
        out_shape=jax.ShapeDtypeStruct((DB, T, D_MODEL), BF16),
        grid=(DB,),
        in_specs=[blk, cache, cache, blk, blk, blk, blk, small((2, A_DIM)), small((2, A_DIM)),
                  small((1, HEAD_W))],
        out_specs=blk, name="attn_sample",
        compiler_params=pltpu.CompilerParams(dimension_semantics=("arbitrary",),
                                             vmem_limit_bytes=VMEM_LIMIT),
    )(qa, kc, vc, kn, vn, ga, ra, lam_q, lam_k, g_sub)


def _tail_kernel(x_ref, mix_ref, p_ref, wo_ref, gffn_ref, wg_ref, wu_ref, wd_ref, gple_ref, wple_ref,
                 wpg_ref, y_ref, act_ref):
    x1 = x_ref[...] + _dot(mix_ref[...], wo_ref[...])
    h = (_rms_scale(x1) * gffn_ref[...]).astype(BF16)
    for c in range(D_FF // MXU_N):
        sl = slice(c * MXU_N, (c + 1) * MXU_N)
        g = _dot(h, wg_ref[:, sl])
        u = _dot(h, wu_ref[:, sl])
        act_ref[:, sl] = ((g * jax.nn.sigmoid(g)) * u).astype(BF16)
    x2 = x1 + _dot(act_ref[...], wd_ref[...])
    h3 = (_rms_scale(x2) * gple_ref[...]).astype(BF16)
    gate = jax.nn.sigmoid(_dot(h3, wpg_ref[...]))
    y_ref[...] = x2 + gate * _dot(p_ref[...].astype(BF16), wple_ref[...])


def _tail(x2d, mix2d, p2d, w_o, g_ffn, w_g, w_u, w_d, g_ple, w_ple, w_pg, *, tm, name):
    n = x2d.shape[0]
    row = lambda i: (i, 0)
    const = lambda i: (0, 0)
    resident = functools.partial(pl.BlockSpec, pipeline_mode=pl.Buffered(1))
    return pl.pallas_call(
        _tail_kernel, out_shape=jax.ShapeDtypeStruct((n, D_MODEL), F32), grid=(n // tm,),
        in_specs=[pl.BlockSpec((tm, D_MODEL), row), pl.BlockSpec((tm, D_MODEL), row),
                  pl.BlockSpec((tm, PLE_DIM), row),
                  resident((D_MODEL, D_MODEL), const), resident((1, D_MODEL), const),
                  resident((D_MODEL, D_FF), const), resident((D_MODEL, D_FF), const),
                  resident((D_FF, D_MODEL), const), resident((1, D_MODEL), const),
                  resident((PLE_DIM, D_MODEL), const), resident((D_MODEL, D_MODEL), const)],
        out_specs=pl.BlockSpec((tm, D_MODEL), row),
        scratch_shapes=[pltpu.VMEM((tm, D_FF), BF16)], name=name,
        compiler_params=pltpu.CompilerParams(dimension_semantics=("arbitrary",),
                                             vmem_limit_bytes=VMEM_LIMIT),
    )(x2d, mix2d, p2d, w_o, g_ffn, w_g, w_u, w_d, g_ple, w_ple, w_pg)


def kernel(x_prompt, x_sample, cache_attn_k, cache_attn_v, state_ret, p_prompt, p_sample, w_in, g_mix_norm, g_q_norm, g_k_norm, lam_q, lam_k, g_sub_norm, g_ret_norm, w_o, g_ffn_norm, w_ff_gate, w_ff_up, w_ff_down, g_ple_norm, w_ple, w_ple_gate):
    B, S, D = x_prompt.shape
    DB, T, _ = x_sample.shape
    P = cache_attn_k.shape[2]
    depth = w_in.shape[0]
    assert depth == 1 and D == D_MODEL and S % 512 == 0 and 256 % T == 0
    l = 0
    lam_init = 0.8 - 0.6 * math.exp(-0.3 * l)
    tm = 256

    w_in_b = w_in[l].astype(BF16)
    tail_w = (w_o[l].astype(BF16), g_ffn_norm[l][None, :], w_ff_gate[l].astype(BF16),
              w_ff_up[l].astype(BF16), w_ff_down[l].astype(BF16), g_ple_norm[l][None, :],
              w_ple[l].astype(BF16), w_ple_gate[l].astype(BF16))
    g_mix = g_mix_norm[l][None, :]
    gq = jnp.tile(g_q_norm[l], PROJ_N // A_DIM)[None, :]
    gk = jnp.tile(g_k_norm[l], PROJ_N // A_DIM)[None, :]
    g_sub = g_sub_norm[l][None, :]
    g_rn = g_ret_norm[l]
    grp = jnp.arange(MXU_N) // A_DIM
    gmat = (grp[:, None] == grp[None, :]).astype(BF16)

    cos_p, sin_p = _rope_tables(jnp.arange(S))
    xp2 = x_prompt.reshape(B * S, D)
    qt, kf, kb, vf, vt, ra, ga, st_p = _inproj(
        xp2, w_in_b, g_mix, gq, gk, cos_p, sin_p, gmat, _ret_tables(CHUNK), g_rn,
        tm=512, seq=S, state0=None)
    r3 = lambda a: a.reshape(B, S, D)
    score_bound = ((A_DIM * Q_SCALE * 1.01) * jnp.max(jnp.abs(g_q_norm[l]))
                   * jnp.max(jnp.abs(g_k_norm[l]))).reshape(1, 1)
    mix_p = _attn_prompt(score_bound, qt, r3(kb), vt, r3(ga), r3(ra), lam_q[l], lam_k[l], g_sub,
                         tq=512, lam_init=lam_init)
    y_p = _tail(xp2, mix_p.reshape(B * S, D), p_prompt[l].reshape(B * S, PLE_DIM), *tail_w,
                tm=tm, name="tail_prompt")

    cos_s, sin_s = _rope_tables(P + jnp.arange(T))
    reps = tm // T
    xs2 = x_sample.reshape(DB * T, D)
    st0 = state_ret[l].reshape(DB, HEADS * R_DK, HEAD_W)
    qa_s, kf_s, kb_s, vf_s, vb_s, ra_s, ga_s, st_s = _inproj(
        xs2, w_in_b, g_mix, gq, gk, jnp.tile(cos_s, (reps, 1)), jnp.tile(sin_s, (reps, 1)), gmat,
        _ret_tables(T), g_rn, tm=tm, seq=T, state0=st0)
    s3 = lambda a: a.reshape(DB, T, D)
    mix_s = _attn_sample(s3(qa_s), cache_attn_k[l].reshape(DB, P * HEADS, HEAD_W),
                         cache_attn_v[l].reshape(DB, P * HEADS, HEAD_W),
                         s3(kb_s), s3(vb_s), s3(ga_s), s3(ra_s), lam_q[l], lam_k[l], g_sub,
                         lam_init=lam_init)
    y_s = _tail(xs2, mix_s.reshape(DB * T, D), p_sample[l].reshape(DB * T, PLE_DIM), *tail_w,
                tm=tm, name="tail_sample")

    return (y_p.reshape(B, S, D), y_s.reshape(DB, T, D),
            kf.reshape(1, B, S, HEADS, HEAD_W), vf.reshape(1, B, S, HEADS, HEAD_W),
            st_p.reshape(1, B, HEADS, R_DK, HEAD_W),
            kf_s.reshape(1, DB, T, HEADS, HEAD_W), vf_s.reshape(1, DB, T, HEADS, HEAD_W),
            st_s.reshape(1, DB, HEADS, R_DK, HEAD_W))
```

```python
import functools
import math

import jax
import jax.numpy as jnp
from jax import lax
from jax.experimental import pallas as pl
from jax.experimental.pallas import tpu as pltpu

F32 = jnp.float32
BF16 = jnp.bfloat16

D_MODEL = 1024
CHUNK = 64
PLE_DIM = 256
EPS = 1e-6
HEADS = 8
HEAD_W = 128
A_DIM = 64
R_DK = 64
ROPE_BASE = 10000.0
D_FF = 2816
N_PAIR = HEADS // 2

OFF_QA, OFF_KA, OFF_VA = 0, 1024, 2048
OFF_QR, OFF_KR, OFF_VR = 3072, 3584, 4096
OFF_GRET, OFF_GA, OFF_GR = 5120, 6144, 7168
IN_WIDTH = 8192

MXU_N = 256
PROJ_N = 2 * MXU_N
VMEM_LIMIT = 56 * 1024 * 1024
NEG_BIG = -1e30
SUM_ROWS = 16
ATT_TK = 256
ATT_UNROLL = 8
EXP2_SAFE_RANGE = 60.0
Q_SCALE = (A_DIM ** -0.5) * math.log2(math.e)

_NT = (((1,), (1,)), ((), ()))


def _dot(a, b):
    return jnp.dot(a, b, preferred_element_type=F32)


def _dot_nt(a, b):
    return lax.dot_general(a, b, _NT, preferred_element_type=F32)


def _rms_scale(x):
    return x * lax.rsqrt(jnp.mean(x * x, axis=-1, keepdims=True) + EPS)


def _inproj_body(x_ref, w_ref, gmix_ref, gq_ref, gk_ref, cos_ref, sin_ref, gmat_ref,
                 dpair_ref, qdec_ref, kdec_ref, cdec_ref, grn_ref,
                 kf_ref, kb_ref, vf_ref, ra_ref, ga_ref,
                 qs_ref, ks_ref, vs_ref, gs_ref, *, L, get_state, set_state, store_q, store_v,
                 interleave):
    tm = x_ref.shape[0]
    x = x_ref[...]
    h = (_rms_scale(x) * gmix_ref[...]).astype(BF16)
    gmat = gmat_ref[...]

    def proj(off, c):
        return _dot(h, w_ref[:, off + c * PROJ_N: off + (c + 1) * PROJ_N])

    def group_norm(z, g_ref):
        zz = (z * z).astype(BF16)
        ss = jnp.concatenate([_dot(zz[:, j * MXU_N:(j + 1) * MXU_N], gmat)
                              for j in range(PROJ_N // MXU_N)], axis=1)
        return (z * lax.rsqrt(ss * (1.0 / A_DIM) + EPS)) * g_ref[...]

    for c in range(D_MODEL // PROJ_N):
        sl = slice(c * PROJ_N, (c + 1) * PROJ_N)
        vs_ref[:, sl] = proj(OFF_VR, c).astype(BF16)
        g_ret = proj(OFF_GRET, c)
        gate_r = proj(OFF_GR, c)
        gs_ref[:, sl] = (g_ret * jax.nn.sigmoid(g_ret)) * jax.nn.sigmoid(gate_r)

    lane = lax.broadcasted_iota(jnp.int32, (tm, HEAD_W), 1)
    first_half = (lane % R_DK) < (R_DK // 2)
    cos = cos_ref[...]
    sin = sin_ref[...]

    def rotary(z):
        partner = jnp.where(first_half, pltpu.roll(z, HEAD_W - R_DK // 2, axis=1),
                            pltpu.roll(z, R_DK // 2, axis=1))
        return z * cos + partner * sin

    assert HEADS * R_DK == PROJ_N
    zq = proj(OFF_QR, 0)
    zk = proj(OFF_KR, 0)
    for g in range(N_PAIR):
        hs = slice(g * HEAD_W, (g + 1) * HEAD_W)
        qs_ref[:, hs] = rotary(zq[:, hs]).astype(BF16)
        ks_ref[:, hs] = rotary(zk[:, hs]) * (R_DK ** -0.5)

    lane_l = lax.broadcasted_iota(jnp.int32, (L, HEAD_W), 1)
    even = lane_l < R_DK

    def chunk_body(ci, carry):
        r0 = ci * L if isinstance(ci, int) else pl.multiple_of(ci * L, L)
        rows = pl.ds(r0, L)
        for g in range(N_PAIR):
            q2 = qs_ref[rows, g * HEAD_W:(g + 1) * HEAD_W]
            k2 = ks_ref[rows, g * HEAD_W:(g + 1) * HEAD_W]
            v2 = vs_ref[rows, 2 * g * HEAD_W:(2 * g + 2) * HEAD_W]
            zero = jnp.zeros_like(q2)
            qq = jnp.concatenate([jnp.where(even, q2, zero), jnp.where(even, zero, q2)], axis=0)
            a = (_dot_nt(qq, k2.astype(BF16)) * dpair_ref[g]).astype(BF16)
            intra = _dot(a, v2)
            state = get_state(ci, g)
            cross = _dot(qq, state.astype(BF16)) * qdec_ref[g]
            kd = (k2 * kdec_ref[g]).T.astype(BF16)
            kv = _dot(kd, v2)
            kv_pair = jnp.concatenate([kv[:R_DK, :HEAD_W], kv[R_DK:, HEAD_W:]], axis=0)
            set_state(ci, g, cdec_ref[g] * state + kv_pair)
            outs = (intra[:L, :HEAD_W] + cross[:L], intra[L:, HEAD_W:] + cross[L:])
            for par, o in enumerate(outs):
                hh = 2 * g + par
                cols = slice(hh * HEAD_W, (hh + 1) * HEAD_W)
                on = _rms_scale(o) * grn_ref[hh:hh + 1, :]
                ra_ref[rows, cols] = (on * gs_ref[rows, cols]).astype(BF16)
        return carry

    def q_unit(sl, c):
        store_q(sl, group_norm(proj(OFF_QA, c), gq_ref) * Q_SCALE)

    def store_cache_layout(ref, c, val):
        for j in range(PROJ_N // HEAD_W):
            head = c * (PROJ_N // HEAD_W) + j
            ref[pl.ds(head, tm, stride=HEADS), :] = val[:, j * HEAD_W:(j + 1) * HEAD_W]

    def k_unit(sl, c):
        kn = group_norm(proj(OFF_KA, c), gk_ref)
        store_cache_layout(kf_ref, c, kn)
        kb_ref[:, sl] = kn.astype(BF16)

    def v_unit(sl, c):
        v = proj(OFF_VA, c)
        store_cache_layout(vf_ref, c, v)
        store_v(sl, v)

    def gate_unit(sl, c):
        ga_ref[:, sl] = jax.nn.sigmoid(proj(OFF_GA, c)).astype(BF16)

    units = [functools.partial(u, slice(c * PROJ_N, (c + 1) * PROJ_N), c)
             for c in range(D_MODEL // PROJ_N) for u in (q_unit, k_unit, v_unit, gate_unit)]
    n_chunks = tm // L
    if interleave:
        per_chunk = -(-len(units) // n_chunks)
        for ci in range(n_chunks):
            chunk_body(ci, 0)
            for u in units[ci * per_chunk:(ci + 1) * per_chunk]:
                u()
    else:
        lax.fori_loop(0, n_chunks, chunk_body, 0)
        for u in units:
            u()


def _inproj_prompt_kernel(*refs):
    ins, outs = refs[:13], refs[13:]
    (qt_ref, kf_ref, kb_ref, vf_ref, vt_ref, ra_ref, ga_ref, st_ref,
     qs_ref, ks_ref, vs_ref, gs_ref) = outs

    @pl.when(pl.program_id(1) == 0)
    def _():
        st_ref[...] = jnp.zeros_like(st_ref)

    def get_state(ci, g):
        return st_ref[0, g * HEAD_W:(g + 1) * HEAD_W, :]

    def set_state(ci, g, val):
        st_ref[0, g * HEAD_W:(g + 1) * HEAD_W, :] = val

    def store_q(sl, val):
        qt_ref[0, 0, sl, :] = val.T.astype(BF16)

    def store_v(sl, val):
        for kb_i in range(val.shape[0] // ATT_TK):
            vt_ref[0, kb_i, sl, :] = val[kb_i * ATT_TK:(kb_i + 1) * ATT_TK].T.astype(BF16)

    _inproj_body(*ins, kf_ref, kb_ref, vf_ref, ra_ref, ga_ref, qs_ref, ks_ref, vs_ref, gs_ref,
                 L=CHUNK, get_state=get_state, set_state=set_state, store_q=store_q, store_v=store_v,
                 interleave=True)


def _inproj_sample_kernel(*refs, L):
    ins, s0_ref, outs = refs[:13], refs[13], refs[14:]
    (qa_ref, kf_ref, kb_ref, vf_ref, vb_ref, ra_ref, ga_ref, st_ref,
     qs_ref, ks_ref, vs_ref, gs_ref) = outs

    def get_state(ci, g):
        return s0_ref[ci, g * HEAD_W:(g + 1) * HEAD_W, :]

    def set_state(ci, g, val):
        st_ref[ci, g * HEAD_W:(g + 1) * HEAD_W, :] = val

    def store_q(sl, val):
        qa_ref[:, sl] = val.astype(BF16)

    def store_v(sl, val):
        vb_ref[:, sl] = val.astype(BF16)

    _inproj_body(*ins, kf_ref, kb_ref, vf_ref, ra_ref, ga_ref, qs_ref, ks_ref, vs_ref, gs_ref,
                 L=L, get_state=get_state, set_state=set_state, store_q=store_q, store_v=store_v,
                 interleave=True)


def _ret_tables(T):
    log_g = jnp.log1p(-jnp.exp2(-5.0 - jnp.arange(HEADS, dtype=F32)))
    i = jnp.arange(T, dtype=F32)
    diff = i[:, None] - i[None, :]
    d_mat = jnp.where(diff >= 0, jnp.exp(log_g[:, None, None] * jnp.maximum(diff, 0.0)), 0.0)
    q_decay = jnp.exp(log_g[None, :] * (i[:, None] + 1.0))
    k_decay = jnp.exp(log_g[None, :] * (T - 1.0 - i[:, None]))
    chunk_decay = jnp.exp(log_g * T)
    dpair = d_mat.reshape(N_PAIR, 2 * T, T)
    qdec = jnp.broadcast_to(q_decay.T.reshape(N_PAIR, 2 * T, 1), (N_PAIR, 2 * T, HEAD_W))
    kdec = jnp.repeat(k_decay.reshape(T, N_PAIR, 2), R_DK, axis=2).transpose(1, 0, 2)
    cdec = jnp.broadcast_to(jnp.repeat(chunk_decay.reshape(N_PAIR, 2), R_DK, axis=1)[:, :, None],
                            (N_PAIR, 2 * R_DK, HEAD_W))
    return dpair, qdec, kdec, cdec


def _rope_tables(pos):
    half = R_DK // 2
    inv_freq = ROPE_BASE ** (-jnp.arange(half, dtype=F32) / half)
    ang = pos.astype(F32)[:, None] * inv_freq[None, :]
    cos, sin = jnp.cos(ang), jnp.sin(ang)
    return jnp.tile(cos, (1, 4)), jnp.tile(jnp.concatenate([-sin, sin], axis=1), (1, 2))


def _inproj(x2d, w_in, g_mix, gq, gk, cos_t, sin_t, gmat, tables, g_rn, *, tm, seq, state0):
    n = x2d.shape[0]
    prompt = state0 is None
    dpair, qdec, kdec, cdec = tables
    nt = seq // tm if prompt else 1
    nb = n // seq if prompt else n // tm
    per_tile = tm // seq if not prompt else 1

    const2 = lambda *_: (0, 0)
    const3 = lambda *_: (0, 0, 0)
    row = lambda b, i: (b * nt + i, 0)
    resident = functools.partial(pl.BlockSpec, pipeline_mode=pl.Buffered(1))
    in_specs = [
        pl.BlockSpec((tm, D_MODEL), row),
        resident((D_MODEL, IN_WIDTH), const2),
        resident((1, D_MODEL), const2),
        resident((1, PROJ_N), const2),
        resident((1, PROJ_N), const2),
        pl.BlockSpec((tm, HEAD_W), (lambda b, i: (i, 0)) if prompt else const2),
        pl.BlockSpec((tm, HEAD_W), (lambda b, i: (i, 0)) if prompt else const2),
        resident((MXU_N, MXU_N), const2),
        resident(dpair.shape, const3),
        resident(qdec.shape, const3),
        resident(kdec.shape, const3),
        resident(cdec.shape, const3),
        resident((HEADS, HEAD_W), const2),
    ]
    args = [x2d, w_in, g_mix, gq, gk, cos_t, sin_t, gmat, dpair, qdec, kdec, cdec, g_rn]
    st_rows = 2 * R_DK * N_PAIR
    if prompt:
        st_spec = pl.BlockSpec((1, st_rows, HEAD_W), lambda b, i: (b, 0, 0))
        st_shape = jax.ShapeDtypeStruct((nb, st_rows, HEAD_W), F32)
        kernel = _inproj_prompt_kernel
        L = CHUNK
    else:
        st_spec = pl.BlockSpec((per_tile, st_rows, HEAD_W), lambda b, i: (b, 0, 0))
        st_shape = jax.ShapeDtypeStruct(state0.shape, F32)
        in_specs.append(st_spec)
        args.append(state0)
        kernel = functools.partial(_inproj_sample_kernel, L=seq)
        L = seq
    tok = lambda dt: jax.ShapeDtypeStruct((n, D_MODEL), dt)
    tok_spec = pl.BlockSpec((tm, D_MODEL), row)
    cache = jax.ShapeDtypeStruct((n * HEADS, HEAD_W), F32)
    cache_spec = pl.BlockSpec((tm * HEADS, HEAD_W), row)
    out_shape = [tok(BF16), cache, tok(BF16), cache, tok(BF16), tok(BF16), tok(BF16), st_shape]
    out_specs = [tok_spec, cache_spec, tok_spec, cache_spec, tok_spec, tok_spec, tok_spec, st_spec]
    if prompt:
        out_shape[0] = jax.ShapeDtypeStruct((nb, nt, D_MODEL, tm), BF16)
        out_specs[0] = pl.BlockSpec((1, 1, D_MODEL, tm), lambda b, i: (b, i, 0, 0))
        out_shape[4] = jax.ShapeDtypeStruct((nb, seq // ATT_TK, D_MODEL, ATT_TK), BF16)
        out_specs[4] = pl.BlockSpec((1, tm // ATT_TK, D_MODEL, ATT_TK), lambda b, i: (b, i, 0, 0))
    scratch = [pltpu.VMEM((tm, HEADS * R_DK), BF16), pltpu.VMEM((tm, HEADS * R_DK), F32),
               pltpu.VMEM((tm, D_MODEL), BF16), pltpu.VMEM((tm, D_MODEL), F32)]
    return pl.pallas_call(
        kernel, out_shape=out_shape, grid=(nb, nt), in_specs=in_specs, out_specs=out_specs,
        scratch_shapes=scratch, name="inproj_prompt" if prompt else "inproj_sample",
        compiler_params=pltpu.CompilerParams(dimension_semantics=("arbitrary", "arbitrary"),
                                             vmem_limit_bytes=VMEM_LIMIT),
    )(*args)


def _lam(lamq_ref, lamk_ref, lam_init):
    e = jnp.exp(jnp.sum(lamq_ref[...] * lamk_ref[...], axis=-1, keepdims=True))
    return e[0:1, :] - e[1:2, :] + lam_init


def _split_maps(q):
    lane = lax.broadcasted_iota(jnp.int32, q.shape, 1)
    zero = jnp.zeros_like(q)
    return jnp.concatenate([jnp.where(lane < A_DIM, q, zero), jnp.where(lane < A_DIM, zero, q)], axis=0)


def _merge_out(o, gsub_ref, ga, ra, lam_init):
    on = (_rms_scale(o) * gsub_ref[...]) * (1.0 - lam_init)
    return (ga.astype(F32) * on + ra.astype(F32)).astype(BF16)


def _attn_prompt_body(steps_ref, qt_ref, k_ref, vt_ref, ga_ref, ra_ref, lamq_ref, lamk_ref, gsub_ref, o_ref,
                      acc_ref, qq_ref, m_ref, d_ref, s0_ref, s1_ref, p0_ref, p1_ref,
                      *, tq, tk, lam_init, bounded):
    n_q = qt_ref.shape[1]
    n_steps = steps_ref.shape[0] - 2
    assert n_steps % ATT_UNROLL == 0 and tq % tk == 0
    s_refs, p_refs = (s0_ref, s1_ref), (p0_ref, p1_ref)
    ones_rows = jnp.ones((SUM_ROWS, tk), BF16)

    def scores(s, slot):
        t, i = steps_ref[s, 0], steps_ref[s, 1]
        rows = pl.ds(pl.multiple_of(t * tk, tk), tk)
        s_refs[slot][...] = _dot(k_ref[0, rows, :], qq_ref[i])

    def softmax(s, slot, alpha):
        t, i = steps_ref[s, 0], steps_ref[s, 1]
        visible = d_ref[...] <= (tq // CHUNK) * i - (tk // CHUNK) * t
        st = jnp.where(visible, s_refs[slot][...], NEG_BIG)
        if bounded:
            p_refs[slot][...] = jnp.exp2(st).astype(BF16)
            return alpha
        m_old = m_ref[i]
        m_new = jnp.maximum(m_old, jnp.max(st, axis=0, keepdims=True))
        m_ref[i] = m_new
        p_refs[slot][...] = jnp.exp2(st - m_new).astype(BF16)
        return jnp.exp2(m_old - m_new)

    def fold_values(s, slot, alpha):
        t, i = steps_ref[s, 0], steps_ref[s, 1]
        vt1 = jnp.concatenate([vt_ref[0, t], ones_rows], axis=0)
        pv = _dot(vt1, p_refs[slot][...])
        acc_ref[i] = acc_ref[i] + pv if bounded else alpha * acc_ref[i] + pv

    def step(s, slot, alpha):
        fold_values(s - 1, 1 - slot, alpha)
        scores(s + 1, 1 - slot)
        return softmax(s, slot, alpha)

    feat = lax.broadcasted_iota(jnp.int32, (HEAD_W, tq), 0)
    for i in range(n_q):
        qt = qt_ref[0, i]
        zero = jnp.zeros_like(qt)
        qq_ref[i] = jnp.concatenate([jnp.where(feat < A_DIM, qt, zero), jnp.where(feat < A_DIM, zero, qt)],
                                    axis=1)
    key = lax.broadcasted_iota(jnp.int32, (tk, 2 * tq), 0)
    qry = lax.broadcasted_iota(jnp.int32, (tk, 2 * tq), 1) % tq
    d_ref[...] = key // CHUNK - qry // CHUNK
    acc_ref[...] = jnp.zeros_like(acc_ref)
    m_ref[...] = jnp.full_like(m_ref, NEG_BIG)
    p0_ref[...] = jnp.zeros_like(p0_ref)
    scores(1, 1)

    def body(jj, alpha):
        for k in range(ATT_UNROLL):
            alpha = step(ATT_UNROLL * jj + 1 + k, (1 + k) & 1, alpha)
        return alpha

    alpha = lax.fori_loop(0, n_steps // ATT_UNROLL, body, jnp.ones((1, 2 * tq), F32))
    fold_values(n_steps, n_steps & 1, alpha)

    lam = _lam(lamq_ref, lamk_ref, lam_init)

    def finish(i, carry):
        acc = acc_ref[i]
        l = acc[HEAD_W:HEAD_W + 1, :]
        ot = acc[:HEAD_W, :tq] / l[:, :tq] - lam * (acc[:HEAD_W, tq:] / l[:, tq:])
        q_rows = pl.ds(pl.multiple_of(i * tq, tq), tq)
        o_ref[0, q_rows, :] = _merge_out(ot.T, gsub_ref, ga_ref[0, q_rows, :], ra_ref[0, q_rows, :], lam_init)
        return carry

    lax.fori_loop(0, n_q, finish, 0)


def _attn_prompt_kernel(bound_ref, *refs, **static):
    bounded = bound_ref[0, 0] <= EXP2_SAFE_RANGE
    pl.when(bounded)(lambda: _attn_prompt_body(*refs, bounded=True, **static))
    pl.when(jnp.logical_not(bounded))(lambda: _attn_prompt_body(*refs, bounded=False, **static))


def _attn_steps(n_q, blocks_per_q):
    pairs = [(t, i) for t in range(n_q * blocks_per_q) for i in range(t // blocks_per_q, n_q)]
    return jnp.asarray([pairs[0]] + pairs + [pairs[-1]], jnp.int32)


def _attn_prompt(bound, qt, kb, vt, ga, ra, lam_q, lam_k, g_sub, *, tq, lam_init):
    B, S, _ = kb.shape
    nt, tk = vt.shape[1], vt.shape[3]
    n_q = S // tq
    assert qt.shape[1:] == (n_q, D_MODEL, tq)
    blk = pl.BlockSpec((1, S, HEAD_W), lambda b, h: (b, 0, h))
    small = lambda shape: pl.BlockSpec(shape, lambda b, h: (0, 0))
    smem = pl.BlockSpec(memory_space=pltpu.SMEM)
    return pl.pallas_call(
        functools.partial(_attn_prompt_kernel, tq=tq, tk=tk, lam_init=lam_init),
        out_shape=jax.ShapeDtypeStruct((B, S, D_MODEL), BF16),
        grid=(B, HEADS),
        in_specs=[smem, smem,
                  pl.BlockSpec((1, n_q, HEAD_W, tq), lambda b, h: (b, 0, h, 0)),
                  blk,
                  pl.BlockSpec((1, nt, HEAD_W, tk), lambda b, h: (b, 0, h, 0)),
                  blk, blk, small((2, A_DIM)), small((2, A_DIM)), small((1, HEAD_W))],
        out_specs=blk, name="attn_prompt",
        scratch_shapes=[pltpu.VMEM((n_q, HEAD_W + SUM_ROWS, 2 * tq), F32),
                        pltpu.VMEM((n_q, HEAD_W, 2 * tq), BF16),
                        pltpu.VMEM((n_q, 1, 2 * tq), F32),
                        pltpu.VMEM((tk, 2 * tq), jnp.int32),
                        pltpu.VMEM((tk, 2 * tq), F32), pltpu.VMEM((tk, 2 * tq), F32),
                        pltpu.VMEM((tk, 2 * tq), BF16), pltpu.VMEM((tk, 2 * tq), BF16)],
        compiler_params=pltpu.CompilerParams(dimension_semantics=("arbitrary",) * 2,
                                             vmem_limit_bytes=VMEM_LIMIT),
    )(bound, _attn_steps(n_q, tq // tk), qt, kb, vt, ga, ra, lam_q, lam_k, g_sub)


def _attn_sample_kernel(q_ref, kc_ref, vc_ref, kn_ref, vn_ref, ga_ref, ra_ref, lamq_ref, lamk_ref,
                        gsub_ref, o_ref, *, lam_init):
    t = q_ref.shape[1]
    half = HEADS // 2
    n_mixed = kc_ref.shape[1] // half
    lam = _lam(lamq_ref, lamk_ref, lam_init)
    row = lax.broadcasted_iota(jnp.int32, (4 * t, n_mixed), 0)
    col = lax.broadcasted_iota(jnp.int32, (4 * t, n_mixed), 1)
    own_head = (col & 1) == (row >= 2 * t).astype(jnp.int32)
    for h in range(half):
        heads = (h, h + half)
        sls = [slice(hh * HEAD_W, (hh + 1) * HEAD_W) for hh in heads]
        pair_rows = pl.ds(h, n_mixed, stride=half)
        qqs = [_split_maps(q_ref[0, :, sl]) for sl in sls]
        s_c = _dot_nt(jnp.concatenate(qqs, axis=0), kc_ref[0, pair_rows, :].astype(BF16))
        s_c = jnp.where(own_head, s_c, NEG_BIG)
        s_n = jnp.concatenate([_dot_nt(qq, kn_ref[0, :, sl]) for qq, sl in zip(qqs, sls)], axis=0)
        m = jnp.maximum(jnp.max(s_c, axis=-1, keepdims=True), jnp.max(s_n, axis=-1, keepdims=True))
        p_c = jnp.exp2(s_c - m)
        p_n = jnp.exp2(s_n - m)
        l = jnp.sum(p_c, axis=-1, keepdims=True) + jnp.sum(p_n, axis=-1, keepdims=True)
        acc = _dot(p_c.astype(BF16), vc_ref[0, pair_rows, :].astype(BF16))
        for j, sl in enumerate(sls):
            r0 = 2 * t * j
            a = acc[r0:r0 + 2 * t] + _dot(p_n[r0:r0 + 2 * t].astype(BF16), vn_ref[0, :, sl])
            lj = l[r0:r0 + 2 * t]
            o = a[:t] / lj[:t] - lam * (a[t:] / lj[t:])
            o_ref[0, :, sl] = _merge_out(o, gsub_ref, ga_ref[0, :, sl], ra_ref[0, :, sl], lam_init)


def _attn_sample(qa, kc, vc, kn, vn, ga, ra, lam_q, lam_k, g_sub, *, lam_init):
    DB, T, _ = qa.shape
    blk = pl.BlockSpec((1, T, D_MODEL), lambda b: (b, 0, 0))
    cache = pl.BlockSpec((1,) + kc.shape[1:], lambda b: (b, 0, 0))
    small = lambda shape: pl.BlockSpec(shape, lambda b: (0, 0))
    return pl.pallas_call(
        functools.partial(_attn_sample_kernel, lam_init=lam_init),
        out_shape=jax.ShapeDtypeStruct((DB, T, D_MODEL), BF16),
        grid=(DB,),
        in_specs=[blk, cache, cache, blk, blk, blk, blk, small((2, A_DIM)), small((2, A_DIM)),
                  small((1, HEAD_W))],
        out_specs=blk, name="attn_sample",
        compiler_params=pltpu.CompilerParams(dimension_semantics=("arbitrary",),
                                             vmem_limit_bytes=VMEM_LIMIT),
    )(qa, kc, vc, kn, vn, ga, ra, lam_q, lam_k, g_sub)


def _tail_kernel(x_ref, mix_ref, p_ref, wo_ref, gffn_ref, wg_ref, wu_ref, wd_ref, gple_ref, wple_ref,
                 wpg_ref, y_ref, act_ref):
    x1 = x_ref[...] + _dot(mix_ref[...], wo_ref[...])
    h = (_rms_scale(x1) * gffn_ref[...]).astype(BF16)
    for c in range(D_FF // MXU_N):
        sl = slice(c * MXU_N, (c + 1) * MXU_N)
        g = _dot(h, wg_ref[:, sl])
        u = _dot(h, wu_ref[:, sl])
        act_ref[:, sl] = ((g * jax.nn.sigmoid(g)) * u).astype(BF16)
    x2 = x1 + _dot(act_ref[...], wd_ref[...])
    h3 = (_rms_scale(x2) * gple_ref[...]).astype(BF16)
    gate = jax.nn.sigmoid(_dot(h3, wpg_ref[...]))
    y_ref[...] = x2 + gate * _dot(p_ref[...].astype(BF16), wple_ref[...])


def _tail(x2d, mix2d, p2d, w_o, g_ffn, w_g, w_u, w_d, g_ple, w_ple, w_pg, *, tm, name):
    n = x2d.shape[0]
    row = lambda i: (i, 0)
    const = lambda i: (0, 0)
    resident = functools.partial(pl.BlockSpec, pipeline_mode=pl.Buffered(1))
    return pl.pallas_call(
        _tail_kernel, out_shape=jax.ShapeDtypeStruct((n, D_MODEL), F32), grid=(n // tm,),
        in_specs=[pl.BlockSpec((tm, D_MODEL), row), pl.BlockSpec((tm, D_MODEL), row),
                  pl.BlockSpec((tm, PLE_DIM), row),
                  resident((D_MODEL, D_MODEL), const), resident((1, D_MODEL), const),
                  resident((D_MODEL, D_FF), const), resident((D_MODEL, D_FF), const),
                  resident((D_FF, D_MODEL), const), resident((1, D_MODEL), const),
                  resident((PLE_DIM, D_MODEL), const), resident((D_MODEL, D_MODEL), const)],
        out_specs=pl.BlockSpec((tm, D_MODEL), row),
        scratch_shapes=[pltpu.VMEM((tm, D_FF), BF16)], name=name,
        compiler_params=pltpu.CompilerParams(dimension_semantics=("arbitrary",),
                                             vmem_limit_bytes=VMEM_LIMIT),
    )(x2d, mix2d, p2d, w_o, g_ffn, w_g, w_u, w_d, g_ple, w_ple, w_pg)


def kernel(x_prompt, x_sample, cache_attn_k, cache_attn_v, state_ret, p_prompt, p_sample, w_in, g_mix_norm, g_q_norm, g_k_norm, lam_q, lam_k, g_sub_norm, g_ret_norm, w_o, g_ffn_norm, w_ff_gate, w_ff_up, w_ff_down, g_ple_norm, w_ple, w_ple_gate):
    B, S, D = x_prompt.shape
    DB, T, _ = x_sample.shape
    P = cache_attn_k.shape[2]
    depth = w_in.shape[0]
    assert depth == 1 and D == D_MODEL and S % 512 == 0 and 256 % T == 0
    l = 0
    lam_init = 0.8 - 0.6 * math.exp(-0.3 * l)
    tm = 256

    w_in_b = w_in[l].astype(BF16)
    tail_w = (w_o[l].astype(BF16), g_ffn_norm[l][None, :], w_ff_gate[l].astype(BF16),
              w_ff_up[l].astype(BF16), w_ff_down[l].astype(BF16), g_ple_norm[l][None, :],
              w_ple[l].astype(BF16), w_ple_gate[l].astype(BF16))
    g_mix = g_mix_norm[l][None, :]
    gq = jnp.tile(g_q_norm[l], PROJ_N // A_DIM)[None, :]
    gk = jnp.tile(g_k_norm[l], PROJ_N // A_DIM)[None, :]
    g_sub = g_sub_norm[l][None, :]
    g_rn = g_ret_norm[l]
    grp = jnp.arange(MXU_N) // A_DIM
    gmat = (grp[:, None] == grp[None, :]).astype(BF16)

    cos_p, sin_p = _rope_tables(jnp.arange(S))
    xp2 = x_prompt.reshape(B * S, D)
    qt, kf, kb, vf, vt, ra, ga, st_p = _inproj(
        xp2, w_in_b, g_mix, gq, gk, cos_p, sin_p, gmat, _ret_tables(CHUNK), g_rn,
        tm=512, seq=S, state0=None)
    r3 = lambda a: a.reshape(B, S, D)
    score_bound = ((A_DIM * Q_SCALE * 1.01) * jnp.max(jnp.abs(g_q_norm[l]))
                   * jnp.max(jnp.abs(g_k_norm[l]))).reshape(1, 1)
    mix_p = _attn_prompt(score_bound, qt, r3(kb), vt, r3(ga), r3(ra), lam_q[l], lam_k[l], g_sub,
                         tq=512, lam_init=lam_init)
    y_p = _tail(xp2, mix_p.reshape(B * S, D), p_prompt[l].reshape(B * S, PLE_DIM), *tail_w,
                tm=tm, name="tail_prompt")

    cos_s, sin_s = _rope_tables(P + jnp.arange(T))
    reps = tm // T
    xs2 = x_sample.reshape(DB * T, D)
    st0 = state_ret[l].reshape(DB, HEADS * R_DK, HEAD_W)
    qa_s, kf_s, kb_s, vf_s, vb_s, ra_s, ga_s, st_s = _inproj(
        xs2, w_in_b, g_mix, gq, gk, jnp.tile(cos_s, (reps, 1)), jnp.tile(sin_s, (reps, 1)), gmat,
        _ret_tables(T), g_rn, tm=tm, seq=T, state0=st0)
    s3 = lambda a: a.reshape(DB, T, D)
    mix_s = _attn_sample(s3(qa_s), cache_attn_k[l].reshape(DB, P * HEADS, HEAD_W),
                         cache_attn_v[l].reshape(DB, P * HEADS, HEAD_W),
                         s3(kb_s), s3(vb_s), s3(ga_s), s3(ra_s), lam_q[l], lam_k[l], g_sub,
                         lam_init=lam_init)
    y_s = _tail(xs2, mix_s.reshape(DB * T, D), p_sample[l].reshape(DB * T, PLE_DIM), *tail_w,
                tm=tm, name="tail_sample")

    return (y_p.reshape(B, S, D), y_s.reshape(DB, T, D),
            kf.reshape(1, B, S, HEADS, HEAD_W), vf.reshape(1, B, S, HEADS, HEAD_W),
            st_p.reshape(1, B, HEADS, R_DK, HEAD_W),
            kf_s.reshape(1, DB, T, HEADS, HEAD_W), vf_s.reshape(1, DB, T, HEADS, HEAD_W),
            st_s.reshape(1, DB, HEADS, R_DK, HEAD_W))
```

```python
import functools
import math

import jax
import jax.numpy as jnp
from jax import lax
from jax.experimental import pallas as pl
from jax.experimental.pallas import tpu as pltpu

F32 = jnp.float32
BF16 = jnp.bfloat16

D_MODEL = 1024
CHUNK = 64
PLE_DIM = 256
EPS = 1e-6
HEADS = 8
HEAD_W = 128
A_DIM = 64
R_DK = 64
ROPE_BASE = 10000.0
D_FF = 2816
N_PAIR = HEADS // 2

OFF_QA, OFF_KA, OFF_VA = 0, 1024, 2048
OFF_QR, OFF_KR, OFF_VR = 3072, 3584, 4096
OFF_GRET, OFF_GA, OFF_GR = 5120, 6144, 7168
IN_WIDTH = 8192

MXU_N = 256
PROJ_N = 2 * MXU_N
VMEM_LIMIT = 56 * 1024 * 1024
NEG_BIG = -1e30
ATT_TK = 256
ATT_UNROLL = 8
EXP2_SAFE_RANGE = 60.0
Q_SCALE = (A_DIM ** -0.5) * math.log2(math.e)

_NT = (((1,), (1,)), ((), ()))


def _dot(a, b):
    return jnp.dot(a, b, preferred_element_type=F32)


def _dot_nt(a, b):
    return lax.dot_general(a, b, _NT, preferred_element_type=F32)


def _rms_scale(x):
    return x * lax.rsqrt(jnp.mean(x * x, axis=-1, keepdims=True) + EPS)


def _inproj_body(x_ref, w_ref, gmix_ref, gq_ref, gk_ref, cos_ref, sin_ref, gmat_ref,
                 dpair_ref, qdec_ref, kdec_ref, cdec_ref, grn_ref,
                 kf_ref, kb_ref, vf_ref, ra_ref, ga_ref,
                 qs_ref, ks_ref, vs_ref, gs_ref, *, L, get_state, set_state, store_q, store_v,
                 interleave):
    tm = x_ref.shape[0]
    x = x_ref[...]
    h = (_rms_scale(x) * gmix_ref[...]).astype(BF16)
    gmat = gmat_ref[...]

    def proj(off, c):
        return _dot(h, w_ref[:, off + c * PROJ_N: off + (c + 1) * PROJ_N])

    def group_norm(z, g_ref):
        zz = (z * z).astype(BF16)
        ss = jnp.concatenate([_dot(zz[:, j * MXU_N:(j + 1) * MXU_N], gmat)
                              for j in range(PROJ_N // MXU_N)], axis=1)
        return (z * lax.rsqrt(ss * (1.0 / A_DIM) + EPS)) * g_ref[...]

    for c in range(D_MODEL // PROJ_N):
        sl = slice(c * PROJ_N, (c + 1) * PROJ_N)
        vs_ref[:, sl] = proj(OFF_VR, c).astype(BF16)
        g_ret = proj(OFF_GRET, c)
        gate_r = proj(OFF_GR, c)
        gs_ref[:, sl] = (g_ret * jax.nn.sigmoid(g_ret)) * jax.nn.sigmoid(gate_r)

    lane = lax.broadcasted_iota(jnp.int32, (tm, HEAD_W), 1)
    first_half = (lane % R_DK) < (R_DK // 2)
    cos = cos_ref[...]
    sin = sin_ref[...]

    def rotary(z):
        partner = jnp.where(first_half, pltpu.roll(z, HEAD_W - R_DK // 2, axis=1),
                            pltpu.roll(z, R_DK // 2, axis=1))
        return z * cos + partner * sin

    assert HEADS * R_DK == PROJ_N
    zq = proj(OFF_QR, 0)
    zk = proj(OFF_KR, 0)
    for g in range(N_PAIR):
        hs = slice(g * HEAD_W, (g + 1) * HEAD_W)
        qs_ref[:, hs] = rotary(zq[:, hs]).astype(BF16)
        ks_ref[:, hs] = rotary(zk[:, hs]) * (R_DK ** -0.5)

    lane_l = lax.broadcasted_iota(jnp.int32, (L, HEAD_W), 1)
    even = lane_l < R_DK

    def chunk_body(ci, carry):
        r0 = ci * L if isinstance(ci, int) else pl.multiple_of(ci * L, L)
        rows = pl.ds(r0, L)
        for g in range(N_PAIR):
            q2 = qs_ref[rows, g * HEAD_W:(g + 1) * HEAD_W]
            k2 = ks_ref[rows, g * HEAD_W:(g + 1) * HEAD_W]
            v2 = vs_ref[rows, 2 * g * HEAD_W:(2 * g + 2) * HEAD_W]
            zero = jnp.zeros_like(q2)
            qq = jnp.concatenate([jnp.where(even, q2, zero), jnp.where(even, zero, q2)], axis=0)
            a = (_dot_nt(qq, k2.astype(BF16)) * dpair_ref[g]).astype(BF16)
            intra = _dot(a, v2)
            state = get_state(ci, g)
            cross = _dot(qq, state.astype(BF16)) * qdec_ref[g]
            kd = (k2 * kdec_ref[g]).T.astype(BF16)
            kv = _dot(kd, v2)
            kv_pair = jnp.concatenate([kv[:R_DK, :HEAD_W], kv[R_DK:, HEAD_W:]], axis=0)
            set_state(ci, g, cdec_ref[g] * state + kv_pair)
            outs = (intra[:L, :HEAD_W] + cross[:L], intra[L:, HEAD_W:] + cross[L:])
            for par, o in enumerate(outs):
                hh = 2 * g + par
                cols = slice(hh * HEAD_W, (hh + 1) * HEAD_W)
                on = _rms_scale(o) * grn_ref[hh:hh + 1, :]
                ra_ref[rows, cols] = (on * gs_ref[rows, cols]).astype(BF16)
        return carry

    def q_unit(sl, c):
        store_q(sl, group_norm(proj(OFF_QA, c), gq_ref) * Q_SCALE)

    def store_cache_layout(ref, c, val):
        for j in range(PROJ_N // HEAD_W):
            head = c * (PROJ_N // HEAD_W) + j
            ref[pl.ds(head, tm, stride=HEADS), :] = val[:, j * HEAD_W:(j + 1) * HEAD_W]

    def k_unit(sl, c):
        kn = group_norm(proj(OFF_KA, c), gk_ref)
        store_cache_layout(kf_ref, c, kn)
        kb_ref[:, sl] = kn.astype(BF16)

    def v_unit(sl, c):
        v = proj(OFF_VA, c)
        store_cache_layout(vf_ref, c, v)
        store_v(sl, v)

    def gate_unit(sl, c):
        ga_ref[:, sl] = jax.nn.sigmoid(proj(OFF_GA, c)).astype(BF16)

    units = [functools.partial(u, slice(c * PROJ_N, (c + 1) * PROJ_N), c)
             for c in range(D_MODEL // PROJ_N) for u in (q_unit, k_unit, v_unit, gate_unit)]
    n_chunks = tm // L
    if interleave:
        per_chunk = -(-len(units) // n_chunks)
        for ci in range(n_chunks):
            chunk_body(ci, 0)
            for u in units[ci * per_chunk:(ci + 1) * per_chunk]:
                u()
    else:
        lax.fori_loop(0, n_chunks, chunk_body, 0)
        for u in units:
            u()


def _inproj_prompt_kernel(*refs):
    ins, outs = refs[:13], refs[13:]
    (qt_ref, kf_ref, kb_ref, vf_ref, vt_ref, ra_ref, ga_ref, st_ref,
     qs_ref, ks_ref, vs_ref, gs_ref) = outs

    @pl.when(pl.program_id(1) == 0)
    def _():
        st_ref[...] = jnp.zeros_like(st_ref)

    def get_state(ci, g):
        return st_ref[0, g * HEAD_W:(g + 1) * HEAD_W, :]

    def set_state(ci, g, val):
        st_ref[0, g * HEAD_W:(g + 1) * HEAD_W, :] = val

    def store_q(sl, val):
        qt_ref[0, 0, sl, :] = val.T.astype(BF16)

    def store_v(sl, val):
        for kb_i in range(val.shape[0] // ATT_TK):
            vt_ref[0, kb_i, sl, :] = val[kb_i * ATT_TK:(kb_i + 1) * ATT_TK].T.astype(BF16)

    _inproj_body(*ins, kf_ref, kb_ref, vf_ref, ra_ref, ga_ref, qs_ref, ks_ref, vs_ref, gs_ref,
                 L=CHUNK, get_state=get_state, set_state=set_state, store_q=store_q, store_v=store_v,
                 interleave=True)


def _inproj_sample_kernel(*refs, L):
    ins, s0_ref, outs = refs[:13], refs[13], refs[14:]
    (qa_ref, kf_ref, kb_ref, vf_ref, vb_ref, ra_ref, ga_ref, st_ref,
     qs_ref, ks_ref, vs_ref, gs_ref) = outs

    def get_state(ci, g):
        return s0_ref[ci, g * HEAD_W:(g + 1) * HEAD_W, :]

    def set_state(ci, g, val):
        st_ref[ci, g * HEAD_W:(g + 1) * HEAD_W, :] = val

    def store_q(sl, val):
        qa_ref[:, sl] = val.astype(BF16)

    def store_v(sl, val):
        vb_ref[:, sl] = val.astype(BF16)

    _inproj_body(*ins, kf_ref, kb_ref, vf_ref, ra_ref, ga_ref, qs_ref, ks_ref, vs_ref, gs_ref,
                 L=L, get_state=get_state, set_state=set_state, store_q=store_q, store_v=store_v,
                 interleave=True)


def _ret_tables(T):
    log_g = jnp.log1p(-jnp.exp2(-5.0 - jnp.arange(HEADS, dtype=F32)))
    i = jnp.arange(T, dtype=F32)
    diff = i[:, None] - i[None, :]
    d_mat = jnp.where(diff >= 0, jnp.exp(log_g[:, None, None] * jnp.maximum(diff, 0.0)), 0.0)
    q_decay = jnp.exp(log_g[None, :] * (i[:, None] + 1.0))
    k_decay = jnp.exp(log_g[None, :] * (T - 1.0 - i[:, None]))
    chunk_decay = jnp.exp(log_g * T)
    dpair = d_mat.reshape(N_PAIR, 2 * T, T)
    qdec = jnp.broadcast_to(q_decay.T.reshape(N_PAIR, 2 * T, 1), (N_PAIR, 2 * T, HEAD_W))
    kdec = jnp.repeat(k_decay.reshape(T, N_PAIR, 2), R_DK, axis=2).transpose(1, 0, 2)
    cdec = jnp.broadcast_to(jnp.repeat(chunk_decay.reshape(N_PAIR, 2), R_DK, axis=1)[:, :, None],
                            (N_PAIR, 2 * R_DK, HEAD_W))
    return dpair, qdec, kdec, cdec


def _rope_tables(pos):
    half = R_DK // 2
    inv_freq = ROPE_BASE ** (-jnp.arange(half, dtype=F32) / half)
    ang = pos.astype(F32)[:, None] * inv_freq[None, :]
    cos, sin = jnp.cos(ang), jnp.sin(ang)
    return jnp.tile(cos, (1, 4)), jnp.tile(jnp.concatenate([-sin, sin], axis=1), (1, 2))


def _inproj(x2d, w_in, g_mix, gq, gk, cos_t, sin_t, gmat, tables, g_rn, *, tm, seq, state0):
    n = x2d.shape[0]
    prompt = state0 is None
    dpair, qdec, kdec, cdec = tables
    nt = seq // tm if prompt else 1
    nb = n // seq if prompt else n // tm
    per_tile = tm // seq if not prompt else 1

    const2 = lambda *_: (0, 0)
    const3 = lambda *_: (0, 0, 0)
    row = lambda b, i: (b * nt + i, 0)
    resident = functools.partial(pl.BlockSpec, pipeline_mode=pl.Buffered(1))
    in_specs = [
        pl.BlockSpec((tm, D_MODEL), row),
        resident((D_MODEL, IN_WIDTH), const2),
        resident((1, D_MODEL), const2),
        resident((1, PROJ_N), const2),
        resident((1, PROJ_N), const2),
        pl.BlockSpec((tm, HEAD_W), (lambda b, i: (i, 0)) if prompt else const2),
        pl.BlockSpec((tm, HEAD_W), (lambda b, i: (i, 0)) if prompt else const2),
        resident((MXU_N, MXU_N), const2),
        resident(dpair.shape, const3),
        resident(qdec.shape, const3),
        resident(kdec.shape, const3),
        resident(cdec.shape, const3),
        resident((HEADS, HEAD_W), const2),
    ]
    args = [x2d, w_in, g_mix, gq, gk, cos_t, sin_t, gmat, dpair, qdec, kdec, cdec, g_rn]
    st_rows = 2 * R_DK * N_PAIR
    if prompt:
        st_spec = pl.BlockSpec((1, st_rows, HEAD_W), lambda b, i: (b, 0, 0))
        st_shape = jax.ShapeDtypeStruct((nb, st_rows, HEAD_W), F32)
        kernel = _inproj_prompt_kernel
        L = CHUNK
    else:
        st_spec = pl.BlockSpec((per_tile, st_rows, HEAD_W), lambda b, i: (b, 0, 0))
        st_shape = jax.ShapeDtypeStruct(state0.shape, F32)
        in_specs.append(st_spec)
        args.append(state0)
        kernel = functools.partial(_inproj_sample_kernel, L=seq)
        L = seq
    tok = lambda dt: jax.ShapeDtypeStruct((n, D_MODEL), dt)
    tok_spec = pl.BlockSpec((tm, D_MODEL), row)
    cache = jax.ShapeDtypeStruct((n * HEADS, HEAD_W), F32)
    cache_spec = pl.BlockSpec((tm * HEADS, HEAD_W), row)
    out_shape = [tok(BF16), cache, tok(BF16), cache, tok(BF16), tok(BF16), tok(BF16), st_shape]
    out_specs = [tok_spec, cache_spec, tok_spec, cache_spec, tok_spec, tok_spec, tok_spec, st_spec]
    if prompt:
        out_shape[0] = jax.ShapeDtypeStruct((nb, nt, D_MODEL, tm), BF16)
        out_specs[0] = pl.BlockSpec((1, 1, D_MODEL, tm), lambda b, i: (b, i, 0, 0))
        out_shape[4] = jax.ShapeDtypeStruct((nb, seq // ATT_TK, D_MODEL, ATT_TK), BF16)
        out_specs[4] = pl.BlockSpec((1, tm // ATT_TK, D_MODEL, ATT_TK), lambda b, i: (b, i, 0, 0))
    scratch = [pltpu.VMEM((tm, HEADS * R_DK), BF16), pltpu.VMEM((tm, HEADS * R_DK), F32),
               pltpu.VMEM((tm, D_MODEL), BF16), pltpu.VMEM((tm, D_MODEL), F32)]
    return pl.pallas_call(
        kernel, out_shape=out_shape, grid=(nb, nt), in_specs=in_specs, out_specs=out_specs,
        scratch_shapes=scratch, name="inproj_prompt" if prompt else "inproj_sample",
        compiler_params=pltpu.CompilerParams(dimension_semantics=("arbitrary", "arbitrary"),
                                             vmem_limit_bytes=VMEM_LIMIT),
    )(*args)


def _lam(lamq_ref, lamk_ref, lam_init):
    e = jnp.exp(jnp.sum(lamq_ref[...] * lamk_ref[...], axis=-1, keepdims=True))
    return e[0:1, :] - e[1:2, :] + lam_init


def _split_maps(q):
    lane = lax.broadcasted_iota(jnp.int32, q.shape, 1)
    zero = jnp.zeros_like(q)
    return jnp.concatenate([jnp.where(lane < A_DIM, q, zero), jnp.where(lane < A_DIM, zero, q)], axis=0)


def _merge_out(o, gsub_ref, ga, ra, lam_init):
    on = (_rms_scale(o) * gsub_ref[...]) * (1.0 - lam_init)
    return (ga.astype(F32) * on + ra.astype(F32)).astype(BF16)


def _attn_prompt_body(steps_ref, qt_ref, k_ref, vt_ref, ga_ref, ra_ref, lamq_ref, lamk_ref, gsub_ref, o_ref,
                      acc_ref, qq_ref, m_ref, l_ref, d_ref, s0_ref, s1_ref, p0_ref, p1_ref,
                      *, tq, tk, lam_init, bounded):
    n_q = qt_ref.shape[1]
    n_steps = steps_ref.shape[0] - 2
    assert n_steps % ATT_UNROLL == 0 and tq % tk == 0
    s_refs, p_refs = (s0_ref, s1_ref), (p0_ref, p1_ref)

    def scores(s, slot):
        t, i = steps_ref[s, 0], steps_ref[s, 1]
        rows = pl.ds(pl.multiple_of(t * tk, tk), tk)
        s_refs[slot][...] = _dot(k_ref[0, rows, :], qq_ref[i])

    def softmax(s, slot, alpha):
        t, i = steps_ref[s, 0], steps_ref[s, 1]
        visible = d_ref[...] <= (tq // CHUNK) * i - (tk // CHUNK) * t
        st = jnp.where(visible, s_refs[slot][...], NEG_BIG)
        if bounded:
            p = jnp.exp2(st)
            l_ref[i] = l_ref[i] + jnp.sum(p, axis=0, keepdims=True)
            p_refs[slot][...] = p.astype(BF16)
            return alpha
        m_old = m_ref[i]
        m_new = jnp.maximum(m_old, jnp.max(st, axis=0, keepdims=True))
        m_ref[i] = m_new
        p = jnp.exp2(st - m_new)
        alpha_new = jnp.exp2(m_old - m_new)
        l_ref[i] = alpha_new * l_ref[i] + jnp.sum(p, axis=0, keepdims=True)
        p_refs[slot][...] = p.astype(BF16)
        return alpha_new

    def fold_values(s, slot, alpha):
        t, i = steps_ref[s, 0], steps_ref[s, 1]
        pv = _dot(vt_ref[0, t], p_refs[slot][...])
        acc_ref[i] = acc_ref[i] + pv if bounded else alpha * acc_ref[i] + pv

    def step(s, slot, alpha):
        fold_values(s - 1, 1 - slot, alpha)
        scores(s + 1, 1 - slot)
        return softmax(s, slot, alpha)

    feat = lax.broadcasted_iota(jnp.int32, (HEAD_W, tq), 0)
    for i in range(n_q):
        qt = qt_ref[0, i]
        zero = jnp.zeros_like(qt)
        qq_ref[i] = jnp.concatenate([jnp.where(feat < A_DIM, qt, zero), jnp.where(feat < A_DIM, zero, qt)],
                                    axis=1)
    key = lax.broadcasted_iota(jnp.int32, (tk, 2 * tq), 0)
    qry = lax.broadcasted_iota(jnp.int32, (tk, 2 * tq), 1) % tq
    d_ref[...] = key // CHUNK - qry // CHUNK
    acc_ref[...] = jnp.zeros_like(acc_ref)
    m_ref[...] = jnp.full_like(m_ref, NEG_BIG)
    l_ref[...] = jnp.zeros_like(l_ref)
    p0_ref[...] = jnp.zeros_like(p0_ref)
    scores(1, 1)

    def body(jj, alpha):
        for k in range(ATT_UNROLL):
            alpha = step(ATT_UNROLL * jj + 1 + k, (1 + k) & 1, alpha)
        return alpha

    alpha = lax.fori_loop(0, n_steps // ATT_UNROLL, body, jnp.ones((1, 2 * tq), F32))
    fold_values(n_steps, n_steps & 1, alpha)

    lam = _lam(lamq_ref, lamk_ref, lam_init)

    def finish(i, carry):
        acc = acc_ref[i]
        l = l_ref[i]
        ot = acc[:HEAD_W, :tq] / l[:, :tq] - lam * (acc[:HEAD_W, tq:] / l[:, tq:])
        q_rows = pl.ds(pl.multiple_of(i * tq, tq), tq)
        o_ref[0, q_rows, :] = _merge_out(ot.T, gsub_ref, ga_ref[0, q_rows, :], ra_ref[0, q_rows, :], lam_init)
        return carry

    lax.fori_loop(0, n_q, finish, 0, unroll=4)


def _attn_prompt_kernel(bound_ref, *refs, **static):
    bounded = bound_ref[0, 0] <= EXP2_SAFE_RANGE
    pl.when(bounded)(lambda: _attn_prompt_body(*refs, bounded=True, **static))
    pl.when(jnp.logical_not(bounded))(lambda: _attn_prompt_body(*refs, bounded=False, **static))


def _attn_steps(n_q, blocks_per_q):
    pairs = [(t, i) for t in range(n_q * blocks_per_q) for i in range(t // blocks_per_q, n_q)]
    return jnp.asarray([pairs[0]] + pairs + [pairs[-1]], jnp.int32)


def _attn_prompt(bound, qt, kb, vt, ga, ra, lam_q, lam_k, g_sub, *, tq, lam_init):
    B, S, _ = kb.shape
    nt, tk = vt.shape[1], vt.shape[3]
    n_q = S // tq
    assert qt.shape[1:] == (n_q, D_MODEL, tq)
    blk = pl.BlockSpec((1, S, HEAD_W), lambda b, h: (b, 0, h))
    small = lambda shape: pl.BlockSpec(shape, lambda b, h: (0, 0))
    smem = pl.BlockSpec(memory_space=pltpu.SMEM)
    return pl.pallas_call(
        functools.partial(_attn_prompt_kernel, tq=tq, tk=tk, lam_init=lam_init),
        out_shape=jax.ShapeDtypeStruct((B, S, D_MODEL), BF16),
        grid=(B, HEADS),
        in_specs=[smem, smem,
                  pl.BlockSpec((1, n_q, HEAD_W, tq), lambda b, h: (b, 0, h, 0)),
                  blk,
                  pl.BlockSpec((1, nt, HEAD_W, tk), lambda b, h: (b, 0, h, 0)),
                  blk, blk, small((2, A_DIM)), small((2, A_DIM)), small((1, HEAD_W))],
        out_specs=blk, name="attn_prompt",
        scratch_shapes=[pltpu.VMEM((n_q, HEAD_W, 2 * tq), F32),
                        pltpu.VMEM((n_q, HEAD_W, 2 * tq), BF16),
                        pltpu.VMEM((n_q, 1, 2 * tq), F32),
                        pltpu.VMEM((n_q, 1, 2 * tq), F32),
                        pltpu.VMEM((tk, 2 * tq), jnp.int32),
                        pltpu.VMEM((tk, 2 * tq), F32), pltpu.VMEM((tk, 2 * tq), F32),
                        pltpu.VMEM((tk, 2 * tq), BF16), pltpu.VMEM((tk, 2 * tq), BF16)],
        compiler_params=pltpu.CompilerParams(dimension_semantics=("arbitrary",) * 2,
                                             vmem_limit_bytes=VMEM_LIMIT),
    )(bound, _attn_steps(n_q, tq // tk), qt, kb, vt, ga, ra, lam_q, lam_k, g_sub)


def _attn_sample_kernel(q_ref, kc_ref, vc_ref, kn_ref, vn_ref, ga_ref, ra_ref, lamq_ref, lamk_ref,
                        gsub_ref, o_ref, *, lam_init):
    t = q_ref.shape[1]
    half = HEADS // 2
    n_mixed = kc_ref.shape[1] // half
    lam = _lam(lamq_ref, lamk_ref, lam_init)
    row = lax.broadcasted_iota(jnp.int32, (4 * t, n_mixed), 0)
    col = lax.broadcasted_iota(jnp.int32, (4 * t, n_mixed), 1)
    own_head = (col & 1) == (row >= 2 * t).astype(jnp.int32)
    for h in range(half):
        heads = (h, h + half)
        sls = [slice(hh * HEAD_W, (hh + 1) * HEAD_W) for hh in heads]
        pair_rows = pl.ds(h, n_mixed, stride=half)
        qqs = [_split_maps(q_ref[0, :, sl]) for sl in sls]
        s_c = _dot_nt(jnp.concatenate(qqs, axis=0), kc_ref[0, pair_rows, :].astype(BF16))
        s_c = jnp.where(own_head, s_c, NEG_BIG)
        s_n = jnp.concatenate([_dot_nt(qq, kn_ref[0, :, sl]) for qq, sl in zip(qqs, sls)], axis=0)
        m = jnp.maximum(jnp.max(s_c, axis=-1, keepdims=True), jnp.max(s_n, axis=-1, keepdims=True))
        p_c = jnp.exp2(s_c - m)
        p_n = jnp.exp2(s_n - m)
        l = jnp.sum(p_c, axis=-1, keepdims=True) + jnp.sum(p_n, axis=-1, keepdims=True)
        acc = _dot(p_c.astype(BF16), vc_ref[0, pair_rows, :].astype(BF16))
        for j, sl in enumerate(sls):
            r0 = 2 * t * j
            a = acc[r0:r0 + 2 * t] + _dot(p_n[r0:r0 + 2 * t].astype(BF16), vn_ref[0, :, sl])
            lj = l[r0:r0 + 2 * t]
            o = a[:t] / lj[:t] - lam * (a[t:] / lj[t:])
            o_ref[0, :, sl] = _merge_out(o, gsub_ref, ga_ref[0, :, sl], ra_ref[0, :, sl], lam_init)


def _attn_sample(qa, kc, vc, kn, vn, ga, ra, lam_q, lam_k, g_sub, *, lam_init):
    DB, T, _ = qa.shape
    blk = pl.BlockSpec((1, T, D_MODEL), lambda b: (b, 0, 0))
    cache = pl.BlockSpec((1,) + kc.shape[1:], lambda b: (b, 0, 0))
    small = lambda shape: pl.BlockSpec(shape, lambda b: (0, 0))
    return pl.pallas_call(
        functools.partial(_attn_sample_kernel, lam_init=lam_init),
        out_shape=jax.ShapeDtypeStruct((DB, T, D_MODEL), BF16),
        grid=(DB,),
        in_specs=[blk, cache, cache, blk, blk, blk, blk, small((2, A_DIM)), small((2, A_DIM)),
                  small((1, HEAD_W))],
        out_specs=blk, name="attn_sample",
        compiler_params=pltpu.CompilerParams(dimension_semantics=("arbitrary",),
                                             vmem_limit_bytes=VMEM_LIMIT),
    )(qa, kc, vc, kn, vn, ga, ra, lam_q, lam_k, g_sub)


def _tail_kernel(x_ref, mix_ref, p_ref, wo_ref, gffn_ref, wg_ref, wu_ref, wd_ref, gple_ref, wple_ref,
                 wpg_ref, y_ref, act_ref):
    x1 = x_ref[...] + _dot(mix_ref[...], wo_ref[...])
    h = (_rms_scale(x1) * gffn_ref[...]).astype(BF16)
    for c in range(D_FF // MXU_N):
        sl = slice(c * MXU_N, (c + 1) * MXU_N)
        g = _dot(h, wg_ref[:, sl])
        u = _dot(h, wu_ref[:, sl])
        act_ref[:, sl] = ((g * jax.nn.sigmoid(g)) * u).astype(BF16)
    x2 = x1 + _dot(act_ref[...], wd_ref[...])
    h3 = (_rms_scale(x2) * gple_ref[...]).astype(BF16)
    gate = jax.nn.sigmoid(_dot(h3, wpg_ref[...]))
    y_ref[...] = x2 + gate * _dot(p_ref[...].astype(BF16), wple_ref[...])


def _tail(x2d, mix2d, p2d, w_o, g_ffn, w_g, w_u, w_d, g_ple, w_ple, w_pg, *, tm, name):
    n = x2d.shape[0]
    row = lambda i: (i, 0)
    const = lambda i: (0, 0)
    resident = functools.partial(pl.BlockSpec, pipeline_mode=pl.Buffered(1))
    return pl.pallas_call(
        _tail_kernel, out_shape=jax.ShapeDtypeStruct((n, D_MODEL), F32), grid=(n // tm,),
        in_specs=[pl.BlockSpec((tm, D_MODEL), row), pl.BlockSpec((tm, D_MODEL), row),
                  pl.BlockSpec((tm, PLE_DIM), row),
                  resident((D_MODEL, D_MODEL), const), resident((1, D_MODEL), const),
                  resident((D_MODEL, D_FF), const), resident((D_MODEL, D_FF), const),
                  resident((D_FF, D_MODEL), const), resident((1, D_MODEL), const),
                  resident((PLE_DIM, D_MODEL), const), resident((D_MODEL, D_MODEL), const)],
        out_specs=pl.BlockSpec((tm, D_MODEL), row),
        scratch_shapes=[pltpu.VMEM((tm, D_FF), BF16)], name=name,
        compiler_params=pltpu.CompilerParams(dimension_semantics=("arbitrary",),
                                             vmem_limit_bytes=VMEM_LIMIT),
    )(x2d, mix2d, p2d, w_o, g_ffn, w_g, w_u, w_d, g_ple, w_ple, w_pg)


def kernel(x_prompt, x_sample, cache_attn_k, cache_attn_v, state_ret, p_prompt, p_sample, w_in, g_mix_norm, g_q_norm, g_k_norm, lam_q, lam_k, g_sub_norm, g_ret_norm, w_o, g_ffn_norm, w_ff_gate, w_ff_up, w_ff_down, g_ple_norm, w_ple, w_ple_gate):
    B, S, D = x_prompt.shape
    DB, T, _ = x_sample.shape
    P = cache_attn_k.shape[2]
    depth = w_in.shape[0]
    assert depth == 1 and D == D_MODEL and S % 512 == 0 and 256 % T == 0
    l = 0
    lam_init = 0.8 - 0.6 * math.exp(-0.3 * l)
    tm = 256

    w_in_b = w_in[l].astype(BF16)
    tail_w = (w_o[l].astype(BF16), g_ffn_norm[l][None, :], w_ff_gate[l].astype(BF16),
              w_ff_up[l].astype(BF16), w_ff_down[l].astype(BF16), g_ple_norm[l][None, :],
              w_ple[l].astype(BF16), w_ple_gate[l].astype(BF16))
    g_mix = g_mix_norm[l][None, :]
    gq = jnp.tile(g_q_norm[l], PROJ_N // A_DIM)[None, :]
    gk = jnp.tile(g_k_norm[l], PROJ_N // A_DIM)[None, :]
    g_sub = g_sub_norm[l][None, :]
    g_rn = g_ret_norm[l]
    grp = jnp.arange(MXU_N) // A_DIM
    gmat = (grp[:, None] == grp[None, :]).astype(BF16)

    cos_p, sin_p = _rope_tables(jnp.arange(S))
    xp2 = x_prompt.reshape(B * S, D)
    qt, kf, kb, vf, vt, ra, ga, st_p = _inproj(
        xp2, w_in_b, g_mix, gq, gk, cos_p, sin_p, gmat, _ret_tables(CHUNK), g_rn,
        tm=512, seq=S, state0=None)
    r3 = lambda a: a.reshape(B, S, D)
    score_bound = ((A_DIM * Q_SCALE * 1.01) * jnp.max(jnp.abs(g_q_norm[l]))
                   * jnp.max(jnp.abs(g_k_norm[l]))).reshape(1, 1)
    mix_p = _attn_prompt(score_bound, qt, r3(kb), vt, r3(ga), r3(ra), lam_q[l], lam_k[l], g_sub,
                         tq=512, lam_init=lam_init)
    y_p = _tail(xp2, mix_p.reshape(B * S, D), p_prompt[l].reshape(B * S, PLE_DIM), *tail_w,
                tm=tm, name="tail_prompt")

    cos_s, sin_s = _rope_tables(P + jnp.arange(T))
    reps = tm // T
    xs2 = x_sample.reshape(DB * T, D)
    st0 = state_ret[l].reshape(DB, HEADS * R_DK, HEAD_W)
    qa_s, kf_s, kb_s, vf_s, vb_s, ra_s, ga_s, st_s = _inproj(
        xs2, w_in_b, g_mix, gq, gk, jnp.tile(cos_s, (reps, 1)), jnp.tile(sin_s, (reps, 1)), gmat,
        _ret_tables(T), g_rn, tm=tm, seq=T, state0=st0)
    s3 = lambda a: a.reshape(DB, T, D)
    mix_s = _attn_sample(s3(qa_s), cache_attn_k[l].reshape(DB, P * HEADS, HEAD_W),
                         cache_attn_v[l].reshape(DB, P * HEADS, HEAD_W),
                         s3(kb_s), s3(vb_s), s3(ga_s), s3(ra_s), lam_q[l], lam_k[l], g_sub,
                         lam_init=lam_init)
    y_s = _tail(xs2, mix_s.reshape(DB * T, D), p_sample[l].reshape(DB * T, PLE_DIM), *tail_w,
                tm=tm, name="tail_sample")

    return (y_p.reshape(B, S, D), y_s.reshape(DB, T, D),
            kf.reshape(1, B, S, HEADS, HEAD_W), vf.reshape(1, B, S, HEADS, HEAD_W),
            st_p.reshape(1, B, HEADS, R_DK, HEAD_W),
            kf_s.reshape(1, DB, T, HEADS, HEAD_W), vf_s.reshape(1, DB, T, HEADS, HEAD_W),
            st_s.reshape(1, DB, HEADS, R_DK, HEAD_W))
```

```python
import functools
import math

import jax
import jax.numpy as jnp
from jax import lax
from jax.experimental import pallas as pl
from jax.experimental.pallas import tpu as pltpu

F32 = jnp.float32
BF16 = jnp.bfloat16

D_MODEL = 1024
CHUNK = 64
PLE_DIM = 256
EPS = 1e-6
HEADS = 8
HEAD_W = 128
A_DIM = 64
R_DK = 64
ROPE_BASE = 10000.0
D_FF = 2816
N_PAIR = HEADS // 2

OFF_QA, OFF_KA, OFF_VA = 0, 1024, 2048
OFF_QR, OFF_KR, OFF_VR = 3072, 3584, 4096
OFF_GRET, OFF_GA, OFF_GR = 5120, 6144, 7168
IN_WIDTH = 8192

MXU_N = 256
PROJ_N = 2 * MXU_N
VMEM_LIMIT = 56 * 1024 * 1024
NEG_BIG = -1e30
ATT_TK = 256
ATT_UNROLL = 8
EXP2_SAFE_RANGE = 60.0
Q_SCALE = (A_DIM ** -0.5) * math.log2(math.e)

_NT = (((1,), (1,)), ((), ()))


def _dot(a, b):
    return jnp.dot(a, b, preferred_element_type=F32)


def _dot_nt(a, b):
    return lax.dot_general(a, b, _NT, preferred_element_type=F32)


def _rms_scale(x):
    return x * lax.rsqrt(jnp.mean(x * x, axis=-1, keepdims=True) + EPS)


def _inproj_body(x_ref, w_ref, gmix_ref, gq_ref, gk_ref, cos_ref, sin_ref, gmat_ref,
                 dpair_ref, qdec_ref, kdec_ref, cdec_ref, grn_ref,
                 kf_ref, kb_ref, vf_ref, ra_ref, ga_ref,
                 qs_ref, ks_ref, vs_ref, gs_ref, *, L, get_state, set_state, store_q, store_v,
                 interleave):
    tm = x_ref.shape[0]
    x = x_ref[...]
    h = (_rms_scale(x) * gmix_ref[...]).astype(BF16)
    gmat = gmat_ref[...]

    def proj(off, c):
        return _dot(h, w_ref[:, off + c * PROJ_N: off + (c + 1) * PROJ_N])

    def group_norm(z, g_ref):
        zz = (z * z).astype(BF16)
        ss = jnp.concatenate([_dot(zz[:, j * MXU_N:(j + 1) * MXU_N], gmat)
                              for j in range(PROJ_N // MXU_N)], axis=1)
        return (z * lax.rsqrt(ss * (1.0 / A_DIM) + EPS)) * g_ref[...]

    for c in range(D_MODEL // PROJ_N):
        sl = slice(c * PROJ_N, (c + 1) * PROJ_N)
        vs_ref[:, sl] = proj(OFF_VR, c).astype(BF16)
        g_ret = proj(OFF_GRET, c)
        gate_r = proj(OFF_GR, c)
        gs_ref[:, sl] = (g_ret * jax.nn.sigmoid(g_ret)) * jax.nn.sigmoid(gate_r)

    lane = lax.broadcasted_iota(jnp.int32, (tm, HEAD_W), 1)
    first_half = (lane % R_DK) < (R_DK // 2)
    cos = cos_ref[...]
    sin = sin_ref[...]

    def rotary(z):
        partner = jnp.where(first_half, pltpu.roll(z, HEAD_W - R_DK // 2, axis=1),
                            pltpu.roll(z, R_DK // 2, axis=1))
        return z * cos + partner * sin

    assert HEADS * R_DK == PROJ_N
    zq = proj(OFF_QR, 0)
    zk = proj(OFF_KR, 0)
    for g in range(N_PAIR):
        hs = slice(g * HEAD_W, (g + 1) * HEAD_W)
        qs_ref[:, hs] = rotary(zq[:, hs]).astype(BF16)
        ks_ref[:, hs] = rotary(zk[:, hs]) * (R_DK ** -0.5)

    def block_diag_cols(x):
        top = lax.broadcasted_iota(jnp.int32, x.shape, 0) < R_DK
        zero = jnp.zeros_like(x)
        return jnp.concatenate([jnp.where(top, x, zero), jnp.where(top, zero, x)], axis=1)

    def block_diag_rows(x):
        left = lax.broadcasted_iota(jnp.int32, x.shape, 1) < HEAD_W
        zero = jnp.zeros_like(x)
        return jnp.concatenate([jnp.where(left, x, zero), jnp.where(left, zero, x)], axis=0)

    def chunk_body(ci, carry):
        r0 = ci * L if isinstance(ci, int) else pl.multiple_of(ci * L, L)
        rows = pl.ds(r0, L)
        for g in range(N_PAIR):
            q2 = qs_ref[rows, g * HEAD_W:(g + 1) * HEAD_W]
            kt = ks_ref[rows, g * HEAD_W:(g + 1) * HEAD_W].T
            v2 = vs_ref[rows, 2 * g * HEAD_W:(2 * g + 2) * HEAD_W]
            a = _dot(q2, block_diag_cols(kt.astype(BF16)))
            a = (a * dpair_ref[g]).astype(BF16)
            intra = _dot(a, block_diag_rows(v2))
            state = get_state(ci, g)
            cross = _dot(q2, block_diag_cols(state.astype(BF16))) * qdec_ref[g]
            kv = _dot((kt * kdec_ref[g]).astype(BF16), v2)
            kv_pair = jnp.concatenate([kv[:R_DK, :HEAD_W], kv[R_DK:, HEAD_W:]], axis=0)
            set_state(ci, g, cdec_ref[g] * state + kv_pair)
            o2 = intra + cross
            for par in range(2):
                hh = 2 * g + par
                cols = slice(hh * HEAD_W, (hh + 1) * HEAD_W)
                on = _rms_scale(o2[:, par * HEAD_W:(par + 1) * HEAD_W]) * grn_ref[hh:hh + 1, :]
                ra_ref[rows, cols] = (on * gs_ref[rows, cols]).astype(BF16)
        return carry

    def q_unit(sl, c):
        store_q(sl, group_norm(proj(OFF_QA, c), gq_ref) * Q_SCALE)

    def store_cache_layout(ref, c, val):
        for j in range(PROJ_N // HEAD_W):
            head = c * (PROJ_N // HEAD_W) + j
            ref[pl.ds(head, tm, stride=HEADS), :] = val[:, j * HEAD_W:(j + 1) * HEAD_W]

    def k_unit(sl, c):
        kn = group_norm(proj(OFF_KA, c), gk_ref)
        store_cache_layout(kf_ref, c, kn)
        kb_ref[:, sl] = kn.astype(BF16)

    def v_unit(sl, c):
        v = proj(OFF_VA, c)
        store_cache_layout(vf_ref, c, v)
        store_v(sl, v)

    def gate_unit(sl, c):
        ga_ref[:, sl] = jax.nn.sigmoid(proj(OFF_GA, c)).astype(BF16)

    units = [functools.partial(u, slice(c * PROJ_N, (c + 1) * PROJ_N), c)
             for c in range(D_MODEL // PROJ_N) for u in (q_unit, k_unit, v_unit, gate_unit)]
    n_chunks = tm // L
    if interleave:
        per_chunk = -(-len(units) // n_chunks)
        for ci in range(n_chunks):
            chunk_body(ci, 0)
            for u in units[ci * per_chunk:(ci + 1) * per_chunk]:
                u()
    else:
        lax.fori_loop(0, n_chunks, chunk_body, 0)
        for u in units:
            u()


def _inproj_prompt_kernel(*refs):
    ins, outs = refs[:13], refs[13:]
    (qt_ref, kf_ref, kb_ref, vf_ref, vt_ref, ra_ref, ga_ref, st_ref,
     qs_ref, ks_ref, vs_ref, gs_ref) = outs

    @pl.when(pl.program_id(1) == 0)
    def _():
        st_ref[...] = jnp.zeros_like(st_ref)

    def get_state(ci, g):
        return st_ref[0, g * HEAD_W:(g + 1) * HEAD_W, :]

    def set_state(ci, g, val):
        st_ref[0, g * HEAD_W:(g + 1) * HEAD_W, :] = val

    def store_q(sl, val):
        qt_ref[0, 0, sl, :] = val.T.astype(BF16)

    def store_v(sl, val):
        for kb_i in range(val.shape[0] // ATT_TK):
            vt_ref[0, kb_i, sl, :] = val[kb_i * ATT_TK:(kb_i + 1) * ATT_TK].T.astype(BF16)

    _inproj_body(*ins, kf_ref, kb_ref, vf_ref, ra_ref, ga_ref, qs_ref, ks_ref, vs_ref, gs_ref,
                 L=CHUNK, get_state=get_state, set_state=set_state, store_q=store_q, store_v=store_v,
                 interleave=True)


def _inproj_sample_kernel(*refs, L):
    ins, s0_ref, outs = refs[:13], refs[13], refs[14:]
    (qa_ref, kf_ref, kb_ref, vf_ref, vb_ref, ra_ref, ga_ref, st_ref,
     qs_ref, ks_ref, vs_ref, gs_ref) = outs

    def get_state(ci, g):
        return s0_ref[ci, g * HEAD_W:(g + 1) * HEAD_W, :]

    def set_state(ci, g, val):
        st_ref[ci, g * HEAD_W:(g + 1) * HEAD_W, :] = val

    def store_q(sl, val):
        qa_ref[:, sl] = val.astype(BF16)

    def store_v(sl, val):
        vb_ref[:, sl] = val.astype(BF16)

    _inproj_body(*ins, kf_ref, kb_ref, vf_ref, ra_ref, ga_ref, qs_ref, ks_ref, vs_ref, gs_ref,
                 L=L, get_state=get_state, set_state=set_state, store_q=store_q, store_v=store_v,
                 interleave=True)


def _ret_tables(T):
    log_g = jnp.log1p(-jnp.exp2(-5.0 - jnp.arange(HEADS, dtype=F32)))
    i = jnp.arange(T, dtype=F32)
    diff = i[:, None] - i[None, :]
    d_mat = jnp.where(diff >= 0, jnp.exp(log_g[:, None, None] * jnp.maximum(diff, 0.0)), 0.0)
    q_decay = jnp.exp(log_g[None, :] * (i[:, None] + 1.0))
    k_decay = jnp.exp(log_g[None, :] * (T - 1.0 - i[:, None]))
    chunk_decay = jnp.exp(log_g * T)
    dpair = d_mat.reshape(N_PAIR, 2, T, T).transpose(0, 2, 1, 3).reshape(N_PAIR, T, 2 * T)
    qdec = jnp.repeat(q_decay.reshape(T, N_PAIR, 2), HEAD_W, axis=2).transpose(1, 0, 2)
    kdec = jnp.repeat(k_decay.T.reshape(N_PAIR, 2, T), R_DK, axis=1)
    cdec = jnp.broadcast_to(jnp.repeat(chunk_decay.reshape(N_PAIR, 2), R_DK, axis=1)[:, :, None],
                            (N_PAIR, 2 * R_DK, HEAD_W))
    return dpair, qdec, kdec, cdec


def _rope_tables(pos):
    half = R_DK // 2
    inv_freq = ROPE_BASE ** (-jnp.arange(half, dtype=F32) / half)
    ang = pos.astype(F32)[:, None] * inv_freq[None, :]
    cos, sin = jnp.cos(ang), jnp.sin(ang)
    return jnp.tile(cos, (1, 4)), jnp.tile(jnp.concatenate([-sin, sin], axis=1), (1, 2))


def _inproj(x2d, w_in, g_mix, gq, gk, cos_t, sin_t, gmat, tables, g_rn, *, tm, seq, state0):
    n = x2d.shape[0]
    prompt = state0 is None
    dpair, qdec, kdec, cdec = tables
    nt = seq // tm if prompt else 1
    nb = n // seq if prompt else n // tm
    per_tile = tm // seq if not prompt else 1

    const2 = lambda *_: (0, 0)
    const3 = lambda *_: (0, 0, 0)
    row = lambda b, i: (b * nt + i, 0)
    resident = functools.partial(pl.BlockSpec, pipeline_mode=pl.Buffered(1))
    in_specs = [
        pl.BlockSpec((tm, D_MODEL), row),
        resident((D_MODEL, IN_WIDTH), const2),
        resident((1, D_MODEL), const2),
        resident((1, PROJ_N), const2),
        resident((1, PROJ_N), const2),
        pl.BlockSpec((tm, HEAD_W), (lambda b, i: (i, 0)) if prompt else const2),
        pl.BlockSpec((tm, HEAD_W), (lambda b, i: (i, 0)) if prompt else const2),
        resident((MXU_N, MXU_N), const2),
        resident(dpair.shape, const3),
        resident(qdec.shape, const3),
        resident(kdec.shape, const3),
        resident(cdec.shape, const3),
        resident((HEADS, HEAD_W), const2),
    ]
    args = [x2d, w_in, g_mix, gq, gk, cos_t, sin_t, gmat, dpair, qdec, kdec, cdec, g_rn]
    st_rows = 2 * R_DK * N_PAIR
    if prompt:
        st_spec = pl.BlockSpec((1, st_rows, HEAD_W), lambda b, i: (b, 0, 0))
        st_shape = jax.ShapeDtypeStruct((nb, st_rows, HEAD_W), F32)
        kernel = _inproj_prompt_kernel
        L = CHUNK
    else:
        st_spec = pl.BlockSpec((per_tile, st_rows, HEAD_W), lambda b, i: (b, 0, 0))
        st_shape = jax.ShapeDtypeStruct(state0.shape, F32)
        in_specs.append(st_spec)
        args.append(state0)
        kernel = functools.partial(_inproj_sample_kernel, L=seq)
        L = seq
    tok = lambda dt: jax.ShapeDtypeStruct((n, D_MODEL), dt)
    tok_spec = pl.BlockSpec((tm, D_MODEL), row)
    cache = jax.ShapeDtypeStruct((n * HEADS, HEAD_W), F32)
    cache_spec = pl.BlockSpec((tm * HEADS, HEAD_W), row)
    out_shape = [tok(BF16), cache, tok(BF16), cache, tok(BF16), tok(BF16), tok(BF16), st_shape]
    out_specs = [tok_spec, cache_spec, tok_spec, cache_spec, tok_spec, tok_spec, tok_spec, st_spec]
    if prompt:
        out_shape[0] = jax.ShapeDtypeStruct((nb, nt, D_MODEL, tm), BF16)
        out_specs[0] = pl.BlockSpec((1, 1, D_MODEL, tm), lambda b, i: (b, i, 0, 0))
        out_shape[4] = jax.ShapeDtypeStruct((nb, seq // ATT_TK, D_MODEL, ATT_TK), BF16)
        out_specs[4] = pl.BlockSpec((1, tm // ATT_TK, D_MODEL, ATT_TK), lambda b, i: (b, i, 0, 0))
    scratch = [pltpu.VMEM((tm, HEADS * R_DK), BF16), pltpu.VMEM((tm, HEADS * R_DK), F32),
               pltpu.VMEM((tm, D_MODEL), BF16), pltpu.VMEM((tm, D_MODEL), F32)]
    return pl.pallas_call(
        kernel, out_shape=out_shape, grid=(nb, nt), in_specs=in_specs, out_specs=out_specs,
        scratch_shapes=scratch, name="inproj_prompt" if prompt else "inproj_sample",
        compiler_params=pltpu.CompilerParams(dimension_semantics=("arbitrary", "arbitrary"),
                                             vmem_limit_bytes=VMEM_LIMIT),
    )(*args)


def _lam(lamq_ref, lamk_ref, lam_init):
    e = jnp.exp(jnp.sum(lamq_ref[...] * lamk_ref[...], axis=-1, keepdims=True))
    return e[0:1, :] - e[1:2, :] + lam_init


def _split_maps(q):
    lane = lax.broadcasted_iota(jnp.int32, q.shape, 1)
    zero = jnp.zeros_like(q)
    return jnp.concatenate([jnp.where(lane < A_DIM, q, zero), jnp.where(lane < A_DIM, zero, q)], axis=0)


def _merge_out(o, gsub_ref, ga, ra, lam_init):
    on = (_rms_scale(o) * gsub_ref[...]) * (1.0 - lam_init)
    return (ga.astype(F32) * on + ra.astype(F32)).astype(BF16)


def _attn_prompt_body(steps_ref, qt_ref, k_ref, vt_ref, ga_ref, ra_ref, lamq_ref, lamk_ref, gsub_ref, o_ref,
                      acc_ref, qq_ref, m_ref, l_ref, d_ref, s0_ref, s1_ref, p0_ref, p1_ref,
                      *, tq, tk, lam_init, bounded):
    n_q = qt_ref.shape[1]
    n_steps = steps_ref.shape[0] - 2
    assert n_steps % ATT_UNROLL == 0 and tq % tk == 0
    s_refs, p_refs = (s0_ref, s1_ref), (p0_ref, p1_ref)

    def scores(s, slot):
        t, i = steps_ref[s, 0], steps_ref[s, 1]
        rows = pl.ds(pl.multiple_of(t * tk, tk), tk)
        s_refs[slot][...] = _dot(k_ref[0, rows, :], qq_ref[i])

    def softmax(s, slot, alpha):
        t, i = steps_ref[s, 0], steps_ref[s, 1]
        visible = d_ref[...] <= (tq // CHUNK) * i - (tk // CHUNK) * t
        st = jnp.where(visible, s_refs[slot][...], NEG_BIG)
        if bounded:
            p = jnp.exp2(st)
            l_ref[i] = l_ref[i] + jnp.sum(p, axis=0, keepdims=True)
            p_refs[slot][...] = p.astype(BF16)
            return alpha
        m_old = m_ref[i]
        m_new = jnp.maximum(m_old, jnp.max(st, axis=0, keepdims=True))
        m_ref[i] = m_new
        p = jnp.exp2(st - m_new)
        alpha_new = jnp.exp2(m_old - m_new)
        l_ref[i] = alpha_new * l_ref[i] + jnp.sum(p, axis=0, keepdims=True)
        p_refs[slot][...] = p.astype(BF16)
        return alpha_new

    def fold_values(s, slot, alpha):
        t, i = steps_ref[s, 0], steps_ref[s, 1]
        pv = _dot(vt_ref[0, t], p_refs[slot][...])
        acc_ref[i] = acc_ref[i] + pv if bounded else alpha * acc_ref[i] + pv

    def step(s, slot, alpha):
        fold_values(s - 1, 1 - slot, alpha)
        scores(s + 1, 1 - slot)
        return softmax(s, slot, alpha)

    feat = lax.broadcasted_iota(jnp.int32, (HEAD_W, tq), 0)
    for i in range(n_q):
        qt = qt_ref[0, i]
        zero = jnp.zeros_like(qt)
        qq_ref[i] = jnp.concatenate([jnp.where(feat < A_DIM, qt, zero), jnp.where(feat < A_DIM, zero, qt)],
                                    axis=1)
    key = lax.broadcasted_iota(jnp.int32, (tk, 2 * tq), 0)
    qry = lax.broadcasted_iota(jnp.int32, (tk, 2 * tq), 1) % tq
    d_ref[...] = key // CHUNK - qry // CHUNK
    acc_ref[...] = jnp.zeros_like(acc_ref)
    m_ref[...] = jnp.full_like(m_ref, NEG_BIG)
    l_ref[...] = jnp.zeros_like(l_ref)
    p0_ref[...] = jnp.zeros_like(p0_ref)
    scores(1, 1)

    def body(jj, alpha):
        for k in range(ATT_UNROLL):
            alpha = step(ATT_UNROLL * jj + 1 + k, (1 + k) & 1, alpha)
        return alpha

    alpha = lax.fori_loop(0, n_steps // ATT_UNROLL, body, jnp.ones((1, 2 * tq), F32))
    fold_values(n_steps, n_steps & 1, alpha)

    lam = _lam(lamq_ref, lamk_ref, lam_init)

    def finish(i, carry):
        acc = acc_ref[i]
        l = l_ref[i]
        ot = acc[:HEAD_W, :tq] / l[:, :tq] - lam * (acc[:HEAD_W, tq:] / l[:, tq:])
        q_rows = pl.ds(pl.multiple_of(i * tq, tq), tq)
        o_ref[0, q_rows, :] = _merge_out(ot.T, gsub_ref, ga_ref[0, q_rows, :], ra_ref[0, q_rows, :], lam_init)
        return carry

    lax.fori_loop(0, n_q, finish, 0, unroll=4)


def _attn_prompt_kernel(bound_ref, *refs, **static):
    bounded = bound_ref[0, 0] <= EXP2_SAFE_RANGE
    pl.when(bounded)(lambda: _attn_prompt_body(*refs, bounded=True, **static))
    pl.when(jnp.logical_not(bounded))(lambda: _attn_prompt_body(*refs, bounded=False, **static))


def _attn_steps(n_q, blocks_per_q):
    pairs = [(t, i) for t in range(n_q * blocks_per_q) for i in range(t // blocks_per_q, n_q)]
    return jnp.asarray([pairs[0]] + pairs + [pairs[-1]], jnp.int32)


def _attn_prompt(bound, qt, kb, vt, ga, ra, lam_q, lam_k, g_sub, *, tq, lam_init):
    B, S, _ = kb.shape
    nt, tk = vt.shape[1], vt.shape[3]
    n_q = S // tq
    assert qt.shape[1:] == (n_q, D_MODEL, tq)
    blk = pl.BlockSpec((1, S, HEAD_W), lambda b, h: (b, 0, h))
    small = lambda shape: pl.BlockSpec(shape, lambda b, h: (0, 0))
    smem = pl.BlockSpec(memory_space=pltpu.SMEM)
    return pl.pallas_call(
        functools.partial(_attn_prompt_kernel, tq=tq, tk=tk, lam_init=lam_init),
        out_shape=jax.ShapeDtypeStruct((B, S, D_MODEL), BF16),
        grid=(B, HEADS),
        in_specs=[smem, smem,
                  pl.BlockSpec((1, n_q, HEAD_W, tq), lambda b, h: (b, 0, h, 0)),
                  blk,
                  pl.BlockSpec((1, nt, HEAD_W, tk), lambda b, h: (b, 0, h, 0)),
                  blk, blk, small((2, A_DIM)), small((2, A_DIM)), small((1, HEAD_W))],
        out_specs=blk, name="attn_prompt",
        scratch_shapes=[pltpu.VMEM((n_q, HEAD_W, 2 * tq), F32),
                        pltpu.VMEM((n_q, HEAD_W, 2 * tq), BF16),
                        pltpu.VMEM((n_q, 1, 2 * tq), F32),
                        pltpu.VMEM((n_q, 1, 2 * tq), F32),
                        pltpu.VMEM((tk, 2 * tq), jnp.int32),
                        pltpu.VMEM((tk, 2 * tq), F32), pltpu.VMEM((tk, 2 * tq), F32),
                        pltpu.VMEM((tk, 2 * tq), BF16), pltpu.VMEM((tk, 2 * tq), BF16)],
        compiler_params=pltpu.CompilerParams(dimension_semantics=("arbitrary",) * 2,
                                             vmem_limit_bytes=VMEM_LIMIT),
    )(bound, _attn_steps(n_q, tq // tk), qt, kb, vt, ga, ra, lam_q, lam_k, g_sub)


def _attn_sample_kernel(q_ref, kc_ref, vc_ref, kn_ref, vn_ref, ga_ref, ra_ref, lamq_ref, lamk_ref,
                        gsub_ref, o_ref, *, lam_init):
    t = q_ref.shape[1]
    half = HEADS // 2
    n_mixed = kc_ref.shape[1] // half
    lam = _lam(lamq_ref, lamk_ref, lam_init)
    row = lax.broadcasted_iota(jnp.int32, (4 * t, n_mixed), 0)
    col = lax.broadcasted_iota(jnp.int32, (4 * t, n_mixed), 1)
    own_head = (col & 1) == (row >= 2 * t).astype(jnp.int32)
    for h in range(half):
        heads = (h, h + half)
        sls = [slice(hh * HEAD_W, (hh + 1) * HEAD_W) for hh in heads]
        pair_rows = pl.ds(h, n_mixed, stride=half)
        qqs = [_split_maps(q_ref[0, :, sl]) for sl in sls]
        s_c = _dot_nt(jnp.concatenate(qqs, axis=0), kc_ref[0, pair_rows, :].astype(BF16))
        s_c = jnp.where(own_head, s_c, NEG_BIG)
        s_n = jnp.concatenate([_dot_nt(qq, kn_ref[0, :, sl]) for qq, sl in zip(qqs, sls)], axis=0)
        m = jnp.maximum(jnp.max(s_c, axis=-1, keepdims=True), jnp.max(s_n, axis=-1, keepdims=True))
        p_c = jnp.exp2(s_c - m)
        p_n = jnp.exp2(s_n - m)
        l = jnp.sum(p_c, axis=-1, keepdims=True) + jnp.sum(p_n, axis=-1, keepdims=True)
        acc = _dot(p_c.astype(BF16), vc_ref[0, pair_rows, :].astype(BF16))
        for j, sl in enumerate(sls):
            r0 = 2 * t * j
            a = acc[r0:r0 + 2 * t] + _dot(p_n[r0:r0 + 2 * t].astype(BF16), vn_ref[0, :, sl])
            lj = l[r0:r0 + 2 * t]
            o = a[:t] / lj[:t] - lam * (a[t:] / lj[t:])
            o_ref[0, :, sl] = _merge_out(o, gsub_ref, ga_ref[0, :, sl], ra_ref[0, :, sl], lam_init)


def _attn_sample(qa, kc, vc, kn, vn, ga, ra, lam_q, lam_k, g_sub, *, lam_init):
    DB, T, _ = qa.shape
    blk = pl.BlockSpec((1, T, D_MODEL), lambda b: (b, 0, 0))
    cache = pl.BlockSpec((1,) + kc.shape[1:], lambda b: (b, 0, 0))
    small = lambda shape: pl.BlockSpec(shape, lambda b: (0, 0))
    return pl.pallas_call(
        functools.partial(_attn_sample_kernel, lam_init=lam_init),
        out_shape=jax.ShapeDtypeStruct((DB, T, D_MODEL), BF16),
        grid=(DB,),
        in_specs=[blk, cache, cache, blk, blk, blk, blk, small((2, A_DIM)), small((2, A_DIM)),
                  small((1, HEAD_W))],
        out_specs=blk, name="attn_sample",
        compiler_params=pltpu.CompilerParams(dimension_semantics=("arbitrary",),
                                             vmem_limit_bytes=VMEM_LIMIT),
    )(qa, kc, vc, kn, vn, ga, ra, lam_q, lam_k, g_sub)


def _tail_kernel(x_ref, mix_ref, p_ref, wo_ref, gffn_ref, wg_ref, wu_ref, wd_ref, gple_ref, wple_ref,
                 wpg_ref, y_ref, act_ref):
    x1 = x_ref[...] + _dot(mix_ref[...], wo_ref[...])
    h = (_rms_scale(x1) * gffn_ref[...]).astype(BF16)
    for c in range(D_FF // MXU_N):
        sl = slice(c * MXU_N, (c + 1) * MXU_N)
        g = _dot(h, wg_ref[:, sl])
        u = _dot(h, wu_ref[:, sl])
        act_ref[:, sl] = ((g * jax.nn.sigmoid(g)) * u).astype(BF16)
    x2 = x1 + _dot(act_ref[...], wd_ref[...])
    h3 = (_rms_scale(x2) * gple_ref[...]).astype(BF16)
    gate = jax.nn.sigmoid(_dot(h3, wpg_ref[...]))
    y_ref[...] = x2 + gate * _dot(p_ref[...].astype(BF16), wple_ref[...])


def _tail(x2d, mix2d, p2d, w_o, g_ffn, w_g, w_u, w_d, g_ple, w_ple, w_pg, *, tm, name):
    n = x2d.shape[0]
    row = lambda i: (i, 0)
    const = lambda i: (0, 0)
    resident = functools.partial(pl.BlockSpec, pipeline_mode=pl.Buffered(1))
    return pl.pallas_call(
        _tail_kernel, out_shape=jax.ShapeDtypeStruct((n, D_MODEL), F32), grid=(n // tm,),
        in_specs=[pl.BlockSpec((tm, D_MODEL), row), pl.BlockSpec((tm, D_MODEL), row),
                  pl.BlockSpec((tm, PLE_DIM), row),
                  resident((D_MODEL, D_MODEL), const), resident((1, D_MODEL), const),
                  resident((D_MODEL, D_FF), const), resident((D_MODEL, D_FF), const),
                  resident((D_FF, D_MODEL), const), resident((1, D_MODEL), const),
                  resident((PLE_DIM, D_MODEL), const), resident((D_MODEL, D_MODEL), const)],
        out_specs=pl.BlockSpec((tm, D_MODEL), row),
        scratch_shapes=[pltpu.VMEM((tm, D_FF), BF16)], name=name,
        compiler_params=pltpu.CompilerParams(dimension_semantics=("arbitrary",),
                                             vmem_limit_bytes=VMEM_LIMIT),
    )(x2d, mix2d, p2d, w_o, g_ffn, w_g, w_u, w_d, g_ple, w_ple, w_pg)


def kernel(x_prompt, x_sample, cache_attn_k, cache_attn_v, state_ret, p_prompt, p_sample, w_in, g_mix_norm, g_q_norm, g_k_norm, lam_q, lam_k, g_sub_norm, g_ret_norm, w_o, g_ffn_norm, w_ff_gate, w_ff_up, w_ff_down, g_ple_norm, w_ple, w_ple_gate):
    B, S, D = x_prompt.shape
    DB, T, _ = x_sample.shape
    P = cache_attn_k.shape[2]
    depth = w_in.shape[0]
    assert depth == 1 and D == D_MODEL and S % 512 == 0 and 256 % T == 0
    l = 0
    lam_init = 0.8 - 0.6 * math.exp(-0.3 * l)
    tm = 256

    w_in_b = w_in[l].astype(BF16)
    tail_w = (w_o[l].astype(BF16), g_ffn_norm[l][None, :], w_ff_gate[l].astype(BF16),
              w_ff_up[l].astype(BF16), w_ff_down[l].astype(BF16), g_ple_norm[l][None, :],
              w_ple[l].astype(BF16), w_ple_gate[l].astype(BF16))
    g_mix = g_mix_norm[l][None, :]
    gq = jnp.tile(g_q_norm[l], PROJ_N // A_DIM)[None, :]
    gk = jnp.tile(g_k_norm[l], PROJ_N // A_DIM)[None, :]
    g_sub = g_sub_norm[l][None, :]
    g_rn = g_ret_norm[l]
    grp = jnp.arange(MXU_N) // A_DIM
    gmat = (grp[:, None] == grp[None, :]).astype(BF16)

    cos_p, sin_p = _rope_tables(jnp.arange(S))
    xp2 = x_prompt.reshape(B * S, D)
    qt, kf, kb, vf, vt, ra, ga, st_p = _inproj(
        xp2, w_in_b, g_mix, gq, gk, cos_p, sin_p, gmat, _ret_tables(CHUNK), g_rn,
        tm=512, seq=S, state0=None)
    r3 = lambda a: a.reshape(B, S, D)
    score_bound = ((A_DIM * Q_SCALE * 1.01) * jnp.max(jnp.abs(g_q_norm[l]))
                   * jnp.max(jnp.abs(g_k_norm[l]))).reshape(1, 1)
    mix_p = _attn_prompt(score_bound, qt, r3(kb), vt, r3(ga), r3(ra), lam_q[l], lam_k[l], g_sub,
                         tq=512, lam_init=lam_init)
    y_p = _tail(xp2, mix_p.reshape(B * S, D), p_prompt[l].reshape(B * S, PLE_DIM), *tail_w,
                tm=tm, name="tail_prompt")

    cos_s, sin_s = _rope_tables(P + jnp.arange(T))
    reps = tm // T
    xs2 = x_sample.reshape(DB * T, D)
    st0 = state_ret[l].reshape(DB, HEADS * R_DK, HEAD_W)
    qa_s, kf_s, kb_s, vf_s, vb_s, ra_s, ga_s, st_s = _inproj(
        xs2, w_in_b, g_mix, gq, gk, jnp.tile(cos_s, (reps, 1)), jnp.tile(sin_s, (reps, 1)), gmat,
        _ret_tables(T), g_rn, tm=tm, seq=T, state0=st0)
    s3 = lambda a: a.reshape(DB, T, D)
    mix_s = _attn_sample(s3(qa_s), cache_attn_k[l].reshape(DB, P * HEADS, HEAD_W),
                         cache_attn_v[l].reshape(DB, P * HEADS, HEAD_W),
                         s3(kb_s), s3(vb_s), s3(ga_s), s3(ra_s), lam_q[l], lam_k[l], g_sub,
                         lam_init=lam_init)
    y_s = _tail(xs2, mix_s.reshape(DB * T, D), p_sample[l].reshape(DB * T, PLE_DIM), *tail_w,
                tm=tm, name="tail_sample")

    return (y_p.reshape(B, S, D), y_s.reshape(DB, T, D),
            kf.reshape(1, B, S, HEADS, HEAD_W), vf.reshape(1, B, S, HEADS, HEAD_W),
            st_p.reshape(1, B, HEADS, R_DK, HEAD_W),
            kf_s.reshape(1, DB, T, HEADS, HEAD_W), vf_s.reshape(1, DB, T, HEADS, HEAD_W),
            st_s.reshape(1, DB, HEADS, R_DK, HEAD_W))
```

```python
import functools
import math

import jax
import jax.numpy as jnp
from jax import lax
from jax.experimental import pallas as pl
from jax.experimental.pallas import tpu as pltpu

F32 = jnp.float32
BF16 = jnp.bfloat16

D_MODEL = 1024
CHUNK = 64
PLE_DIM = 256
EPS = 1e-6
HEADS = 8
HEAD_W = 128
A_DIM = 64
R_DK = 64
ROPE_BASE = 10000.0
D_FF = 2816
N_PAIR = HEADS // 2

OFF_QA, OFF_KA, OFF_VA = 0, 1024, 2048
OFF_QR, OFF_KR, OFF_VR = 3072, 3584, 4096
OFF_GRET, OFF_GA, OFF_GR = 5120, 6144, 7168
IN_WIDTH = 8192

MXU_N = 256
PROJ_N = 2 * MXU_N
VMEM_LIMIT = 56 * 1024 * 1024
NEG_BIG = -1e30
ATT_TK = 256
ATT_UNROLL = 8
EXP2_SAFE_RANGE = 60.0
Q_SCALE = (A_DIM ** -0.5) * math.log2(math.e)

_NT = (((1,), (1,)), ((), ()))


def _dot(a, b):
    return jnp.dot(a, b, preferred_element_type=F32)


def _dot_nt(a, b):
    return lax.dot_general(a, b, _NT, preferred_element_type=F32)


def _rms_scale(x):
    return x * lax.rsqrt(jnp.mean(x * x, axis=-1, keepdims=True) + EPS)


def _inproj_body(x_ref, w_ref, gmix_ref, gq_ref, gk_ref, cos_ref, sin_ref, gmat_ref,
                 dpair_ref, qdec_ref, kdec_ref, cdec_ref, grn_ref,
                 kf_ref, kb_ref, vf_ref, ra_ref, ga_ref,
                 qs_ref, ks_ref, vs_ref, gs_ref, *, L, get_state, set_state, store_q, store_v,
                 interleave):
    tm = x_ref.shape[0]
    x = x_ref[...]
    h = (_rms_scale(x) * gmix_ref[...]).astype(BF16)
    gmat = gmat_ref[...]

    def proj(off, c):
        return _dot(h, w_ref[:, off + c * PROJ_N: off + (c + 1) * PROJ_N])

    def group_norm(z, g_ref):
        zz = (z * z).astype(BF16)
        ss = jnp.concatenate([_dot(zz[:, j * MXU_N:(j + 1) * MXU_N], gmat)
                              for j in range(PROJ_N // MXU_N)], axis=1)
        return (z * lax.rsqrt(ss * (1.0 / A_DIM) + EPS)) * g_ref[...]

    for c in range(D_MODEL // PROJ_N):
        sl = slice(c * PROJ_N, (c + 1) * PROJ_N)
        vs_ref[:, sl] = proj(OFF_VR, c).astype(BF16)
        g_ret = proj(OFF_GRET, c)
        gate_r = proj(OFF_GR, c)
        gs_ref[:, sl] = (g_ret * jax.nn.sigmoid(g_ret)) * jax.nn.sigmoid(gate_r)

    lane = lax.broadcasted_iota(jnp.int32, (tm, HEAD_W), 1)
    first_half = (lane % R_DK) < (R_DK // 2)
    cos = cos_ref[...]
    sin = sin_ref[...]

    def rotary(z):
        partner = jnp.where(first_half, pltpu.roll(z, HEAD_W - R_DK // 2, axis=1),
                            pltpu.roll(z, R_DK // 2, axis=1))
        return z * cos + partner * sin

    assert HEADS * R_DK == PROJ_N
    zq = proj(OFF_QR, 0)
    zk = proj(OFF_KR, 0)
    for g in range(N_PAIR):
        hs = slice(g * HEAD_W, (g + 1) * HEAD_W)
        qs_ref[:, hs] = rotary(zq[:, hs]).astype(BF16)
        ks_ref[:, hs] = rotary(zk[:, hs]) * (R_DK ** -0.5)

    def block_diag_cols(x):
        top = lax.broadcasted_iota(jnp.int32, x.shape, 0) < R_DK
        zero = jnp.zeros_like(x)
        return jnp.concatenate([jnp.where(top, x, zero), jnp.where(top, zero, x)], axis=1)

    def block_diag_rows(x):
        left = lax.broadcasted_iota(jnp.int32, x.shape, 1) < HEAD_W
        zero = jnp.zeros_like(x)
        return jnp.concatenate([jnp.where(left, x, zero), jnp.where(left, zero, x)], axis=0)

    def chunk_body(ci, carry):
        r0 = ci * L if isinstance(ci, int) else pl.multiple_of(ci * L, L)
        rows = pl.ds(r0, L)
        for g in range(N_PAIR):
            q2 = qs_ref[rows, g * HEAD_W:(g + 1) * HEAD_W]
            kt = ks_ref[rows, g * HEAD_W:(g + 1) * HEAD_W].T
            v2 = vs_ref[rows, 2 * g * HEAD_W:(2 * g + 2) * HEAD_W]
            a = _dot(q2, block_diag_cols(kt.astype(BF16)))
            a = (a * dpair_ref[g]).astype(BF16)
            intra = _dot(a, block_diag_rows(v2))
            state = get_state(ci, g)
            cross = _dot(q2, block_diag_cols(state.astype(BF16))) * qdec_ref[g]
            kv = _dot((kt * kdec_ref[g]).astype(BF16), v2)
            kv_pair = jnp.concatenate([kv[:R_DK, :HEAD_W], kv[R_DK:, HEAD_W:]], axis=0)
            set_state(ci, g, cdec_ref[g] * state + kv_pair)
            o2 = intra + cross
            for par in range(2):
                hh = 2 * g + par
                cols = slice(hh * HEAD_W, (hh + 1) * HEAD_W)
                on = _rms_scale(o2[:, par * HEAD_W:(par + 1) * HEAD_W]) * grn_ref[hh:hh + 1, :]
                ra_ref[rows, cols] = (on * gs_ref[rows, cols]).astype(BF16)
        return carry

    def q_unit(sl, c):
        store_q(sl, group_norm(proj(OFF_QA, c), gq_ref) * Q_SCALE)

    def store_cache_layout(ref, c, val):
        for j in range(PROJ_N // HEAD_W):
            head = c * (PROJ_N // HEAD_W) + j
            ref[pl.ds(head, tm, stride=HEADS), :] = val[:, j * HEAD_W:(j + 1) * HEAD_W]

    def k_unit(sl, c):
        kn = group_norm(proj(OFF_KA, c), gk_ref)
        store_cache_layout(kf_ref, c, kn)
        kb_ref[:, sl] = kn.astype(BF16)

    def v_unit(sl, c):
        v = proj(OFF_VA, c)
        store_cache_layout(vf_ref, c, v)
        store_v(sl, v)

    def gate_unit(sl, c):
        ga_ref[:, sl] = jax.nn.sigmoid(proj(OFF_GA, c)).astype(BF16)

    units = [functools.partial(u, slice(c * PROJ_N, (c + 1) * PROJ_N), c)
             for c in range(D_MODEL // PROJ_N) for u in (q_unit, k_unit, v_unit, gate_unit)]
    n_chunks = tm // L
    if interleave:
        per_chunk = -(-len(units) // n_chunks)
        for ci in range(n_chunks):
            chunk_body(ci, 0)
            for u in units[ci * per_chunk:(ci + 1) * per_chunk]:
                u()
    else:
        lax.fori_loop(0, n_chunks, chunk_body, 0)
        for u in units:
            u()


def _inproj_prompt_kernel(*refs):
    ins, outs = refs[:13], refs[13:]
    (qt_ref, kf_ref, kb_ref, vf_ref, vt_ref, ra_ref, ga_ref, st_ref,
     qs_ref, ks_ref, vs_ref, gs_ref) = outs

    @pl.when(pl.program_id(1) == 0)
    def _():
        st_ref[...] = jnp.zeros_like(st_ref)

    def get_state(ci, g):
        return st_ref[0, g * HEAD_W:(g + 1) * HEAD_W, :]

    def set_state(ci, g, val):
        st_ref[0, g * HEAD_W:(g + 1) * HEAD_W, :] = val

    def store_q(sl, val):
        qt_ref[0, 0, sl, :] = val.T.astype(BF16)

    def store_v(sl, val):
        for kb_i in range(val.shape[0] // ATT_TK):
            vt_ref[0, kb_i, sl, :] = val[kb_i * ATT_TK:(kb_i + 1) * ATT_TK].T.astype(BF16)

    _inproj_body(*ins, kf_ref, kb_ref, vf_ref, ra_ref, ga_ref, qs_ref, ks_ref, vs_ref, gs_ref,
                 L=CHUNK, get_state=get_state, set_state=set_state, store_q=store_q, store_v=store_v,
                 interleave=True)


def _inproj_sample_kernel(*refs, L):
    ins, s0_ref, outs = refs[:13], refs[13], refs[14:]
    (qa_ref, kf_ref, kb_ref, vf_ref, vb_ref, ra_ref, ga_ref, st_ref,
     qs_ref, ks_ref, vs_ref, gs_ref) = outs

    def get_state(ci, g):
        return s0_ref[ci, g * HEAD_W:(g + 1) * HEAD_W, :]

    def set_state(ci, g, val):
        st_ref[ci, g * HEAD_W:(g + 1) * HEAD_W, :] = val

    def store_q(sl, val):
        qa_ref[:, sl] = val.astype(BF16)

    def store_v(sl, val):
        vb_ref[:, sl] = val.astype(BF16)

    _inproj_body(*ins, kf_ref, kb_ref, vf_ref, ra_ref, ga_ref, qs_ref, ks_ref, vs_ref, gs_ref,
                 L=L, get_state=get_state, set_state=set_state, store_q=store_q, store_v=store_v,
                 interleave=True)


def _ret_tables(T):
    log_g = jnp.log1p(-jnp.exp2(-5.0 - jnp.arange(HEADS, dtype=F32)))
    i = jnp.arange(T, dtype=F32)
    diff = i[:, None] - i[None, :]
    d_mat = jnp.where(diff >= 0, jnp.exp(log_g[:, None, None] * jnp.maximum(diff, 0.0)), 0.0)
    q_decay = jnp.exp(log_g[None, :] * (i[:, None] + 1.0))
    k_decay = jnp.exp(log_g[None, :] * (T - 1.0 - i[:, None]))
    chunk_decay = jnp.exp(log_g * T)
    dpair = d_mat.reshape(N_PAIR, 2, T, T).transpose(0, 2, 1, 3).reshape(N_PAIR, T, 2 * T)
    qdec = jnp.repeat(q_decay.reshape(T, N_PAIR, 2), HEAD_W, axis=2).transpose(1, 0, 2)
    kdec = jnp.repeat(k_decay.T.reshape(N_PAIR, 2, T), R_DK, axis=1)
    cdec = jnp.broadcast_to(jnp.repeat(chunk_decay.reshape(N_PAIR, 2), R_DK, axis=1)[:, :, None],
                            (N_PAIR, 2 * R_DK, HEAD_W))
    return dpair, qdec, kdec, cdec


def _rope_tables(pos):
    half = R_DK // 2
    inv_freq = ROPE_BASE ** (-jnp.arange(half, dtype=F32) / half)
    ang = pos.astype(F32)[:, None] * inv_freq[None, :]
    cos, sin = jnp.cos(ang), jnp.sin(ang)
    return jnp.tile(cos, (1, 4)), jnp.tile(jnp.concatenate([-sin, sin], axis=1), (1, 2))


def _inproj(x2d, w_in, g_mix, gq, gk, cos_t, sin_t, gmat, tables, g_rn, *, tm, seq, state0):
    n = x2d.shape[0]
    prompt = state0 is None
    dpair, qdec, kdec, cdec = tables
    nt = seq // tm if prompt else 1
    nb = n // seq if prompt else n // tm
    per_tile = tm // seq if not prompt else 1

    const2 = lambda *_: (0, 0)
    const3 = lambda *_: (0, 0, 0)
    row = lambda b, i: (b * nt + i, 0)
    resident = functools.partial(pl.BlockSpec, pipeline_mode=pl.Buffered(1))
    in_specs = [
        pl.BlockSpec((tm, D_MODEL), row),
        resident((D_MODEL, IN_WIDTH), const2),
        resident((1, D_MODEL), const2),
        resident((1, PROJ_N), const2),
        resident((1, PROJ_N), const2),
        pl.BlockSpec((tm, HEAD_W), (lambda b, i: (i, 0)) if prompt else const2),
        pl.BlockSpec((tm, HEAD_W), (lambda b, i: (i, 0)) if prompt else const2),
        resident((MXU_N, MXU_N), const2),
        resident(dpair.shape, const3),
        resident(qdec.shape, const3),
        resident(kdec.shape, const3),
        resident(cdec.shape, const3),
        resident((HEADS, HEAD_W), const2),
    ]
    args = [x2d, w_in, g_mix, gq, gk, cos_t, sin_t, gmat, dpair, qdec, kdec, cdec, g_rn]
    st_rows = 2 * R_DK * N_PAIR
    if prompt:
        st_spec = pl.BlockSpec((1, st_rows, HEAD_W), lambda b, i: (b, 0, 0))
        st_shape = jax.ShapeDtypeStruct((nb, st_rows, HEAD_W), F32)
        kernel = _inproj_prompt_kernel
        L = CHUNK
    else:
        st_spec = pl.BlockSpec((per_tile, st_rows, HEAD_W), lambda b, i: (b, 0, 0))
        st_shape = jax.ShapeDtypeStruct(state0.shape, F32)
        in_specs.append(st_spec)
        args.append(state0)
        kernel = functools.partial(_inproj_sample_kernel, L=seq)
        L = seq
    tok = lambda dt: jax.ShapeDtypeStruct((n, D_MODEL), dt)
    tok_spec = pl.BlockSpec((tm, D_MODEL), row)
    cache = jax.ShapeDtypeStruct((n * HEADS, HEAD_W), F32)
    cache_spec = pl.BlockSpec((tm * HEADS, HEAD_W), row)
    out_shape = [tok(BF16), cache, tok(BF16), cache, tok(BF16), tok(BF16), tok(BF16), st_shape]
    out_specs = [tok_spec, cache_spec, tok_spec, cache_spec, tok_spec, tok_spec, tok_spec, st_spec]
    if prompt:
        out_shape[0] = jax.ShapeDtypeStruct((nb, nt, D_MODEL, tm), BF16)
        out_specs[0] = pl.BlockSpec((1, 1, D_MODEL, tm), lambda b, i: (b, i, 0, 0))
        out_shape[4] = jax.ShapeDtypeStruct((nb, seq // ATT_TK, D_MODEL, ATT_TK), BF16)
        out_specs[4] = pl.BlockSpec((1, tm // ATT_TK, D_MODEL, ATT_TK), lambda b, i: (b, i, 0, 0))
    scratch = [pltpu.VMEM((tm, HEADS * R_DK), BF16), pltpu.VMEM((tm, HEADS * R_DK), F32),
               pltpu.VMEM((tm, D_MODEL), BF16), pltpu.VMEM((tm, D_MODEL), F32)]
    return pl.pallas_call(
        kernel, out_shape=out_shape, grid=(nb, nt), in_specs=in_specs, out_specs=out_specs,
        scratch_shapes=scratch, name="inproj_prompt" if prompt else "inproj_sample",
        compiler_params=pltpu.CompilerParams(dimension_semantics=("arbitrary", "arbitrary"),
                                             vmem_limit_bytes=VMEM_LIMIT),
    )(*args)


def _lam(lamq_ref, lamk_ref, lam_init):
    e = jnp.exp(jnp.sum(lamq_ref[...] * lamk_ref[...], axis=-1, keepdims=True))
    return e[0:1, :] - e[1:2, :] + lam_init


def _split_maps(q):
    lane = lax.broadcasted_iota(jnp.int32, q.shape, 1)
    zero = jnp.zeros_like(q)
    return jnp.concatenate([jnp.where(lane < A_DIM, q, zero), jnp.where(lane < A_DIM, zero, q)], axis=0)


def _merge_out(o, gsub_ref, ga, ra, lam_init):
    on = (_rms_scale(o) * gsub_ref[...]) * (1.0 - lam_init)
    return (ga.astype(F32) * on + ra.astype(F32)).astype(BF16)


def _attn_prompt_body(steps_ref, qt_ref, k_ref, vt_ref, ga_ref, ra_ref, lamq_ref, lamk_ref, gsub_ref, o_ref,
                      acc_ref, qq_ref, m_ref, l_ref, d_ref, s0_ref, s1_ref, p0_ref, p1_ref,
                      *, tq, tk, lam_init, bounded):
    n_q = qt_ref.shape[1]
    n_steps = steps_ref.shape[0] - 2
    assert n_steps % ATT_UNROLL == 0 and tq % tk == 0
    s_refs, p_refs = (s0_ref, s1_ref), (p0_ref, p1_ref)

    def scores(s, slot):
        t, i = steps_ref[s, 0], steps_ref[s, 1]
        rows = pl.ds(pl.multiple_of(t * tk, tk), tk)
        s_refs[slot][...] = _dot(k_ref[0, rows, :], qq_ref[i])

    def softmax(s, slot, alpha, diagonal):
        t, i = steps_ref[s, 0], steps_ref[s, 1]
        st = s_refs[slot][...]
        if diagonal:
            st = jnp.where(d_ref[...] <= (tq // CHUNK) * i - (tk // CHUNK) * t, st, NEG_BIG)
        if bounded:
            p = jnp.exp2(st)
            l_ref[i] = l_ref[i] + jnp.sum(p, axis=0, keepdims=True)
            p_refs[slot][...] = p.astype(BF16)
            return alpha
        m_old = m_ref[i]
        m_new = jnp.maximum(m_old, jnp.max(st, axis=0, keepdims=True))
        m_ref[i] = m_new
        p = jnp.exp2(st - m_new)
        alpha_new = jnp.exp2(m_old - m_new)
        l_ref[i] = alpha_new * l_ref[i] + jnp.sum(p, axis=0, keepdims=True)
        p_refs[slot][...] = p.astype(BF16)
        return alpha_new

    def fold_values(s, slot, alpha):
        t, i = steps_ref[s, 0], steps_ref[s, 1]
        pv = _dot(vt_ref[0, t], p_refs[slot][...])
        acc_ref[i] = acc_ref[i] + pv if bounded else alpha * acc_ref[i] + pv

    def step(s, slot, alpha, diagonal):
        fold_values(s - 1, 1 - slot, alpha)
        scores(s + 1, 1 - slot)
        return softmax(s, slot, alpha, diagonal)

    feat = lax.broadcasted_iota(jnp.int32, (HEAD_W, tq), 0)
    for i in range(n_q):
        qt = qt_ref[0, i]
        zero = jnp.zeros_like(qt)
        qq_ref[i] = jnp.concatenate([jnp.where(feat < A_DIM, qt, zero), jnp.where(feat < A_DIM, zero, qt)],
                                    axis=1)
    key = lax.broadcasted_iota(jnp.int32, (tk, 2 * tq), 0)
    qry = lax.broadcasted_iota(jnp.int32, (tk, 2 * tq), 1) % tq
    d_ref[...] = key // CHUNK - qry // CHUNK
    acc_ref[...] = jnp.zeros_like(acc_ref)
    m_ref[...] = jnp.full_like(m_ref, NEG_BIG)
    l_ref[...] = jnp.zeros_like(l_ref)
    p0_ref[...] = jnp.zeros_like(p0_ref)
    scores(1, 1)

    def body(jj, alpha, diagonal):
        for k in range(ATT_UNROLL):
            alpha = step(ATT_UNROLL * jj + 1 + k, (1 + k) & 1, alpha, diagonal)
        return alpha

    n_diag = n_q * (tq // tk)
    assert n_diag % ATT_UNROLL == 0
    alpha = lax.fori_loop(0, n_diag // ATT_UNROLL, functools.partial(body, diagonal=True),
                          jnp.ones((1, 2 * tq), F32))
    alpha = lax.fori_loop(n_diag // ATT_UNROLL, n_steps // ATT_UNROLL,
                          functools.partial(body, diagonal=False), alpha)
    fold_values(n_steps, n_steps & 1, alpha)

    lam = _lam(lamq_ref, lamk_ref, lam_init)

    def finish(i, carry):
        acc = acc_ref[i]
        l = l_ref[i]
        ot = acc[:HEAD_W, :tq] / l[:, :tq] - lam * (acc[:HEAD_W, tq:] / l[:, tq:])
        q_rows = pl.ds(pl.multiple_of(i * tq, tq), tq)
        o_ref[0, q_rows, :] = _merge_out(ot.T, gsub_ref, ga_ref[0, q_rows, :], ra_ref[0, q_rows, :], lam_init)
        return carry

    lax.fori_loop(0, n_q, finish, 0, unroll=4)


def _attn_prompt_kernel(bound_ref, *refs, **static):
    bounded = bound_ref[0, 0] <= EXP2_SAFE_RANGE
    pl.when(bounded)(lambda: _attn_prompt_body(*refs, bounded=True, **static))
    pl.when(jnp.logical_not(bounded))(lambda: _attn_prompt_body(*refs, bounded=False, **static))


def _attn_steps(n_q, blocks_per_q):
    diag = [(i * blocks_per_q + d, i) for i in range(n_q) for d in range(blocks_per_q)]
    full = [(t, i) for t in range(n_q * blocks_per_q) for i in range(t // blocks_per_q + 1, n_q)]
    pairs = diag + full
    return jnp.asarray([pairs[0]] + pairs + [pairs[-1]], jnp.int32)


def _attn_prompt(bound, qt, kb, vt, ga, ra, lam_q, lam_k, g_sub, *, tq, lam_init):
    B, S, _ = kb.shape
    nt, tk = vt.shape[1], vt.shape[3]
    n_q = S // tq
    assert qt.shape[1:] == (n_q, D_MODEL, tq)
    blk = pl.BlockSpec((1, S, HEAD_W), lambda b, h: (b, 0, h))
    small = lambda shape: pl.BlockSpec(shape, lambda b, h: (0, 0))
    smem = pl.BlockSpec(memory_space=pltpu.SMEM)
    return pl.pallas_call(
        functools.partial(_attn_prompt_kernel, tq=tq, tk=tk, lam_init=lam_init),
        out_shape=jax.ShapeDtypeStruct((B, S, D_MODEL), BF16),
        grid=(B, HEADS),
        in_specs=[smem, smem,
                  pl.BlockSpec((1, n_q, HEAD_W, tq), lambda b, h: (b, 0, h, 0)),
                  blk,
                  pl.BlockSpec((1, nt, HEAD_W, tk), lambda b, h: (b, 0, h, 0)),
                  blk, blk, small((2, A_DIM)), small((2, A_DIM)), small((1, HEAD_W))],
        out_specs=blk, name="attn_prompt",
        scratch_shapes=[pltpu.VMEM((n_q, HEAD_W, 2 * tq), F32),
                        pltpu.VMEM((n_q, HEAD_W, 2 * tq), BF16),
                        pltpu.VMEM((n_q, 1, 2 * tq), F32),
                        pltpu.VMEM((n_q, 1, 2 * tq), F32),
                        pltpu.VMEM((tk, 2 * tq), jnp.int32),
                        pltpu.VMEM((tk, 2 * tq), F32), pltpu.VMEM((tk, 2 * tq), F32),
                        pltpu.VMEM((tk, 2 * tq), BF16), pltpu.VMEM((tk, 2 * tq), BF16)],
        compiler_params=pltpu.CompilerParams(dimension_semantics=("arbitrary",) * 2,
                                             vmem_limit_bytes=VMEM_LIMIT),
    )(bound, _attn_steps(n_q, tq // tk), qt, kb, vt, ga, ra, lam_q, lam_k, g_sub)


def _attn_sample_kernel(q_ref, kc_ref, vc_ref, kn_ref, vn_ref, ga_ref, ra_ref, lamq_ref, lamk_ref,
                        gsub_ref, o_ref, *, lam_init):
    t = q_ref.shape[1]
    half = HEADS // 2
    n_mixed = kc_ref.shape[1] // half
    lam = _lam(lamq_ref, lamk_ref, lam_init)
    row = lax.broadcasted_iota(jnp.int32, (4 * t, n_mixed), 0)
    col = lax.broadcasted_iota(jnp.int32, (4 * t, n_mixed), 1)
    own_head = (col & 1) == (row >= 2 * t).astype(jnp.int32)
    for h in range(half):
        heads = (h, h + half)
        sls = [slice(hh * HEAD_W, (hh + 1) * HEAD_W) for hh in heads]
        pair_rows = pl.ds(h, n_mixed, stride=half)
        qqs = [_split_maps(q_ref[0, :, sl]) for sl in sls]
        s_c = _dot_nt(jnp.concatenate(qqs, axis=0), kc_ref[0, pair_rows, :].astype(BF16))
        s_c = jnp.where(own_head, s_c, NEG_BIG)
        s_n = jnp.concatenate([_dot_nt(qq, kn_ref[0, :, sl]) for qq, sl in zip(qqs, sls)], axis=0)
        m = jnp.maximum(jnp.max(s_c, axis=-1, keepdims=True), jnp.max(s_n, axis=-1, keepdims=True))
        p_c = jnp.exp2(s_c - m)
        p_n = jnp.exp2(s_n - m)
        l = jnp.sum(p_c, axis=-1, keepdims=True) + jnp.sum(p_n, axis=-1, keepdims=True)
        acc = _dot(p_c.astype(BF16), vc_ref[0, pair_rows, :].astype(BF16))
        for j, sl in enumerate(sls):
            r0 = 2 * t * j
            a = acc[r0:r0 + 2 * t] + _dot(p_n[r0:r0 + 2 * t].astype(BF16), vn_ref[0, :, sl])
            lj = l[r0:r0 + 2 * t]
            o = a[:t] / lj[:t] - lam * (a[t:] / lj[t:])
            o_ref[0, :, sl] = _merge_out(o, gsub_ref, ga_ref[0, :, sl], ra_ref[0, :, sl], lam_init)


def _attn_sample(qa, kc, vc, kn, vn, ga, ra, lam_q, lam_k, g_sub, *, lam_init):
    DB, T, _ = qa.shape
    blk = pl.BlockSpec((1, T, D_MODEL), lambda b: (b, 0, 0))
    cache = pl.BlockSpec((1,) + kc.shape[1:], lambda b: (b, 0, 0))
    small = lambda shape: pl.BlockSpec(shape, lambda b: (0, 0))
    return pl.pallas_call(
        functools.partial(_attn_sample_kernel, lam_init=lam_init),
        out_shape=jax.ShapeDtypeStruct((DB, T, D_MODEL), BF16),
        grid=(DB,),
        in_specs=[blk, cache, cache, blk, blk, blk, blk, small((2, A_DIM)), small((2, A_DIM)),
                  small((1, HEAD_W))],
        out_specs=blk, name="attn_sample",
        compiler_params=pltpu.CompilerParams(dimension_semantics=("arbitrary",),
                                             vmem_limit_bytes=VMEM_LIMIT),
    )(qa, kc, vc, kn, vn, ga, ra, lam_q, lam_k, g_sub)


def _tail_kernel(x_ref, mix_ref, p_ref, wo_ref, gffn_ref, wg_ref, wu_ref, wd_ref, gple_ref, wple_ref,
                 wpg_ref, y_ref, act_ref):
    x1 = x_ref[...] + _dot(mix_ref[...], wo_ref[...])
    h = (_rms_scale(x1) * gffn_ref[...]).astype(BF16)
    for c in range(D_FF // MXU_N):
        sl = slice(c * MXU_N, (c + 1) * MXU_N)
        g = _dot(h, wg_ref[:, sl])
        u = _dot(h, wu_ref[:, sl])
        act_ref[:, sl] = ((g * jax.nn.sigmoid(g)) * u).astype(BF16)
    x2 = x1 + _dot(act_ref[...], wd_ref[...])
    h3 = (_rms_scale(x2) * gple_ref[...]).astype(BF16)
    gate = jax.nn.sigmoid(_dot(h3, wpg_ref[...]))
    y_ref[...] = x2 + gate * _dot(p_ref[...].astype(BF16), wple_ref[...])


def _tail(x2d, mix2d, p2d, w_o, g_ffn, w_g, w_u, w_d, g_ple, w_ple, w_pg, *, tm, name):
    n = x2d.shape[0]
    row = lambda i: (i, 0)
    const = lambda i: (0, 0)
    resident = functools.partial(pl.BlockSpec, pipeline_mode=pl.Buffered(1))
    return pl.pallas_call(
        _tail_kernel, out_shape=jax.ShapeDtypeStruct((n, D_MODEL), F32), grid=(n // tm,),
        in_specs=[pl.BlockSpec((tm, D_MODEL), row), pl.BlockSpec((tm, D_MODEL), row),
                  pl.BlockSpec((tm, PLE_DIM), row),
                  resident((D_MODEL, D_MODEL), const), resident((1, D_MODEL), const),
                  resident((D_MODEL, D_FF), const), resident((D_MODEL, D_FF), const),
                  resident((D_FF, D_MODEL), const), resident((1, D_MODEL), const),
                  resident((PLE_DIM, D_MODEL), const), resident((D_MODEL, D_MODEL), const)],
        out_specs=pl.BlockSpec((tm, D_MODEL), row),
        scratch_shapes=[pltpu.VMEM((tm, D_FF), BF16)], name=name,
        compiler_params=pltpu.CompilerParams(dimension_semantics=("arbitrary",),
                                             vmem_limit_bytes=VMEM_LIMIT),
    )(x2d, mix2d, p2d, w_o, g_ffn, w_g, w_u, w_d, g_ple, w_ple, w_pg)


def kernel(x_prompt, x_sample, cache_attn_k, cache_attn_v, state_ret, p_prompt, p_sample, w_in, g_mix_norm, g_q_norm, g_k_norm, lam_q, lam_k, g_sub_norm, g_ret_norm, w_o, g_ffn_norm, w_ff_gate, w_ff_up, w_ff_down, g_ple_norm, w_ple, w_ple_gate):
    B, S, D = x_prompt.shape
    DB, T, _ = x_sample.shape
    P = cache_attn_k.shape[2]
    depth = w_in.shape[0]
    assert depth == 1 and D == D_MODEL and S % 512 == 0 and 256 % T == 0
    l = 0
    lam_init = 0.8 - 0.6 * math.exp(-0.3 * l)
    tm = 256

    w_in_b = w_in[l].astype(BF16)
    tail_w = (w_o[l].astype(BF16), g_ffn_norm[l][None, :], w_ff_gate[l].astype(BF16),
              w_ff_up[l].astype(BF16), w_ff_down[l].astype(BF16), g_ple_norm[l][None, :],
              w_ple[l].astype(BF16), w_ple_gate[l].astype(BF16))
    g_mix = g_mix_norm[l][None, :]
    gq = jnp.tile(g_q_norm[l], PROJ_N // A_DIM)[None, :]
    gk = jnp.tile(g_k_norm[l], PROJ_N // A_DIM)[None, :]
    g_sub = g_sub_norm[l][None, :]
    g_rn = g_ret_norm[l]
    grp = jnp.arange(MXU_N) // A_DIM
    gmat = (grp[:, None] == grp[None, :]).astype(BF16)

    cos_p, sin_p = _rope_tables(jnp.arange(S))
    xp2 = x_prompt.reshape(B * S, D)
    qt, kf, kb, vf, vt, ra, ga, st_p = _inproj(
        xp2, w_in_b, g_mix, gq, gk, cos_p, sin_p, gmat, _ret_tables(CHUNK), g_rn,
        tm=512, seq=S, state0=None)
    r3 = lambda a: a.reshape(B, S, D)
    score_bound = ((A_DIM * Q_SCALE * 1.01) * jnp.max(jnp.abs(g_q_norm[l]))
                   * jnp.max(jnp.abs(g_k_norm[l]))).reshape(1, 1)
    mix_p = _attn_prompt(score_bound, qt, r3(kb), vt, r3(ga), r3(ra), lam_q[l], lam_k[l], g_sub,
                         tq=512, lam_init=lam_init)
    y_p = _tail(xp2, mix_p.reshape(B * S, D), p_prompt[l].reshape(B * S, PLE_DIM), *tail_w,
                tm=512, name="tail_prompt")

    cos_s, sin_s = _rope_tables(P + jnp.arange(T))
    reps = tm // T
    xs2 = x_sample.reshape(DB * T, D)
    st0 = state_ret[l].reshape(DB, HEADS * R_DK, HEAD_W)
    qa_s, kf_s, kb_s, vf_s, vb_s, ra_s, ga_s, st_s = _inproj(
        xs2, w_in_b, g_mix, gq, gk, jnp.tile(cos_s, (reps, 1)), jnp.tile(sin_s, (reps, 1)), gmat,
        _ret_tables(T), g_rn, tm=tm, seq=T, state0=st0)
    s3 = lambda a: a.reshape(DB, T, D)
    mix_s = _attn_sample(s3(qa_s), cache_attn_k[l].reshape(DB, P * HEADS, HEAD_W),
                         cache_attn_v[l].reshape(DB, P * HEADS, HEAD_W),
                         s3(kb_s), s3(vb_s), s3(ga_s), s3(ra_s), lam_q[l], lam_k[l], g_sub,
                         lam_init=lam_init)
    y_s = _tail(xs2, mix_s.reshape(DB * T, D), p_sample[l].reshape(DB * T, PLE_DIM), *tail_w,
                tm=tm, name="tail_sample")

    return (y_p.reshape(B, S, D), y_s.reshape(DB, T, D),
            kf.reshape(1, B, S, HEADS, HEAD_W), vf.reshape(1, B, S, HEADS, HEAD_W),
            st_p.reshape(1, B, HEADS, R_DK, HEAD_W),
            kf_s.reshape(1, DB, T, HEADS, HEAD_W), vf_s.reshape(1, DB, T, HEADS, HEAD_W),
            st_s.reshape(1, DB, HEADS, R_DK, HEAD_W))
```

```python
import functools
import math

import jax
import jax.numpy as jnp
from jax import lax
from jax.experimental import pallas as pl
from jax.experimental.pallas import tpu as pltpu

F32 = jnp.float32
BF16 = jnp.bfloat16

D_MODEL = 1024
CHUNK = 64
PLE_DIM = 256
EPS = 1e-6
HEADS = 8
HEAD_W = 128
A_DIM = 64
R_DK = 64
ROPE_BASE = 10000.0
D_FF = 2816
N_PAIR = HEADS // 2

OFF_QA, OFF_KA, OFF_VA = 0, 1024, 2048
OFF_QR, OFF_KR, OFF_VR = 3072, 3584, 4096
OFF_GRET, OFF_GA, OFF_GR = 5120, 6144, 7168
IN_WIDTH = 8192

MXU_N = 256
PROJ_N = 2 * MXU_N
VMEM_LIMIT = 56 * 1024 * 1024
NEG_BIG = -1e30
ATT_TK = 256
ATT_UNROLL = 8
EXP2_SAFE_RANGE = 60.0
Q_SCALE = (A_DIM ** -0.5) * math.log2(math.e)

_NT = (((1,), (1,)), ((), ()))


def _dot(a, b):
    return jnp.dot(a, b, preferred_element_type=F32)


def _dot_nt(a, b):
    return lax.dot_general(a, b, _NT, preferred_element_type=F32)


def _rms_scale(x):
    return x * lax.rsqrt(jnp.mean(x * x, axis=-1, keepdims=True) + EPS)


def _inproj_body(x_ref, w_ref, gmix_ref, gq_ref, gk_ref, cos_ref, sin_ref, gmat_ref,
                 dpair_ref, qdec_ref, kdec_ref, cdec_ref, grn_ref,
                 kf_ref, kb_ref, vf_ref, ra_ref, ga_ref,
                 qs_ref, ks_ref, vs_ref, gs_ref, *, L, get_state, set_state, store_q, store_v,
                 interleave):
    tm = x_ref.shape[0]
    x = x_ref[...]
    h = (_rms_scale(x) * gmix_ref[...]).astype(BF16)
    gmat = gmat_ref[...]

    def proj(off, c):
        return _dot(h, w_ref[:, off + c * PROJ_N: off + (c + 1) * PROJ_N])

    def group_norm(z, g_ref):
        zz = (z * z).astype(BF16)
        ss = jnp.concatenate([_dot(zz[:, j * MXU_N:(j + 1) * MXU_N], gmat)
                              for j in range(PROJ_N // MXU_N)], axis=1)
        return (z * lax.rsqrt(ss * (1.0 / A_DIM) + EPS)) * g_ref[...]

    for c in range(D_MODEL // PROJ_N):
        sl = slice(c * PROJ_N, (c + 1) * PROJ_N)
        vs_ref[:, sl] = proj(OFF_VR, c).astype(BF16)

    lane = lax.broadcasted_iota(jnp.int32, (tm, HEAD_W), 1)
    first_half = (lane % R_DK) < (R_DK // 2)
    cos = cos_ref[...]
    sin = sin_ref[...]

    def rotary(z):
        partner = jnp.where(first_half, pltpu.roll(z, HEAD_W - R_DK // 2, axis=1),
                            pltpu.roll(z, R_DK // 2, axis=1))
        return z * cos + partner * sin

    assert HEADS * R_DK == PROJ_N
    zq = proj(OFF_QR, 0)
    zk = proj(OFF_KR, 0)
    for g in range(N_PAIR):
        hs = slice(g * HEAD_W, (g + 1) * HEAD_W)
        qs_ref[:, hs] = rotary(zq[:, hs]).astype(BF16)
        ks_ref[:, hs] = rotary(zk[:, hs]) * (R_DK ** -0.5)

    def block_diag_cols(x):
        top = lax.broadcasted_iota(jnp.int32, x.shape, 0) < R_DK
        zero = jnp.zeros_like(x)
        return jnp.concatenate([jnp.where(top, x, zero), jnp.where(top, zero, x)], axis=1)

    def block_diag_rows(x):
        left = lax.broadcasted_iota(jnp.int32, x.shape, 1) < HEAD_W
        zero = jnp.zeros_like(x)
        return jnp.concatenate([jnp.where(left, x, zero), jnp.where(left, zero, x)], axis=0)

    def chunk_body(ci, carry):
        r0 = ci * L if isinstance(ci, int) else pl.multiple_of(ci * L, L)
        rows = pl.ds(r0, L)
        for g in range(N_PAIR):
            q2 = qs_ref[rows, g * HEAD_W:(g + 1) * HEAD_W]
            kt = ks_ref[rows, g * HEAD_W:(g + 1) * HEAD_W].T
            v2 = vs_ref[rows, 2 * g * HEAD_W:(2 * g + 2) * HEAD_W]
            a = _dot(q2, block_diag_cols(kt.astype(BF16)))
            a = (a * dpair_ref[g]).astype(BF16)
            intra = _dot(a, block_diag_rows(v2))
            state = get_state(ci, g)
            cross = _dot(q2, block_diag_cols(state.astype(BF16))) * qdec_ref[g]
            kv = _dot((kt * kdec_ref[g]).astype(BF16), v2)
            kv_pair = jnp.concatenate([kv[:R_DK, :HEAD_W], kv[R_DK:, HEAD_W:]], axis=0)
            set_state(ci, g, cdec_ref[g] * state + kv_pair)
            o2 = intra + cross
            for par in range(2):
                hh = 2 * g + par
                cols = slice(hh * HEAD_W, (hh + 1) * HEAD_W)
                on = _rms_scale(o2[:, par * HEAD_W:(par + 1) * HEAD_W]) * grn_ref[hh:hh + 1, :]
                gs_ref[rows, cols] = on
        return carry

    def q_unit(sl, c):
        store_q(sl, group_norm(proj(OFF_QA, c), gq_ref) * Q_SCALE)

    def store_cache_layout(ref, c, val):
        for j in range(PROJ_N // HEAD_W):
            head = c * (PROJ_N // HEAD_W) + j
            ref[pl.ds(head, tm, stride=HEADS), :] = val[:, j * HEAD_W:(j + 1) * HEAD_W]

    def k_unit(sl, c):
        kn = group_norm(proj(OFF_KA, c), gk_ref)
        store_cache_layout(kf_ref, c, kn)
        kb_ref[:, sl] = kn.astype(BF16)

    def v_unit(sl, c):
        v = proj(OFF_VA, c)
        store_cache_layout(vf_ref, c, v)
        store_v(sl, v)

    def gate_unit(sl, c):
        ga_ref[:, sl] = jax.nn.sigmoid(proj(OFF_GA, c)).astype(BF16)

    def retention_gate_unit(sl, c):
        g_ret = proj(OFF_GRET, c)
        gate_r = proj(OFF_GR, c)
        gate = (g_ret * jax.nn.sigmoid(g_ret)) * jax.nn.sigmoid(gate_r)
        ra_ref[:, sl] = (gs_ref[:, sl] * gate).astype(BF16)

    units = [functools.partial(u, slice(c * PROJ_N, (c + 1) * PROJ_N), c)
             for c in range(D_MODEL // PROJ_N) for u in (q_unit, k_unit, v_unit, gate_unit)]
    n_chunks = tm // L
    if interleave:
        per_chunk = -(-len(units) // n_chunks)
        for ci in range(n_chunks):
            chunk_body(ci, 0)
            for u in units[ci * per_chunk:(ci + 1) * per_chunk]:
                u()
    else:
        lax.fori_loop(0, n_chunks, chunk_body, 0)
        for u in units:
            u()
    for c in range(D_MODEL // PROJ_N):
        retention_gate_unit(slice(c * PROJ_N, (c + 1) * PROJ_N), c)


def _inproj_prompt_kernel(*refs):
    ins, outs = refs[:13], refs[13:]
    (qt_ref, kf_ref, kb_ref, vf_ref, vt_ref, ra_ref, ga_ref, st_ref,
     qs_ref, ks_ref, vs_ref, gs_ref) = outs

    @pl.when(pl.program_id(1) == 0)
    def _():
        st_ref[...] = jnp.zeros_like(st_ref)

    def get_state(ci, g):
        return st_ref[0, g * HEAD_W:(g + 1) * HEAD_W, :]

    def set_state(ci, g, val):
        st_ref[0, g * HEAD_W:(g + 1) * HEAD_W, :] = val

    def store_q(sl, val):
        qt_ref[0, 0, sl, :] = val.T.astype(BF16)

    def store_v(sl, val):
        for kb_i in range(val.shape[0] // ATT_TK):
            vt_ref[0, kb_i, sl, :] = val[kb_i * ATT_TK:(kb_i + 1) * ATT_TK].T.astype(BF16)

    _inproj_body(*ins, kf_ref, kb_ref, vf_ref, ra_ref, ga_ref, qs_ref, ks_ref, vs_ref, gs_ref,
                 L=CHUNK, get_state=get_state, set_state=set_state, store_q=store_q, store_v=store_v,
                 interleave=True)


def _inproj_sample_kernel(*refs, L):
    ins, s0_ref, outs = refs[:13], refs[13], refs[14:]
    (qa_ref, kf_ref, kb_ref, vf_ref, vb_ref, ra_ref, ga_ref, st_ref,
     qs_ref, ks_ref, vs_ref, gs_ref) = outs

    def get_state(ci, g):
        return s0_ref[ci, g * HEAD_W:(g + 1) * HEAD_W, :]

    def set_state(ci, g, val):
        st_ref[ci, g * HEAD_W:(g + 1) * HEAD_W, :] = val

    def store_q(sl, val):
        qa_ref[:, sl] = val.astype(BF16)

    def store_v(sl, val):
        vb_ref[:, sl] = val.astype(BF16)

    _inproj_body(*ins, kf_ref, kb_ref, vf_ref, ra_ref, ga_ref, qs_ref, ks_ref, vs_ref, gs_ref,
                 L=L, get_state=get_state, set_state=set_state, store_q=store_q, store_v=store_v,
                 interleave=True)


def _ret_tables(T):
    log_g = jnp.log1p(-jnp.exp2(-5.0 - jnp.arange(HEADS, dtype=F32)))
    i = jnp.arange(T, dtype=F32)
    diff = i[:, None] - i[None, :]
    d_mat = jnp.where(diff >= 0, jnp.exp(log_g[:, None, None] * jnp.maximum(diff, 0.0)), 0.0)
    q_decay = jnp.exp(log_g[None, :] * (i[:, None] + 1.0))
    k_decay = jnp.exp(log_g[None, :] * (T - 1.0 - i[:, None]))
    chunk_decay = jnp.exp(log_g * T)
    dpair = d_mat.reshape(N_PAIR, 2, T, T).transpose(0, 2, 1, 3).reshape(N_PAIR, T, 2 * T)
    qdec = jnp.repeat(q_decay.reshape(T, N_PAIR, 2), HEAD_W, axis=2).transpose(1, 0, 2)
    kdec = jnp.repeat(k_decay.T.reshape(N_PAIR, 2, T), R_DK, axis=1)
    cdec = jnp.broadcast_to(jnp.repeat(chunk_decay.reshape(N_PAIR, 2), R_DK, axis=1)[:, :, None],
                            (N_PAIR, 2 * R_DK, HEAD_W))
    return dpair, qdec, kdec, cdec


def _rope_tables(pos):
    half = R_DK // 2
    inv_freq = ROPE_BASE ** (-jnp.arange(half, dtype=F32) / half)
    ang = pos.astype(F32)[:, None] * inv_freq[None, :]
    cos, sin = jnp.cos(ang), jnp.sin(ang)
    return jnp.tile(cos, (1, 4)), jnp.tile(jnp.concatenate([-sin, sin], axis=1), (1, 2))


def _inproj(x2d, w_in, g_mix, gq, gk, cos_t, sin_t, gmat, tables, g_rn, *, tm, seq, state0):
    n = x2d.shape[0]
    prompt = state0 is None
    dpair, qdec, kdec, cdec = tables
    nt = seq // tm if prompt else 1
    nb = n // seq if prompt else n // tm
    per_tile = tm // seq if not prompt else 1

    const2 = lambda *_: (0, 0)
    const3 = lambda *_: (0, 0, 0)
    row = lambda b, i: (b * nt + i, 0)
    resident = functools.partial(pl.BlockSpec, pipeline_mode=pl.Buffered(1))
    in_specs = [
        pl.BlockSpec((tm, D_MODEL), row),
        resident((D_MODEL, IN_WIDTH), const2),
        resident((1, D_MODEL), const2),
        resident((1, PROJ_N), const2),
        resident((1, PROJ_N), const2),
        pl.BlockSpec((tm, HEAD_W), (lambda b, i: (i, 0)) if prompt else const2),
        pl.BlockSpec((tm, HEAD_W), (lambda b, i: (i, 0)) if prompt else const2),
        resident((MXU_N, MXU_N), const2),
        resident(dpair.shape, const3),
        resident(qdec.shape, const3),
        resident(kdec.shape, const3),
        resident(cdec.shape, const3),
        resident((HEADS, HEAD_W), const2),
    ]
    args = [x2d, w_in, g_mix, gq, gk, cos_t, sin_t, gmat, dpair, qdec, kdec, cdec, g_rn]
    st_rows = 2 * R_DK * N_PAIR
    if prompt:
        st_spec = pl.BlockSpec((1, st_rows, HEAD_W), lambda b, i: (b, 0, 0))
        st_shape = jax.ShapeDtypeStruct((nb, st_rows, HEAD_W), F32)
        kernel = _inproj_prompt_kernel
        L = CHUNK
    else:
        st_spec = pl.BlockSpec((per_tile, st_rows, HEAD_W), lambda b, i: (b, 0, 0))
        st_shape = jax.ShapeDtypeStruct(state0.shape, F32)
        in_specs.append(st_spec)
        args.append(state0)
        kernel = functools.partial(_inproj_sample_kernel, L=seq)
        L = seq
    tok = lambda dt: jax.ShapeDtypeStruct((n, D_MODEL), dt)
    tok_spec = pl.BlockSpec((tm, D_MODEL), row)
    cache = jax.ShapeDtypeStruct((n * HEADS, HEAD_W), F32)
    cache_spec = pl.BlockSpec((tm * HEADS, HEAD_W), row)
    out_shape = [tok(BF16), cache, tok(BF16), cache, tok(BF16), tok(BF16), tok(BF16), st_shape]
    out_specs = [tok_spec, cache_spec, tok_spec, cache_spec, tok_spec, tok_spec, tok_spec, st_spec]
    if prompt:
        out_shape[0] = jax.ShapeDtypeStruct((nb, nt, D_MODEL, tm), BF16)
        out_specs[0] = pl.BlockSpec((1, 1, D_MODEL, tm), lambda b, i: (b, i, 0, 0))
        out_shape[4] = jax.ShapeDtypeStruct((nb, seq // ATT_TK, D_MODEL, ATT_TK), BF16)
        out_specs[4] = pl.BlockSpec((1, tm // ATT_TK, D_MODEL, ATT_TK), lambda b, i: (b, i, 0, 0))
    scratch = [pltpu.VMEM((tm, HEADS * R_DK), BF16), pltpu.VMEM((tm, HEADS * R_DK), F32),
               pltpu.VMEM((tm, D_MODEL), BF16), pltpu.VMEM((tm, D_MODEL), F32)]
    return pl.pallas_call(
        kernel, out_shape=out_shape, grid=(nb, nt), in_specs=in_specs, out_specs=out_specs,
        scratch_shapes=scratch, name="inproj_prompt" if prompt else "inproj_sample",
        compiler_params=pltpu.CompilerParams(dimension_semantics=("arbitrary", "arbitrary"),
                                             vmem_limit_bytes=VMEM_LIMIT),
    )(*args)


def _lam(lamq_ref, lamk_ref, lam_init):
    e = jnp.exp(jnp.sum(lamq_ref[...] * lamk_ref[...], axis=-1, keepdims=True))
    return e[0:1, :] - e[1:2, :] + lam_init


def _split_maps(q):
    lane = lax.broadcasted_iota(jnp.int32, q.shape, 1)
    zero = jnp.zeros_like(q)
    return jnp.concatenate([jnp.where(lane < A_DIM, q, zero), jnp.where(lane < A_DIM, zero, q)], axis=0)


def _merge_out(o, gsub_ref, ga, ra, lam_init):
    on = (_rms_scale(o) * gsub_ref[...]) * (1.0 - lam_init)
    return (ga.astype(F32) * on + ra.astype(F32)).astype(BF16)


def _attn_prompt_body(steps_ref, qt_ref, k_ref, vt_ref, ga_ref, ra_ref, lamq_ref, lamk_ref, gsub_ref, o_ref,
                      acc_ref, qq_ref, m_ref, l_ref, d_ref, s0_ref, s1_ref, p0_ref, p1_ref,
                      *, tq, tk, lam_init, bounded):
    n_q = qt_ref.shape[1]
    n_steps = steps_ref.shape[0] - 2
    assert n_steps % ATT_UNROLL == 0 and tq % tk == 0
    s_refs, p_refs = (s0_ref, s1_ref), (p0_ref, p1_ref)

    def scores(s, slot):
        t, i = steps_ref[s, 0], steps_ref[s, 1]
        rows = pl.ds(pl.multiple_of(t * tk, tk), tk)
        s_refs[slot][...] = _dot(k_ref[0, rows, :], qq_ref[i])

    def softmax(s, slot, alpha, diagonal):
        t, i = steps_ref[s, 0], steps_ref[s, 1]
        st = s_refs[slot][...]
        if diagonal:
            st = jnp.where(d_ref[...] <= (tq // CHUNK) * i - (tk // CHUNK) * t, st, NEG_BIG)
        if bounded:
            p = jnp.exp2(st)
            l_ref[i] = l_ref[i] + jnp.sum(p, axis=0, keepdims=True)
            p_refs[slot][...] = p.astype(BF16)
            return alpha
        m_old = m_ref[i]
        m_new = jnp.maximum(m_old, jnp.max(st, axis=0, keepdims=True))
        m_ref[i] = m_new
        p = jnp.exp2(st - m_new)
        alpha_new = jnp.exp2(m_old - m_new)
        l_ref[i] = alpha_new * l_ref[i] + jnp.sum(p, axis=0, keepdims=True)
        p_refs[slot][...] = p.astype(BF16)
        return alpha_new

    def fold_values(s, slot, alpha):
        t, i = steps_ref[s, 0], steps_ref[s, 1]
        pv = _dot(vt_ref[0, t], p_refs[slot][...])
        acc_ref[i] = acc_ref[i] + pv if bounded else alpha * acc_ref[i] + pv

    def step(s, slot, alpha, diagonal):
        fold_values(s - 1, 1 - slot, alpha)
        scores(s + 1, 1 - slot)
        return softmax(s, slot, alpha, diagonal)

    feat = lax.broadcasted_iota(jnp.int32, (HEAD_W, tq), 0)
    for i in range(n_q):
        qt = qt_ref[0, i]
        zero = jnp.zeros_like(qt)
        qq_ref[i] = jnp.concatenate([jnp.where(feat < A_DIM, qt, zero), jnp.where(feat < A_DIM, zero, qt)],
                                    axis=1)
    key = lax.broadcasted_iota(jnp.int32, (tk, 2 * tq), 0)
    qry = lax.broadcasted_iota(jnp.int32, (tk, 2 * tq), 1) % tq
    d_ref[...] = key // CHUNK - qry // CHUNK
    acc_ref[...] = jnp.zeros_like(acc_ref)
    m_ref[...] = jnp.full_like(m_ref, NEG_BIG)
    l_ref[...] = jnp.zeros_like(l_ref)
    p0_ref[...] = jnp.zeros_like(p0_ref)
    scores(1, 1)

    def body(jj, alpha, diagonal):
        for k in range(ATT_UNROLL):
            alpha = step(ATT_UNROLL * jj + 1 + k, (1 + k) & 1, alpha, diagonal)
        return alpha

    n_diag = n_q * (tq // tk)
    assert n_diag % ATT_UNROLL == 0
    alpha = lax.fori_loop(0, n_diag // ATT_UNROLL, functools.partial(body, diagonal=True),
                          jnp.ones((1, 2 * tq), F32))
    alpha = lax.fori_loop(n_diag // ATT_UNROLL, n_steps // ATT_UNROLL,
                          functools.partial(body, diagonal=False), alpha)
    fold_values(n_steps, n_steps & 1, alpha)

    lam = _lam(lamq_ref, lamk_ref, lam_init)

    def finish(i, carry):
        acc = acc_ref[i]
        l = l_ref[i]
        ot = acc[:HEAD_W, :tq] / l[:, :tq] - lam * (acc[:HEAD_W, tq:] / l[:, tq:])
        q_rows = pl.ds(pl.multiple_of(i * tq, tq), tq)
        o_ref[0, q_rows, :] = _merge_out(ot.T, gsub_ref, ga_ref[0, q_rows, :], ra_ref[0, q_rows, :], lam_init)
        return carry

    lax.fori_loop(0, n_q, finish, 0, unroll=4)


def _attn_prompt_kernel(bound_ref, *refs, **static):
    bounded = bound_ref[0, 0] <= EXP2_SAFE_RANGE
    pl.when(bounded)(lambda: _attn_prompt_body(*refs, bounded=True, **static))
    pl.when(jnp.logical_not(bounded))(lambda: _attn_prompt_body(*refs, bounded=False, **static))


def _attn_steps(n_q, blocks_per_q):
    diag = [(i * blocks_per_q + d, i) for i in range(n_q) for d in range(blocks_per_q)]
    full = [(t, i) for t in range(n_q * blocks_per_q) for i in range(t // blocks_per_q + 1, n_q)]
    pairs = diag + full
    return jnp.asarray([pairs[0]] + pairs + [pairs[-1]], jnp.int32)


def _attn_prompt(bound, qt, kb, vt, ga, ra, lam_q, lam_k, g_sub, *, tq, lam_init):
    B, S, _ = kb.shape
    nt, tk = vt.shape[1], vt.shape[3]
    n_q = S // tq
    assert qt.shape[1:] == (n_q, D_MODEL, tq)
    blk = pl.BlockSpec((1, S, HEAD_W), lambda b, h: (b, 0, h))
    small = lambda shape: pl.BlockSpec(shape, lambda b, h: (0, 0))
    smem = pl.BlockSpec(memory_space=pltpu.SMEM)
    return pl.pallas_call(
        functools.partial(_attn_prompt_kernel, tq=tq, tk=tk, lam_init=lam_init),
        out_shape=jax.ShapeDtypeStruct((B, S, D_MODEL), BF16),
        grid=(B, HEADS),
        in_specs=[smem, smem,
                  pl.BlockSpec((1, n_q, HEAD_W, tq), lambda b, h: (b, 0, h, 0)),
                  blk,
                  pl.BlockSpec((1, nt, HEAD_W, tk), lambda b, h: (b, 0, h, 0)),
                  blk, blk, small((2, A_DIM)), small((2, A_DIM)), small((1, HEAD_W))],
        out_specs=blk, name="attn_prompt",
        scratch_shapes=[pltpu.VMEM((n_q, HEAD_W, 2 * tq), F32),
                        pltpu.VMEM((n_q, HEAD_W, 2 * tq), BF16),
                        pltpu.VMEM((n_q, 1, 2 * tq), F32),
                        pltpu.VMEM((n_q, 1, 2 * tq), F32),
                        pltpu.VMEM((tk, 2 * tq), jnp.int32),
                        pltpu.VMEM((tk, 2 * tq), F32), pltpu.VMEM((tk, 2 * tq), F32),
                        pltpu.VMEM((tk, 2 * tq), BF16), pltpu.VMEM((tk, 2 * tq), BF16)],
        compiler_params=pltpu.CompilerParams(dimension_semantics=("arbitrary",) * 2,
                                             vmem_limit_bytes=VMEM_LIMIT),
    )(bound, _attn_steps(n_q, tq // tk), qt, kb, vt, ga, ra, lam_q, lam_k, g_sub)


def _attn_sample_kernel(q_ref, kc_ref, vc_ref, kn_ref, vn_ref, ga_ref, ra_ref, lamq_ref, lamk_ref,
                        gsub_ref, o_ref, *, lam_init):
    t = q_ref.shape[1]
    half = HEADS // 2
    n_mixed = kc_ref.shape[1] // half
    lam = _lam(lamq_ref, lamk_ref, lam_init)
    row = lax.broadcasted_iota(jnp.int32, (4 * t, n_mixed), 0)
    col = lax.broadcasted_iota(jnp.int32, (4 * t, n_mixed), 1)
    own_head = (col & 1) == (row >= 2 * t).astype(jnp.int32)
    for h in range(half):
        heads = (h, h + half)
        sls = [slice(hh * HEAD_W, (hh + 1) * HEAD_W) for hh in heads]
        pair_rows = pl.ds(h, n_mixed, stride=half)
        qqs = [_split_maps(q_ref[0, :, sl]) for sl in sls]
        s_c = _dot_nt(jnp.concatenate(qqs, axis=0), kc_ref[0, pair_rows, :].astype(BF16))
        s_c = jnp.where(own_head, s_c, NEG_BIG)
        s_n = jnp.concatenate([_dot_nt(qq, kn_ref[0, :, sl]) for qq, sl in zip(qqs, sls)], axis=0)
        m = jnp.maximum(jnp.max(s_c, axis=-1, keepdims=True), jnp.max(s_n, axis=-1, keepdims=True))
        p_c = jnp.exp2(s_c - m)
        p_n = jnp.exp2(s_n - m)
        l = jnp.sum(p_c, axis=-1, keepdims=True) + jnp.sum(p_n, axis=-1, keepdims=True)
        acc = _dot(p_c.astype(BF16), vc_ref[0, pair_rows, :].astype(BF16))
        for j, sl in enumerate(sls):
            r0 = 2 * t * j
            a = acc[r0:r0 + 2 * t] + _dot(p_n[r0:r0 + 2 * t].astype(BF16), vn_ref[0, :, sl])
            lj = l[r0:r0 + 2 * t]
            o = a[:t] / lj[:t] - lam * (a[t:] / lj[t:])
            o_ref[0, :, sl] = _merge_out(o, gsub_ref, ga_ref[0, :, sl], ra_ref[0, :, sl], lam_init)


def _attn_sample(qa, kc, vc, kn, vn, ga, ra, lam_q, lam_k, g_sub, *, lam_init):
    DB, T, _ = qa.shape
    blk = pl.BlockSpec((1, T, D_MODEL), lambda b: (b, 0, 0))
    cache = pl.BlockSpec((1,) + kc.shape[1:], lambda b: (b, 0, 0))
    small = lambda shape: pl.BlockSpec(shape, lambda b: (0, 0))
    return pl.pallas_call(
        functools.partial(_attn_sample_kernel, lam_init=lam_init),
        out_shape=jax.ShapeDtypeStruct((DB, T, D_MODEL), BF16),
        grid=(DB,),
        in_specs=[blk, cache, cache, blk, blk, blk, blk, small((2, A_DIM)), small((2, A_DIM)),
                  small((1, HEAD_W))],
        out_specs=blk, name="attn_sample",
        compiler_params=pltpu.CompilerParams(dimension_semantics=("arbitrary",),
                                             vmem_limit_bytes=VMEM_LIMIT),
    )(qa, kc, vc, kn, vn, ga, ra, lam_q, lam_k, g_sub)


def _tail_kernel(x_ref, mix_ref, p_ref, wo_ref, gffn_ref, wg_ref, wu_ref, wd_ref, gple_ref, wple_ref,
                 wpg_ref, y_ref, act_ref):
    x1 = x_ref[...] + _dot(mix_ref[...], wo_ref[...])
    h = (_rms_scale(x1) * gffn_ref[...]).astype(BF16)
    for c in range(D_FF // MXU_N):
        sl = slice(c * MXU_N, (c + 1) * MXU_N)
        g = _dot(h, wg_ref[:, sl])
        u = _dot(h, wu_ref[:, sl])
        act_ref[:, sl] = ((g * jax.nn.sigmoid(g)) * u).astype(BF16)
    x2 = x1 + _dot(act_ref[...], wd_ref[...])
    h3 = (_rms_scale(x2) * gple_ref[...]).astype(BF16)
    gate = jax.nn.sigmoid(_dot(h3, wpg_ref[...]))
    y_ref[...] = x2 + gate * _dot(p_ref[...].astype(BF16), wple_ref[...])


def _tail(x2d, mix2d, p2d, w_o, g_ffn, w_g, w_u, w_d, g_ple, w_ple, w_pg, *, tm, name):
    n = x2d.shape[0]
    row = lambda i: (i, 0)
    const = lambda i: (0, 0)
    resident = functools.partial(pl.BlockSpec, pipeline_mode=pl.Buffered(1))
    return pl.pallas_call(
        _tail_kernel, out_shape=jax.ShapeDtypeStruct((n, D_MODEL), F32), grid=(n // tm,),
        in_specs=[pl.BlockSpec((tm, D_MODEL), row), pl.BlockSpec((tm, D_MODEL), row),
                  pl.BlockSpec((tm, PLE_DIM), row),
                  resident((D_MODEL, D_MODEL), const), resident((1, D_MODEL), const),
                  resident((D_MODEL, D_FF), const), resident((D_MODEL, D_FF), const),
                  resident((D_FF, D_MODEL), const), resident((1, D_MODEL), const),
                  resident((PLE_DIM, D_MODEL), const), resident((D_MODEL, D_MODEL), const)],
        out_specs=pl.BlockSpec((tm, D_MODEL), row),
        scratch_shapes=[pltpu.VMEM((tm, D_FF), BF16)], name=name,
        compiler_params=pltpu.CompilerParams(dimension_semantics=("arbitrary",),
                                             vmem_limit_bytes=VMEM_LIMIT),
    )(x2d, mix2d, p2d, w_o, g_ffn, w_g, w_u, w_d, g_ple, w_ple, w_pg)


def kernel(x_prompt, x_sample, cache_attn_k, cache_attn_v, state_ret, p_prompt, p_sample, w_in, g_mix_norm, g_q_norm, g_k_norm, lam_q, lam_k, g_sub_norm, g_ret_norm, w_o, g_ffn_norm, w_ff_gate, w_ff_up, w_ff_down, g_ple_norm, w_ple, w_ple_gate):
    B, S, D = x_prompt.shape
    DB, T, _ = x_sample.shape
    P = cache_attn_k.shape[2]
    depth = w_in.shape[0]
    tm = 512
    tm_s = 256
    assert depth == 1 and D == D_MODEL and S % tm == 0 and tm_s % T == 0 and (DB * T) % tm == 0
    l = 0
    lam_init = 0.8 - 0.6 * math.exp(-0.3 * l)

    w_in_b = w_in[l].astype(BF16)
    tail_w = (w_o[l].astype(BF16), g_ffn_norm[l][None, :], w_ff_gate[l].astype(BF16),
              w_ff_up[l].astype(BF16), w_ff_down[l].astype(BF16), g_ple_norm[l][None, :],
              w_ple[l].astype(BF16), w_ple_gate[l].astype(BF16))
    g_mix = g_mix_norm[l][None, :]
    gq = jnp.tile(g_q_norm[l], PROJ_N // A_DIM)[None, :]
    gk = jnp.tile(g_k_norm[l], PROJ_N // A_DIM)[None, :]
    g_sub = g_sub_norm[l][None, :]
    g_rn = g_ret_norm[l]
    grp = jnp.arange(MXU_N) // A_DIM
    gmat = (grp[:, None] == grp[None, :]).astype(BF16)

    cos_p, sin_p = _rope_tables(jnp.arange(S))
    xp2 = x_prompt.reshape(B * S, D)
    qt, kf, kb, vf, vt, ra, ga, st_p = _inproj(
        xp2, w_in_b, g_mix, gq, gk, cos_p, sin_p, gmat, _ret_tables(CHUNK), g_rn,
        tm=tm, seq=S, state0=None)
    r3 = lambda a: a.reshape(B, S, D)
    score_bound = ((A_DIM * Q_SCALE * 1.01) * jnp.max(jnp.abs(g_q_norm[l]))
                   * jnp.max(jnp.abs(g_k_norm[l]))).reshape(1, 1)
    mix_p = _attn_prompt(score_bound, qt, r3(kb), vt, r3(ga), r3(ra), lam_q[l], lam_k[l], g_sub,
                         tq=tm, lam_init=lam_init)
    y_p = _tail(xp2, mix_p.reshape(B * S, D), p_prompt[l].reshape(B * S, PLE_DIM), *tail_w,
                tm=tm, name="tail_prompt")

    cos_s, sin_s = _rope_tables(P + jnp.arange(T))
    reps = tm_s // T
    xs2 = x_sample.reshape(DB * T, D)
    st0 = state_ret[l].reshape(DB, HEADS * R_DK, HEAD_W)
    qa_s, kf_s, kb_s, vf_s, vb_s, ra_s, ga_s, st_s = _inproj(
        xs2, w_in_b, g_mix, gq, gk, jnp.tile(cos_s, (reps, 1)), jnp.tile(sin_s, (reps, 1)), gmat,
        _ret_tables(T), g_rn, tm=tm_s, seq=T, state0=st0)
    s3 = lambda a: a.reshape(DB, T, D)
    mix_s = _attn_sample(s3(qa_s), cache_attn_k[l].reshape(DB, P * HEADS, HEAD_W),
                         cache_attn_v[l].reshape(DB, P * HEADS, HEAD_W),
                         s3(kb_s), s3(vb_s), s3(ga_s), s3(ra_s), lam_q[l], lam_k[l], g_sub,
                         lam_init=lam_init)
    y_s = _tail(xs2, mix_s.reshape(DB * T, D), p_sample[l].reshape(DB * T, PLE_DIM), *tail_w,
                tm=tm, name="tail_sample")

    return (y_p.reshape(B, S, D), y_s.reshape(DB, T, D),
            kf.reshape(1, B, S, HEADS, HEAD_W), vf.reshape(1, B, S, HEADS, HEAD_W),
            st_p.reshape(1, B, HEADS, R_DK, HEAD_W),
            kf_s.reshape(1, DB, T, HEADS, HEAD_W), vf_s.reshape(1, DB, T, HEADS, HEAD_W),
            st_s.reshape(1, DB, HEADS, R_DK, HEAD_W))
```

```python
import functools
import math

import numpy as np

import jax
import jax.numpy as jnp
from jax import lax
from jax.experimental import pallas as pl
from jax.experimental.pallas import tpu as pltpu

F32 = jnp.float32
BF16 = jnp.bfloat16

D_MODEL = 1024
CHUNK = 64
PLE_DIM = 256
EPS = 1e-6
HEADS = 8
HEAD_W = 128
A_DIM = 64
R_DK = 64
ROPE_BASE = 10000.0
D_FF = 2816
N_PAIR = HEADS // 2

OFF_QA, OFF_KA, OFF_VA = 0, 1024, 2048
OFF_QR, OFF_KR, OFF_VR = 3072, 3584, 4096
OFF_GRET, OFF_GA, OFF_GR = 5120, 6144, 7168
IN_WIDTH = 8192

MXU_N = 256
PROJ_N = 2 * MXU_N
VMEM_LIMIT = 56 * 1024 * 1024
NEG_BIG = -1e30
ATT_TK = 256
ATT_UNROLL = 8
EXP2_SAFE_RANGE = 60.0
Q_SCALE = (A_DIM ** -0.5) * math.log2(math.e)

_NT = (((1,), (1,)), ((), ()))


def _dot(a, b):
    return jnp.dot(a, b, preferred_element_type=F32)


def _dot_nt(a, b):
    return lax.dot_general(a, b, _NT, preferred_element_type=F32)


def _rms_scale(x):
    return x * lax.rsqrt(jnp.mean(x * x, axis=-1, keepdims=True) + EPS)


def _inproj_body(x_ref, w_ref, gmix_ref, gq_ref, gk_ref, cos_ref, sin_ref, gmat_ref,
                 dpair_ref, qdec_ref, kdec_ref, cdec_ref, grn_ref,
                 kf_ref, kb_ref, vf_ref, ra_ref, ga_ref,
                 qs_ref, ks_ref, vs_ref, gs_ref, *, L, get_state, set_state, store_q, store_v,
                 interleave):
    tm = x_ref.shape[0]
    x = x_ref[...]
    h = (_rms_scale(x) * gmix_ref[...]).astype(BF16)
    gmat = gmat_ref[...]

    def proj(off, c):
        return _dot(h, w_ref[:, off + c * PROJ_N: off + (c + 1) * PROJ_N])

    def group_norm(z, g_ref):
        zz = (z * z).astype(BF16)
        ss = jnp.concatenate([_dot(zz[:, j * MXU_N:(j + 1) * MXU_N], gmat)
                              for j in range(PROJ_N // MXU_N)], axis=1)
        return (z * lax.rsqrt(ss * (1.0 / A_DIM) + EPS)) * g_ref[...]

    for c in range(D_MODEL // PROJ_N):
        sl = slice(c * PROJ_N, (c + 1) * PROJ_N)
        vs_ref[:, sl] = proj(OFF_VR, c).astype(BF16)

    lane = lax.broadcasted_iota(jnp.int32, (tm, HEAD_W), 1)
    first_half = (lane % R_DK) < (R_DK // 2)
    cos = cos_ref[...]
    sin = sin_ref[...]

    def rotary(z):
        partner = jnp.where(first_half, pltpu.roll(z, HEAD_W - R_DK // 2, axis=1),
                            pltpu.roll(z, R_DK // 2, axis=1))
        return z * cos + partner * sin

    assert HEADS * R_DK == PROJ_N
    zq = proj(OFF_QR, 0)
    zk = proj(OFF_KR, 0)
    for g in range(N_PAIR):
        hs = slice(g * HEAD_W, (g + 1) * HEAD_W)
        qs_ref[:, hs] = rotary(zq[:, hs]).astype(BF16)
        ks_ref[:, hs] = rotary(zk[:, hs]) * (R_DK ** -0.5)

    def block_diag_cols(x):
        top = lax.broadcasted_iota(jnp.int32, x.shape, 0) < R_DK
        zero = jnp.zeros_like(x)
        return jnp.concatenate([jnp.where(top, x, zero), jnp.where(top, zero, x)], axis=1)

    def block_diag_rows(x):
        left = lax.broadcasted_iota(jnp.int32, x.shape, 1) < HEAD_W
        zero = jnp.zeros_like(x)
        return jnp.concatenate([jnp.where(left, x, zero), jnp.where(left, zero, x)], axis=0)

    def chunk_body(ci, carry):
        r0 = ci * L if isinstance(ci, int) else pl.multiple_of(ci * L, L)
        rows = pl.ds(r0, L)
        for g in range(N_PAIR):
            q2 = qs_ref[rows, g * HEAD_W:(g + 1) * HEAD_W]
            kt = ks_ref[rows, g * HEAD_W:(g + 1) * HEAD_W].T
            v2 = vs_ref[rows, 2 * g * HEAD_W:(2 * g + 2) * HEAD_W]
            a = _dot(q2, block_diag_cols(kt.astype(BF16)))
            a = (a * dpair_ref[g]).astype(BF16)
            intra = _dot(a, block_diag_rows(v2))
            state = get_state(ci, g)
            cross = _dot(q2, block_diag_cols(state.astype(BF16))) * qdec_ref[g]
            kv = _dot((kt * kdec_ref[g]).astype(BF16), v2)
            kv_pair = jnp.concatenate([kv[:R_DK, :HEAD_W], kv[R_DK:, HEAD_W:]], axis=0)
            set_state(ci, g, cdec_ref[g] * state + kv_pair)
            o2 = intra + cross
            for par in range(2):
                hh = 2 * g + par
                cols = slice(hh * HEAD_W, (hh + 1) * HEAD_W)
                on = _rms_scale(o2[:, par * HEAD_W:(par + 1) * HEAD_W]) * grn_ref[hh:hh + 1, :]
                gs_ref[rows, cols] = on
        return carry

    def q_unit(sl, c):
        store_q(sl, group_norm(proj(OFF_QA, c), gq_ref) * Q_SCALE)

    def store_cache_layout(ref, c, val):
        for j in range(PROJ_N // HEAD_W):
            head = c * (PROJ_N // HEAD_W) + j
            ref[pl.ds(head, tm, stride=HEADS), :] = val[:, j * HEAD_W:(j + 1) * HEAD_W]

    def k_unit(sl, c):
        kn = group_norm(proj(OFF_KA, c), gk_ref)
        store_cache_layout(kf_ref, c, kn)
        kb_ref[:, sl] = kn.astype(BF16)

    def v_unit(sl, c):
        v = proj(OFF_VA, c)
        store_cache_layout(vf_ref, c, v)
        store_v(sl, v)

    def gate_unit(sl, c):
        ga_ref[:, sl] = jax.nn.sigmoid(proj(OFF_GA, c)).astype(BF16)

    def retention_gate_unit(sl, c):
        g_ret = proj(OFF_GRET, c)
        gate_r = proj(OFF_GR, c)
        gate = (g_ret * jax.nn.sigmoid(g_ret)) * jax.nn.sigmoid(gate_r)
        ra_ref[:, sl] = (gs_ref[:, sl] * gate).astype(BF16)

    units = [functools.partial(u, slice(c * PROJ_N, (c + 1) * PROJ_N), c)
             for c in range(D_MODEL // PROJ_N) for u in (q_unit, k_unit, v_unit, gate_unit)]
    n_chunks = tm // L
    if interleave:
        per_chunk = -(-len(units) // n_chunks)
        for ci in range(n_chunks):
            chunk_body(ci, 0)
            for u in units[ci * per_chunk:(ci + 1) * per_chunk]:
                u()
    else:
        lax.fori_loop(0, n_chunks, chunk_body, 0)
        for u in units:
            u()
    for c in range(D_MODEL // PROJ_N):
        retention_gate_unit(slice(c * PROJ_N, (c + 1) * PROJ_N), c)


def _inproj_prompt_kernel(*refs):
    ins, outs = refs[:13], refs[13:]
    (qt_ref, kf_ref, kb_ref, vf_ref, vt_ref, ra_ref, ga_ref, st_ref,
     qs_ref, ks_ref, vs_ref, gs_ref) = outs

    @pl.when(pl.program_id(1) == 0)
    def _():
        st_ref[...] = jnp.zeros_like(st_ref)

    def get_state(ci, g):
        return st_ref[0, g * HEAD_W:(g + 1) * HEAD_W, :]

    def set_state(ci, g, val):
        st_ref[0, g * HEAD_W:(g + 1) * HEAD_W, :] = val

    def store_q(sl, val):
        qt_ref[0, 0, sl, :] = val.T.astype(BF16)

    def store_v(sl, val):
        for kb_i in range(val.shape[0] // ATT_TK):
            vt_ref[0, kb_i, sl, :] = val[kb_i * ATT_TK:(kb_i + 1) * ATT_TK].T.astype(BF16)

    _inproj_body(*ins, kf_ref, kb_ref, vf_ref, ra_ref, ga_ref, qs_ref, ks_ref, vs_ref, gs_ref,
                 L=CHUNK, get_state=get_state, set_state=set_state, store_q=store_q, store_v=store_v,
                 interleave=True)


def _inproj_sample_kernel(*refs, L):
    ins, s0_ref, outs = refs[:13], refs[13], refs[14:]
    (qa_ref, kf_ref, kb_ref, vf_ref, vb_ref, ra_ref, ga_ref, st_ref,
     qs_ref, ks_ref, vs_ref, gs_ref) = outs

    def get_state(ci, g):
        return s0_ref[ci, g * HEAD_W:(g + 1) * HEAD_W, :]

    def set_state(ci, g, val):
        st_ref[ci, g * HEAD_W:(g + 1) * HEAD_W, :] = val

    def store_q(sl, val):
        qa_ref[:, sl] = val.astype(BF16)

    def store_v(sl, val):
        vb_ref[:, sl] = val.astype(BF16)

    _inproj_body(*ins, kf_ref, kb_ref, vf_ref, ra_ref, ga_ref, qs_ref, ks_ref, vs_ref, gs_ref,
                 L=L, get_state=get_state, set_state=set_state, store_q=store_q, store_v=store_v,
                 interleave=True)


def _f32(a):
    return np.ascontiguousarray(a, np.float32)


def _ret_tables(T):
    log_g = np.log1p(-np.exp2(-5.0 - np.arange(HEADS, dtype=np.float64)))
    i = np.arange(T, dtype=np.float64)
    diff = i[:, None] - i[None, :]
    d_mat = np.where(diff >= 0, np.exp(log_g[:, None, None] * np.maximum(diff, 0.0)), 0.0)
    q_decay = np.exp(log_g[None, :] * (i[:, None] + 1.0))
    k_decay = np.exp(log_g[None, :] * (T - 1.0 - i[:, None]))
    chunk_decay = np.exp(log_g * T)
    dpair = d_mat.reshape(N_PAIR, 2, T, T).transpose(0, 2, 1, 3).reshape(N_PAIR, T, 2 * T)
    qdec = np.repeat(q_decay.reshape(T, N_PAIR, 2), HEAD_W, axis=2).transpose(1, 0, 2)
    kdec = np.repeat(k_decay.T.reshape(N_PAIR, 2, T), R_DK, axis=1)
    cdec = np.broadcast_to(np.repeat(chunk_decay.reshape(N_PAIR, 2), R_DK, axis=1)[:, :, None],
                           (N_PAIR, 2 * R_DK, HEAD_W))
    return _f32(dpair), _f32(qdec), _f32(kdec), _f32(cdec)


def _rope_tables(pos):
    half = R_DK // 2
    inv_freq = ROPE_BASE ** (-jnp.arange(half, dtype=F32) / half)
    ang = pos.astype(F32)[:, None] * inv_freq[None, :]
    cos, sin = jnp.cos(ang), jnp.sin(ang)
    return jnp.tile(cos, (1, 4)), jnp.tile(jnp.concatenate([-sin, sin], axis=1), (1, 2))


def _inproj(x2d, w_in, g_mix, gq, gk, cos_t, sin_t, gmat, tables, g_rn, *, tm, seq, state0):
    n = x2d.shape[0]
    prompt = state0 is None
    dpair, qdec, kdec, cdec = tables
    nt = seq // tm if prompt else 1
    nb = n // seq if prompt else n // tm
    per_tile = tm // seq if not prompt else 1

    const2 = lambda *_: (0, 0)
    const3 = lambda *_: (0, 0, 0)
    row = lambda b, i: (b * nt + i, 0)
    resident = functools.partial(pl.BlockSpec, pipeline_mode=pl.Buffered(1))
    in_specs = [
        pl.BlockSpec((tm, D_MODEL), row),
        resident((D_MODEL, IN_WIDTH), const2),
        resident((1, D_MODEL), const2),
        resident((1, PROJ_N), const2),
        resident((1, PROJ_N), const2),
        pl.BlockSpec((tm, HEAD_W), (lambda b, i: (i, 0)) if prompt else const2),
        pl.BlockSpec((tm, HEAD_W), (lambda b, i: (i, 0)) if prompt else const2),
        resident((MXU_N, MXU_N), const2),
        resident(dpair.shape, const3),
        resident(qdec.shape, const3),
        resident(kdec.shape, const3),
        resident(cdec.shape, const3),
        resident((HEADS, HEAD_W), const2),
    ]
    args = [x2d, w_in, g_mix, gq, gk, cos_t, sin_t, gmat, dpair, qdec, kdec, cdec, g_rn]
    st_rows = 2 * R_DK * N_PAIR
    if prompt:
        st_spec = pl.BlockSpec((1, st_rows, HEAD_W), lambda b, i: (b, 0, 0))
        st_shape = jax.ShapeDtypeStruct((nb, st_rows, HEAD_W), F32)
        kernel = _inproj_prompt_kernel
        L = CHUNK
    else:
        st_spec = pl.BlockSpec((per_tile, st_rows, HEAD_W), lambda b, i: (b, 0, 0))
        st_shape = jax.ShapeDtypeStruct(state0.shape, F32)
        in_specs.append(st_spec)
        args.append(state0)
        kernel = functools.partial(_inproj_sample_kernel, L=seq)
        L = seq
    tok = lambda dt: jax.ShapeDtypeStruct((n, D_MODEL), dt)
    tok_spec = pl.BlockSpec((tm, D_MODEL), row)
    cache = jax.ShapeDtypeStruct((n * HEADS, HEAD_W), F32)
    cache_spec = pl.BlockSpec((tm * HEADS, HEAD_W), row)
    out_shape = [tok(BF16), cache, tok(BF16), cache, tok(BF16), tok(BF16), tok(BF16), st_shape]
    out_specs = [tok_spec, cache_spec, tok_spec, cache_spec, tok_spec, tok_spec, tok_spec, st_spec]
    if prompt:
        out_shape[0] = jax.ShapeDtypeStruct((nb, nt, D_MODEL, tm), BF16)
        out_specs[0] = pl.BlockSpec((1, 1, D_MODEL, tm), lambda b, i: (b, i, 0, 0))
        out_shape[4] = jax.ShapeDtypeStruct((nb, seq // ATT_TK, D_MODEL, ATT_TK), BF16)
        out_specs[4] = pl.BlockSpec((1, tm // ATT_TK, D_MODEL, ATT_TK), lambda b, i: (b, i, 0, 0))
    scratch = [pltpu.VMEM((tm, HEADS * R_DK), BF16), pltpu.VMEM((tm, HEADS * R_DK), F32),
               pltpu.VMEM((tm, D_MODEL), BF16), pltpu.VMEM((tm, D_MODEL), F32)]
    return pl.pallas_call(
        kernel, out_shape=out_shape, grid=(nb, nt), in_specs=in_specs, out_specs=out_specs,
        scratch_shapes=scratch, name="inproj_prompt" if prompt else "inproj_sample",
        compiler_params=pltpu.CompilerParams(dimension_semantics=("arbitrary", "arbitrary"),
                                             vmem_limit_bytes=VMEM_LIMIT),
    )(*args)


def _lam(lamq_ref, lamk_ref, lam_init):
    e = jnp.exp(jnp.sum(lamq_ref[...] * lamk_ref[...], axis=-1, keepdims=True))
    return e[0:1, :] - e[1:2, :] + lam_init


def _split_maps(q):
    lane = lax.broadcasted_iota(jnp.int32, q.shape, 1)
    zero = jnp.zeros_like(q)
    return jnp.concatenate([jnp.where(lane < A_DIM, q, zero), jnp.where(lane < A_DIM, zero, q)], axis=0)


def _merge_out(o, gsub_ref, ga, ra, lam_init):
    on = (_rms_scale(o) * gsub_ref[...]) * (1.0 - lam_init)
    return (ga.astype(F32) * on + ra.astype(F32)).astype(BF16)


def _attn_prompt_body(steps_ref, qt_ref, k_ref, vt_ref, ga_ref, ra_ref, lamq_ref, lamk_ref, gsub_ref, o_ref,
                      acc_ref, qq_ref, m_ref, l_ref, d_ref, s0_ref, s1_ref, p0_ref, p1_ref,
                      *, tq, tk, lam_init, bounded):
    n_q = qt_ref.shape[1]
    n_steps = steps_ref.shape[0] - 2
    assert n_steps % ATT_UNROLL == 0 and tq % tk == 0
    s_refs, p_refs = (s0_ref, s1_ref), (p0_ref, p1_ref)

    def scores(s, slot):
        t, i = steps_ref[s, 0], steps_ref[s, 1]
        rows = pl.ds(pl.multiple_of(t * tk, tk), tk)
        s_refs[slot][...] = _dot(k_ref[0, rows, :], qq_ref[i])

    def softmax(s, slot, alpha, diagonal):
        t, i = steps_ref[s, 0], steps_ref[s, 1]
        st = s_refs[slot][...]
        if diagonal:
            st = jnp.where(d_ref[...] <= (tq // CHUNK) * i - (tk // CHUNK) * t, st, NEG_BIG)
        if bounded:
            p = jnp.exp2(st)
            l_ref[i] = l_ref[i] + jnp.sum(p, axis=0, keepdims=True)
            p_refs[slot][...] = p.astype(BF16)
            return alpha
        m_old = m_ref[i]
        m_new = jnp.maximum(m_old, jnp.max(st, axis=0, keepdims=True))
        m_ref[i] = m_new
        p = jnp.exp2(st - m_new)
        alpha_new = jnp.exp2(m_old - m_new)
        l_ref[i] = alpha_new * l_ref[i] + jnp.sum(p, axis=0, keepdims=True)
        p_refs[slot][...] = p.astype(BF16)
        return alpha_new

    def fold_values(s, slot, alpha):
        t, i = steps_ref[s, 0], steps_ref[s, 1]
        pv = _dot(vt_ref[0, t], p_refs[slot][...])
        acc_ref[i] = acc_ref[i] + pv if bounded else alpha * acc_ref[i] + pv

    def step(s, slot, alpha, diagonal):
        fold_values(s - 1, 1 - slot, alpha)
        scores(s + 1, 1 - slot)
        return softmax(s, slot, alpha, diagonal)

    feat = lax.broadcasted_iota(jnp.int32, (HEAD_W, tq), 0)
    for i in range(n_q):
        qt = qt_ref[0, i]
        zero = jnp.zeros_like(qt)
        qq_ref[i] = jnp.concatenate([jnp.where(feat < A_DIM, qt, zero), jnp.where(feat < A_DIM, zero, qt)],
                                    axis=1)
    @pl.when((pl.program_id(0) == 0) & (pl.program_id(1) == 0))
    def _():
        key = lax.broadcasted_iota(jnp.int32, (tk, 2 * tq), 0)
        qry = lax.broadcasted_iota(jnp.int32, (tk, 2 * tq), 1) % tq
        d_ref[...] = key // CHUNK - qry // CHUNK
    acc_ref[...] = jnp.zeros_like(acc_ref)
    m_ref[...] = jnp.full_like(m_ref, NEG_BIG)
    l_ref[...] = jnp.zeros_like(l_ref)
    p0_ref[...] = jnp.zeros_like(p0_ref)
    scores(1, 1)

    def body(jj, alpha, diagonal):
        for k in range(ATT_UNROLL):
            alpha = step(ATT_UNROLL * jj + 1 + k, (1 + k) & 1, alpha, diagonal)
        return alpha

    n_diag = n_q * (tq // tk)
    assert n_diag % ATT_UNROLL == 0
    alpha = lax.fori_loop(0, n_diag // ATT_UNROLL, functools.partial(body, diagonal=True),
                          jnp.ones((1, 2 * tq), F32))
    alpha = lax.fori_loop(n_diag // ATT_UNROLL, n_steps // ATT_UNROLL,
                          functools.partial(body, diagonal=False), alpha)
    fold_values(n_steps, n_steps & 1, alpha)

    lam = _lam(lamq_ref, lamk_ref, lam_init)

    def finish(i, carry):
        acc = acc_ref[i]
        l = l_ref[i]
        ot = acc[:HEAD_W, :tq] / l[:, :tq] - lam * (acc[:HEAD_W, tq:] / l[:, tq:])
        q_rows = pl.ds(pl.multiple_of(i * tq, tq), tq)
        o_ref[0, q_rows, :] = _merge_out(ot.T, gsub_ref, ga_ref[0, q_rows, :], ra_ref[0, q_rows, :], lam_init)
        return carry

    lax.fori_loop(0, n_q, finish, 0, unroll=4)


def _attn_prompt_kernel(bound_ref, *refs, **static):
    bounded = bound_ref[0, 0] <= EXP2_SAFE_RANGE
    pl.when(bounded)(lambda: _attn_prompt_body(*refs, bounded=True, **static))
    pl.when(jnp.logical_not(bounded))(lambda: _attn_prompt_body(*refs, bounded=False, **static))


def _attn_steps(n_q, blocks_per_q):
    diag = [(i * blocks_per_q + d, i) for i in range(n_q) for d in range(blocks_per_q)]
    full = [(t, i) for t in range(n_q * blocks_per_q) for i in range(t // blocks_per_q + 1, n_q)]
    pairs = diag + full
    return jnp.asarray([pairs[0]] + pairs + [pairs[-1]], jnp.int32)


def _attn_prompt(bound, qt, kb, vt, ga, ra, lam_q, lam_k, g_sub, *, tq, lam_init):
    B, S, _ = kb.shape
    nt, tk = vt.shape[1], vt.shape[3]
    n_q = S // tq
    assert qt.shape[1:] == (n_q, D_MODEL, tq)
    blk = pl.BlockSpec((1, S, HEAD_W), lambda b, h: (b, 0, h))
    small = lambda shape: pl.BlockSpec(shape, lambda b, h: (0, 0))
    smem = pl.BlockSpec(memory_space=pltpu.SMEM)
    return pl.pallas_call(
        functools.partial(_attn_prompt_kernel, tq=tq, tk=tk, lam_init=lam_init),
        out_shape=jax.ShapeDtypeStruct((B, S, D_MODEL), BF16),
        grid=(B, HEADS),
        in_specs=[smem, smem,
                  pl.BlockSpec((1, n_q, HEAD_W, tq), lambda b, h: (b, 0, h, 0)),
                  blk,
                  pl.BlockSpec((1, nt, HEAD_W, tk), lambda b, h: (b, 0, h, 0)),
                  blk, blk, small((2, A_DIM)), small((2, A_DIM)), small((1, HEAD_W))],
        out_specs=blk, name="attn_prompt",
        scratch_shapes=[pltpu.VMEM((n_q, HEAD_W, 2 * tq), F32),
                        pltpu.VMEM((n_q, HEAD_W, 2 * tq), BF16),
                        pltpu.VMEM((n_q, 1, 2 * tq), F32),
                        pltpu.VMEM((n_q, 1, 2 * tq), F32),
                        pltpu.VMEM((tk, 2 * tq), jnp.int32),
                        pltpu.VMEM((tk, 2 * tq), F32), pltpu.VMEM((tk, 2 * tq), F32),
                        pltpu.VMEM((tk, 2 * tq), BF16), pltpu.VMEM((tk, 2 * tq), BF16)],
        compiler_params=pltpu.CompilerParams(dimension_semantics=("arbitrary",) * 2,
                                             vmem_limit_bytes=VMEM_LIMIT),
    )(bound, _attn_steps(n_q, tq // tk), qt, kb, vt, ga, ra, lam_q, lam_k, g_sub)


def _attn_sample_kernel(q_ref, kc_ref, vc_ref, kn_ref, vn_ref, ga_ref, ra_ref, lamq_ref, lamk_ref,
                        gsub_ref, o_ref, *, lam_init):
    t = q_ref.shape[1]
    half = HEADS // 2
    n_mixed = kc_ref.shape[1] // half
    lam = _lam(lamq_ref, lamk_ref, lam_init)
    row = lax.broadcasted_iota(jnp.int32, (4 * t, n_mixed), 0)
    col = lax.broadcasted_iota(jnp.int32, (4 * t, n_mixed), 1)
    own_head = (col & 1) == (row >= 2 * t).astype(jnp.int32)
    for h in range(half):
        heads = (h, h + half)
        sls = [slice(hh * HEAD_W, (hh + 1) * HEAD_W) for hh in heads]
        pair_rows = pl.ds(h, n_mixed, stride=half)
        qqs = [_split_maps(q_ref[0, :, sl]) for sl in sls]
        s_c = _dot_nt(jnp.concatenate(qqs, axis=0), kc_ref[0, pair_rows, :].astype(BF16))
        s_c = jnp.where(own_head, s_c, NEG_BIG)
        s_n = jnp.concatenate([_dot_nt(qq, kn_ref[0, :, sl]) for qq, sl in zip(qqs, sls)], axis=0)
        m = jnp.maximum(jnp.max(s_c, axis=-1, keepdims=True), jnp.max(s_n, axis=-1, keepdims=True))
        p_c = jnp.exp2(s_c - m)
        p_n = jnp.exp2(s_n - m)
        l = jnp.sum(p_c, axis=-1, keepdims=True) + jnp.sum(p_n, axis=-1, keepdims=True)
        acc = _dot(p_c.astype(BF16), vc_ref[0, pair_rows, :].astype(BF16))
        for j, sl in enumerate(sls):
            r0 = 2 * t * j
            a = acc[r0:r0 + 2 * t] + _dot(p_n[r0:r0 + 2 * t].astype(BF16), vn_ref[0, :, sl])
            lj = l[r0:r0 + 2 * t]
            o = a[:t] / lj[:t] - lam * (a[t:] / lj[t:])
            o_ref[0, :, sl] = _merge_out(o, gsub_ref, ga_ref[0, :, sl], ra_ref[0, :, sl], lam_init)


def _attn_sample(qa, kc, vc, kn, vn, ga, ra, lam_q, lam_k, g_sub, *, lam_init):
    DB, T, _ = qa.shape
    blk = pl.BlockSpec((1, T, D_MODEL), lambda b: (b, 0, 0))
    cache = pl.BlockSpec((1,) + kc.shape[1:], lambda b: (b, 0, 0))
    small = lambda shape: pl.BlockSpec(shape, lambda b: (0, 0))
    return pl.pallas_call(
        functools.partial(_attn_sample_kernel, lam_init=lam_init),
        out_shape=jax.ShapeDtypeStruct((DB, T, D_MODEL), BF16),
        grid=(DB,),
        in_specs=[blk, cache, cache, blk, blk, blk, blk, small((2, A_DIM)), small((2, A_DIM)),
                  small((1, HEAD_W))],
        out_specs=blk, name="attn_sample",
        compiler_params=pltpu.CompilerParams(dimension_semantics=("arbitrary",),
                                             vmem_limit_bytes=VMEM_LIMIT),
    )(qa, kc, vc, kn, vn, ga, ra, lam_q, lam_k, g_sub)


def _tail_kernel(x_ref, mix_ref, p_ref, wo_ref, gffn_ref, wg_ref, wu_ref, wd_ref, gple_ref, wple_ref,
                 wpg_ref, y_ref, act_ref):
    x1 = x_ref[...] + _dot(mix_ref[...], wo_ref[...])
    h = (_rms_scale(x1) * gffn_ref[...]).astype(BF16)
    for c in range(D_FF // MXU_N):
        sl = slice(c * MXU_N, (c + 1) * MXU_N)
        g = _dot(h, wg_ref[:, sl])
        u = _dot(h, wu_ref[:, sl])
        act_ref[:, sl] = ((g * jax.nn.sigmoid(g)) * u).astype(BF16)
    x2 = x1 + _dot(act_ref[...], wd_ref[...])
    h3 = (_rms_scale(x2) * gple_ref[...]).astype(BF16)
    gate = jax.nn.sigmoid(_dot(h3, wpg_ref[...]))
    y_ref[...] = x2 + gate * _dot(p_ref[...].astype(BF16), wple_ref[...])


def _tail(x2d, mix2d, p2d, w_o, g_ffn, w_g, w_u, w_d, g_ple, w_ple, w_pg, *, tm, name):
    n = x2d.shape[0]
    row = lambda i: (i, 0)
    const = lambda i: (0, 0)
    resident = functools.partial(pl.BlockSpec, pipeline_mode=pl.Buffered(1))
    return pl.pallas_call(
        _tail_kernel, out_shape=jax.ShapeDtypeStruct((n, D_MODEL), F32), grid=(n // tm,),
        in_specs=[pl.BlockSpec((tm, D_MODEL), row), pl.BlockSpec((tm, D_MODEL), row),
                  pl.BlockSpec((tm, PLE_DIM), row),
                  resident((D_MODEL, D_MODEL), const), resident((1, D_MODEL), const),
                  resident((D_MODEL, D_FF), const), resident((D_MODEL, D_FF), const),
                  resident((D_FF, D_MODEL), const), resident((1, D_MODEL), const),
                  resident((PLE_DIM, D_MODEL), const), resident((D_MODEL, D_MODEL), const)],
        out_specs=pl.BlockSpec((tm, D_MODEL), row),
        scratch_shapes=[pltpu.VMEM((tm, D_FF), BF16)], name=name,
        compiler_params=pltpu.CompilerParams(dimension_semantics=("arbitrary",),
                                             vmem_limit_bytes=VMEM_LIMIT),
    )(x2d, mix2d, p2d, w_o, g_ffn, w_g, w_u, w_d, g_ple, w_ple, w_pg)


def kernel(x_prompt, x_sample, cache_attn_k, cache_attn_v, state_ret, p_prompt, p_sample, w_in, g_mix_norm, g_q_norm, g_k_norm, lam_q, lam_k, g_sub_norm, g_ret_norm, w_o, g_ffn_norm, w_ff_gate, w_ff_up, w_ff_down, g_ple_norm, w_ple, w_ple_gate):
    B, S, D = x_prompt.shape
    DB, T, _ = x_sample.shape
    P = cache_attn_k.shape[2]
    depth = w_in.shape[0]
    tm = 512
    tm_s = 256
    assert depth == 1 and D == D_MODEL and S % tm == 0 and tm_s % T == 0 and (DB * T) % tm == 0
    l = 0
    lam_init = 0.8 - 0.6 * math.exp(-0.3 * l)

    w_in_b = w_in[l].astype(BF16)
    tail_w = (w_o[l].astype(BF16), g_ffn_norm[l][None, :], w_ff_gate[l].astype(BF16),
              w_ff_up[l].astype(BF16), w_ff_down[l].astype(BF16), g_ple_norm[l][None, :],
              w_ple[l].astype(BF16), w_ple_gate[l].astype(BF16))
    g_mix = g_mix_norm[l][None, :]
    gq = jnp.tile(g_q_norm[l], PROJ_N // A_DIM)[None, :]
    gk = jnp.tile(g_k_norm[l], PROJ_N // A_DIM)[None, :]
    g_sub = g_sub_norm[l][None, :]
    g_rn = g_ret_norm[l]
    grp = np.arange(MXU_N) // A_DIM
    gmat = (grp[:, None] == grp[None, :]).astype(BF16)

    cos_p, sin_p = _rope_tables(jnp.arange(S))
    xp2 = x_prompt.reshape(B * S, D)
    qt, kf, kb, vf, vt, ra, ga, st_p = _inproj(
        xp2, w_in_b, g_mix, gq, gk, cos_p, sin_p, gmat, _ret_tables(CHUNK), g_rn,
        tm=tm, seq=S, state0=None)
    r3 = lambda a: a.reshape(B, S, D)
    score_bound = ((A_DIM * Q_SCALE * 1.01) * jnp.max(jnp.abs(g_q_norm[l]))
                   * jnp.max(jnp.abs(g_k_norm[l]))).reshape(1, 1)
    mix_p = _attn_prompt(score_bound, qt, r3(kb), vt, r3(ga), r3(ra), lam_q[l], lam_k[l], g_sub,
                         tq=tm, lam_init=lam_init)
    y_p = _tail(xp2, mix_p.reshape(B * S, D), p_prompt[l].reshape(B * S, PLE_DIM), *tail_w,
                tm=tm, name="tail_prompt")

    cos_s, sin_s = _rope_tables(P + jnp.arange(T))
    reps = tm_s // T
    xs2 = x_sample.reshape(DB * T, D)
    st0 = state_ret[l].reshape(DB, HEADS * R_DK, HEAD_W)
    qa_s, kf_s, kb_s, vf_s, vb_s, ra_s, ga_s, st_s = _inproj(
        xs2, w_in_b, g_mix, gq, gk, jnp.tile(cos_s, (reps, 1)), jnp.tile(sin_s, (reps, 1)), gmat,
        _ret_tables(T), g_rn, tm=tm_s, seq=T, state0=st0)
    s3 = lambda a: a.reshape(DB, T, D)
    mix_s = _attn_sample(s3(qa_s), cache_attn_k[l].reshape(DB, P * HEADS, HEAD_W),
                         cache_attn_v[l].reshape(DB, P * HEADS, HEAD_W),
                         s3(kb_s), s3(vb_s), s3(ga_s), s3(ra_s), lam_q[l], lam_k[l], g_sub,
                         lam_init=lam_init)
    y_s = _tail(xs2, mix_s.reshape(DB * T, D), p_sample[l].reshape(DB * T, PLE_DIM), *tail_w,
                tm=tm, name="tail_sample")

    return (y_p.reshape(B, S, D), y_s.reshape(DB, T, D),
            kf.reshape(1, B, S, HEADS, HEAD_W), vf.reshape(1, B, S, HEADS, HEAD_W),
            st_p.reshape(1, B, HEADS, R_DK, HEAD_W),
            kf_s.reshape(1, DB, T, HEADS, HEAD_W), vf_s.reshape(1, DB, T, HEADS, HEAD_W),
            st_s.reshape(1, DB, HEADS, R_DK, HEAD_W))
```

```python
import functools
import math

import numpy as np

import jax
import jax.numpy as jnp
from jax import lax
from jax.experimental import pallas as pl
from jax.experimental.pallas import tpu as pltpu

F32 = jnp.float32
BF16 = jnp.bfloat16

D_MODEL = 1024
CHUNK = 64
PLE_DIM = 256
EPS = 1e-6
HEADS = 8
HEAD_W = 128
A_DIM = 64
R_DK = 64
ROPE_BASE = 10000.0
D_FF = 2816
N_PAIR = HEADS // 2

OFF_QA, OFF_KA, OFF_VA = 0, 1024, 2048
OFF_QR, OFF_KR, OFF_VR = 3072, 3584, 4096
OFF_GRET, OFF_GA, OFF_GR = 5120, 6144, 7168
IN_WIDTH = 8192

MXU_N = 256
PROJ_N = 2 * MXU_N
VMEM_LIMIT = 56 * 1024 * 1024
NEG_BIG = -1e30
ATT_TK = 256
ATT_UNROLL = 8
ATT_UNROLL_FULL = 14
EXP2_SAFE_RANGE = 60.0
Q_SCALE = (A_DIM ** -0.5) * math.log2(math.e)

_NT = (((1,), (1,)), ((), ()))


def _dot(a, b):
    return jnp.dot(a, b, preferred_element_type=F32)


def _dot_nt(a, b):
    return lax.dot_general(a, b, _NT, preferred_element_type=F32)


def _rms_scale(x):
    return x * lax.rsqrt(jnp.mean(x * x, axis=-1, keepdims=True) + EPS)


def _inproj_body(x_ref, w_ref, gmix_ref, gq_ref, gk_ref, cos_ref, sin_ref, gmat_ref,
                 dpair_ref, qdec_ref, kdec_ref, cdec_ref, grn_ref,
                 kf_ref, kb_ref, vf_ref, ra_ref, ga_ref,
                 qs_ref, ks_ref, vs_ref, gs_ref, *, L, get_state, set_state, store_q, store_v,
                 interleave):
    tm = x_ref.shape[0]
    x = x_ref[...]
    h = (_rms_scale(x) * gmix_ref[...]).astype(BF16)
    gmat = gmat_ref[...]

    def proj(off, c):
        return _dot(h, w_ref[:, off + c * PROJ_N: off + (c + 1) * PROJ_N])

    def group_norm(z, g_ref):
        zz = (z * z).astype(BF16)
        ss = jnp.concatenate([_dot(zz[:, j * MXU_N:(j + 1) * MXU_N], gmat)
                              for j in range(PROJ_N // MXU_N)], axis=1)
        return (z * lax.rsqrt(ss * (1.0 / A_DIM) + EPS)) * g_ref[...]

    for c in range(D_MODEL // PROJ_N):
        sl = slice(c * PROJ_N, (c + 1) * PROJ_N)
        vs_ref[:, sl] = proj(OFF_VR, c).astype(BF16)

    lane = lax.broadcasted_iota(jnp.int32, (tm, HEAD_W), 1)
    first_half = (lane % R_DK) < (R_DK // 2)
    cos = cos_ref[...]
    sin = sin_ref[...]

    def rotary(z):
        partner = jnp.where(first_half, pltpu.roll(z, HEAD_W - R_DK // 2, axis=1),
                            pltpu.roll(z, R_DK // 2, axis=1))
        return z * cos + partner * sin

    assert HEADS * R_DK == PROJ_N
    zq = proj(OFF_QR, 0)
    zk = proj(OFF_KR, 0)
    for g in range(N_PAIR):
        hs = slice(g * HEAD_W, (g + 1) * HEAD_W)
        qs_ref[:, hs] = rotary(zq[:, hs]).astype(BF16)
        ks_ref[:, hs] = rotary(zk[:, hs]) * (R_DK ** -0.5)

    def block_diag_cols(x):
        top = lax.broadcasted_iota(jnp.int32, x.shape, 0) < R_DK
        zero = jnp.zeros_like(x)
        return jnp.concatenate([jnp.where(top, x, zero), jnp.where(top, zero, x)], axis=1)

    def block_diag_rows(x):
        left = lax.broadcasted_iota(jnp.int32, x.shape, 1) < HEAD_W
        zero = jnp.zeros_like(x)
        return jnp.concatenate([jnp.where(left, x, zero), jnp.where(left, zero, x)], axis=0)

    def chunk_body(ci, carry):
        r0 = ci * L if isinstance(ci, int) else pl.multiple_of(ci * L, L)
        rows = pl.ds(r0, L)
        for g in range(N_PAIR):
            q2 = qs_ref[rows, g * HEAD_W:(g + 1) * HEAD_W]
            kt = ks_ref[rows, g * HEAD_W:(g + 1) * HEAD_W].T
            v2 = vs_ref[rows, 2 * g * HEAD_W:(2 * g + 2) * HEAD_W]
            a = _dot(q2, block_diag_cols(kt.astype(BF16)))
            a = (a * dpair_ref[g]).astype(BF16)
            intra = _dot(a, block_diag_rows(v2))
            state = get_state(ci, g)
            cross = _dot(q2, block_diag_cols(state.astype(BF16))) * qdec_ref[g]
            kv = _dot((kt * kdec_ref[g]).astype(BF16), v2)
            kv_pair = jnp.concatenate([kv[:R_DK, :HEAD_W], kv[R_DK:, HEAD_W:]], axis=0)
            set_state(ci, g, cdec_ref[g] * state + kv_pair)
            o2 = intra + cross
            for par in range(2):
                hh = 2 * g + par
                cols = slice(hh * HEAD_W, (hh + 1) * HEAD_W)
                on = _rms_scale(o2[:, par * HEAD_W:(par + 1) * HEAD_W]) * grn_ref[hh:hh + 1, :]
                gs_ref[rows, cols] = on
        return carry

    def q_unit(sl, c):
        store_q(sl, group_norm(proj(OFF_QA, c), gq_ref) * Q_SCALE)

    def store_cache_layout(ref, c, val):
        for j in range(PROJ_N // HEAD_W):
            head = c * (PROJ_N // HEAD_W) + j
            ref[pl.ds(head, tm, stride=HEADS), :] = val[:, j * HEAD_W:(j + 1) * HEAD_W]

    def k_unit(sl, c):
        kn = group_norm(proj(OFF_KA, c), gk_ref)
        store_cache_layout(kf_ref, c, kn)
        kb_ref[:, sl] = kn.astype(BF16)

    def v_unit(sl, c):
        v = proj(OFF_VA, c)
        store_cache_layout(vf_ref, c, v)
        store_v(sl, v)

    def gate_unit(sl, c):
        ga_ref[:, sl] = jax.nn.sigmoid(proj(OFF_GA, c)).astype(BF16)

    def retention_gate_unit(sl, c):
        g_ret = proj(OFF_GRET, c)
        gate_r = proj(OFF_GR, c)
        gate = (g_ret * jax.nn.sigmoid(g_ret)) * jax.nn.sigmoid(gate_r)
        ra_ref[:, sl] = (gs_ref[:, sl] * gate).astype(BF16)

    units = [functools.partial(u, slice(c * PROJ_N, (c + 1) * PROJ_N), c)
             for c in range(D_MODEL // PROJ_N) for u in (q_unit, k_unit, v_unit, gate_unit)]
    n_chunks = tm // L
    if interleave:
        per_chunk = -(-len(units) // n_chunks)
        for ci in range(n_chunks):
            chunk_body(ci, 0)
            for u in units[ci * per_chunk:(ci + 1) * per_chunk]:
                u()
    else:
        lax.fori_loop(0, n_chunks, chunk_body, 0)
        for u in units:
            u()
    for c in range(D_MODEL // PROJ_N):
        retention_gate_unit(slice(c * PROJ_N, (c + 1) * PROJ_N), c)


def _inproj_prompt_kernel(*refs):
    ins, outs = refs[:13], refs[13:]
    (qt_ref, kf_ref, kb_ref, vf_ref, vt_ref, ra_ref, ga_ref, st_ref,
     qs_ref, ks_ref, vs_ref, gs_ref) = outs

    @pl.when(pl.program_id(1) == 0)
    def _():
        st_ref[...] = jnp.zeros_like(st_ref)

    def get_state(ci, g):
        return st_ref[0, g * HEAD_W:(g + 1) * HEAD_W, :]

    def set_state(ci, g, val):
        st_ref[0, g * HEAD_W:(g + 1) * HEAD_W, :] = val

    def store_q(sl, val):
        qt_ref[0, 0, sl, :] = val.T.astype(BF16)

    def store_v(sl, val):
        for kb_i in range(val.shape[0] // ATT_TK):
            vt_ref[0, kb_i, sl, :] = val[kb_i * ATT_TK:(kb_i + 1) * ATT_TK].T.astype(BF16)

    _inproj_body(*ins, kf_ref, kb_ref, vf_ref, ra_ref, ga_ref, qs_ref, ks_ref, vs_ref, gs_ref,
                 L=CHUNK, get_state=get_state, set_state=set_state, store_q=store_q, store_v=store_v,
                 interleave=True)


def _inproj_sample_kernel(*refs, L):
    ins, s0_ref, outs = refs[:13], refs[13], refs[14:]
    (qa_ref, kf_ref, kb_ref, vf_ref, vb_ref, ra_ref, ga_ref, st_ref,
     qs_ref, ks_ref, vs_ref, gs_ref) = outs

    def get_state(ci, g):
        return s0_ref[ci, g * HEAD_W:(g + 1) * HEAD_W, :]

    def set_state(ci, g, val):
        st_ref[ci, g * HEAD_W:(g + 1) * HEAD_W, :] = val

    def store_q(sl, val):
        qa_ref[:, sl] = val.astype(BF16)

    def store_v(sl, val):
        vb_ref[:, sl] = val.astype(BF16)

    _inproj_body(*ins, kf_ref, kb_ref, vf_ref, ra_ref, ga_ref, qs_ref, ks_ref, vs_ref, gs_ref,
                 L=L, get_state=get_state, set_state=set_state, store_q=store_q, store_v=store_v,
                 interleave=True)


def _f32(a):
    return np.ascontiguousarray(a, np.float32)


def _ret_tables(T):
    log_g = np.log1p(-np.exp2(-5.0 - np.arange(HEADS, dtype=np.float64)))
    i = np.arange(T, dtype=np.float64)
    diff = i[:, None] - i[None, :]
    d_mat = np.where(diff >= 0, np.exp(log_g[:, None, None] * np.maximum(diff, 0.0)), 0.0)
    q_decay = np.exp(log_g[None, :] * (i[:, None] + 1.0))
    k_decay = np.exp(log_g[None, :] * (T - 1.0 - i[:, None]))
    chunk_decay = np.exp(log_g * T)
    dpair = d_mat.reshape(N_PAIR, 2, T, T).transpose(0, 2, 1, 3).reshape(N_PAIR, T, 2 * T)
    qdec = np.repeat(q_decay.reshape(T, N_PAIR, 2), HEAD_W, axis=2).transpose(1, 0, 2)
    kdec = np.repeat(k_decay.T.reshape(N_PAIR, 2, T), R_DK, axis=1)
    cdec = np.broadcast_to(np.repeat(chunk_decay.reshape(N_PAIR, 2), R_DK, axis=1)[:, :, None],
                           (N_PAIR, 2 * R_DK, HEAD_W))
    return _f32(dpair), _f32(qdec), _f32(kdec), _f32(cdec)


def _rope_tables(pos):
    half = R_DK // 2
    inv_freq = ROPE_BASE ** (-jnp.arange(half, dtype=F32) / half)
    ang = pos.astype(F32)[:, None] * inv_freq[None, :]
    cos, sin = jnp.cos(ang), jnp.sin(ang)
    return jnp.tile(cos, (1, 4)), jnp.tile(jnp.concatenate([-sin, sin], axis=1), (1, 2))


def _inproj(x2d, w_in, g_mix, gq, gk, cos_t, sin_t, gmat, tables, g_rn, *, tm, seq, state0):
    n = x2d.shape[0]
    prompt = state0 is None
    dpair, qdec, kdec, cdec = tables
    nt = seq // tm if prompt else 1
    nb = n // seq if prompt else n // tm
    per_tile = tm // seq if not prompt else 1

    const2 = lambda *_: (0, 0)
    const3 = lambda *_: (0, 0, 0)
    row = lambda b, i: (b * nt + i, 0)
    resident = functools.partial(pl.BlockSpec, pipeline_mode=pl.Buffered(1))
    in_specs = [
        pl.BlockSpec((tm, D_MODEL), row),
        resident((D_MODEL, IN_WIDTH), const2),
        resident((1, D_MODEL), const2),
        resident((1, PROJ_N), const2),
        resident((1, PROJ_N), const2),
        pl.BlockSpec((tm, HEAD_W), (lambda b, i: (i, 0)) if prompt else const2),
        pl.BlockSpec((tm, HEAD_W), (lambda b, i: (i, 0)) if prompt else const2),
        resident((MXU_N, MXU_N), const2),
        resident(dpair.shape, const3),
        resident(qdec.shape, const3),
        resident(kdec.shape, const3),
        resident(cdec.shape, const3),
        resident((HEADS, HEAD_W), const2),
    ]
    args = [x2d, w_in, g_mix, gq, gk, cos_t, sin_t, gmat, dpair, qdec, kdec, cdec, g_rn]
    st_rows = 2 * R_DK * N_PAIR
    if prompt:
        st_spec = pl.BlockSpec((1, st_rows, HEAD_W), lambda b, i: (b, 0, 0))
        st_shape = jax.ShapeDtypeStruct((nb, st_rows, HEAD_W), F32)
        kernel = _inproj_prompt_kernel
        L = CHUNK
    else:
        st_spec = pl.BlockSpec((per_tile, st_rows, HEAD_W), lambda b, i: (b, 0, 0))
        st_shape = jax.ShapeDtypeStruct(state0.shape, F32)
        in_specs.append(st_spec)
        args.append(state0)
        kernel = functools.partial(_inproj_sample_kernel, L=seq)
        L = seq
    tok = lambda dt: jax.ShapeDtypeStruct((n, D_MODEL), dt)
    tok_spec = pl.BlockSpec((tm, D_MODEL), row)
    cache = jax.ShapeDtypeStruct((n * HEADS, HEAD_W), F32)
    cache_spec = pl.BlockSpec((tm * HEADS, HEAD_W), row)
    out_shape = [tok(BF16), cache, tok(BF16), cache, tok(BF16), tok(BF16), tok(BF16), st_shape]
    out_specs = [tok_spec, cache_spec, tok_spec, cache_spec, tok_spec, tok_spec, tok_spec, st_spec]
    if prompt:
        out_shape[0] = jax.ShapeDtypeStruct((nb, nt, D_MODEL, tm), BF16)
        out_specs[0] = pl.BlockSpec((1, 1, D_MODEL, tm), lambda b, i: (b, i, 0, 0))
        out_shape[4] = jax.ShapeDtypeStruct((nb, seq // ATT_TK, D_MODEL, ATT_TK), BF16)
        out_specs[4] = pl.BlockSpec((1, tm // ATT_TK, D_MODEL, ATT_TK), lambda b, i: (b, i, 0, 0))
    scratch = [pltpu.VMEM((tm, HEADS * R_DK), BF16), pltpu.VMEM((tm, HEADS * R_DK), F32),
               pltpu.VMEM((tm, D_MODEL), BF16), pltpu.VMEM((tm, D_MODEL), F32)]
    return pl.pallas_call(
        kernel, out_shape=out_shape, grid=(nb, nt), in_specs=in_specs, out_specs=out_specs,
        scratch_shapes=scratch, name="inproj_prompt" if prompt else "inproj_sample",
        compiler_params=pltpu.CompilerParams(dimension_semantics=("arbitrary", "arbitrary"),
                                             vmem_limit_bytes=VMEM_LIMIT),
    )(*args)


def _lam(lamq_ref, lamk_ref, lam_init):
    e = jnp.exp(jnp.sum(lamq_ref[...] * lamk_ref[...], axis=-1, keepdims=True))
    return e[0:1, :] - e[1:2, :] + lam_init


def _split_maps(q):
    lane = lax.broadcasted_iota(jnp.int32, q.shape, 1)
    zero = jnp.zeros_like(q)
    return jnp.concatenate([jnp.where(lane < A_DIM, q, zero), jnp.where(lane < A_DIM, zero, q)], axis=0)


def _merge_out(o, gsub_ref, ga, ra, lam_init):
    on = (_rms_scale(o) * gsub_ref[...]) * (1.0 - lam_init)
    return (ga.astype(F32) * on + ra.astype(F32)).astype(BF16)


def _attn_prompt_body(steps_ref, qt_ref, k_ref, vt_ref, ga_ref, ra_ref, lamq_ref, lamk_ref, gsub_ref, o_ref,
                      acc_ref, qq_ref, m_ref, l_ref, d_ref, s0_ref, s1_ref, p0_ref, p1_ref,
                      *, tq, tk, lam_init, bounded):
    n_q = qt_ref.shape[1]
    n_steps = steps_ref.shape[0] - 2
    assert n_steps % ATT_UNROLL == 0 and tq % tk == 0
    s_refs, p_refs = (s0_ref, s1_ref), (p0_ref, p1_ref)

    def scores(s, slot):
        t, i = steps_ref[s, 0], steps_ref[s, 1]
        rows = pl.ds(pl.multiple_of(t * tk, tk), tk)
        s_refs[slot][...] = _dot(k_ref[0, rows, :], qq_ref[i])

    def softmax(s, slot, alpha, diagonal):
        t, i = steps_ref[s, 0], steps_ref[s, 1]
        st = s_refs[slot][...]
        if diagonal:
            st = jnp.where(d_ref[...] <= (tq // CHUNK) * i - (tk // CHUNK) * t, st, NEG_BIG)
        if bounded:
            p = jnp.exp2(st)
            l_ref[i] = l_ref[i] + jnp.sum(p, axis=0, keepdims=True)
            p_refs[slot][...] = p.astype(BF16)
            return alpha
        m_old = m_ref[i]
        m_new = jnp.maximum(m_old, jnp.max(st, axis=0, keepdims=True))
        m_ref[i] = m_new
        p = jnp.exp2(st - m_new)
        alpha_new = jnp.exp2(m_old - m_new)
        l_ref[i] = alpha_new * l_ref[i] + jnp.sum(p, axis=0, keepdims=True)
        p_refs[slot][...] = p.astype(BF16)
        return alpha_new

    def fold_values(s, slot, alpha):
        t, i = steps_ref[s, 0], steps_ref[s, 1]
        pv = _dot(vt_ref[0, t], p_refs[slot][...])
        acc_ref[i] = acc_ref[i] + pv if bounded else alpha * acc_ref[i] + pv

    def step(s, slot, alpha, diagonal):
        fold_values(s - 1, 1 - slot, alpha)
        scores(s + 1, 1 - slot)
        return softmax(s, slot, alpha, diagonal)

    feat = lax.broadcasted_iota(jnp.int32, (HEAD_W, tq), 0)
    for i in range(n_q):
        qt = qt_ref[0, i]
        zero = jnp.zeros_like(qt)
        qq_ref[i] = jnp.concatenate([jnp.where(feat < A_DIM, qt, zero), jnp.where(feat < A_DIM, zero, qt)],
                                    axis=1)
    @pl.when((pl.program_id(0) == 0) & (pl.program_id(1) == 0))
    def _():
        key = lax.broadcasted_iota(jnp.int32, (tk, 2 * tq), 0)
        qry = lax.broadcasted_iota(jnp.int32, (tk, 2 * tq), 1) % tq
        d_ref[...] = key // CHUNK - qry // CHUNK
    acc_ref[...] = jnp.zeros_like(acc_ref)
    m_ref[...] = jnp.full_like(m_ref, NEG_BIG)
    l_ref[...] = jnp.zeros_like(l_ref)
    p0_ref[...] = jnp.zeros_like(p0_ref)
    scores(1, 1)

    def body(jj, alpha, diagonal, first, unroll):
        for k in range(unroll):
            alpha = step(first + unroll * jj + k, (1 + k) & 1, alpha, diagonal)
        return alpha

    n_diag = n_q * (tq // tk)
    assert n_diag % ATT_UNROLL == 0
    alpha = lax.fori_loop(0, n_diag // ATT_UNROLL,
                          functools.partial(body, diagonal=True, first=1, unroll=ATT_UNROLL),
                          jnp.ones((1, 2 * tq), F32))
    n_full = n_steps - n_diag
    assert n_full % ATT_UNROLL_FULL == 0 and n_diag % 2 == 0 and ATT_UNROLL_FULL % 2 == 0
    alpha = lax.fori_loop(0, n_full // ATT_UNROLL_FULL,
                          functools.partial(body, diagonal=False, first=n_diag + 1, unroll=ATT_UNROLL_FULL),
                          alpha)
    fold_values(n_steps, n_steps & 1, alpha)

    lam = _lam(lamq_ref, lamk_ref, lam_init)

    def finish(i, carry):
        acc = acc_ref[i]
        l = l_ref[i]
        ot = acc[:HEAD_W, :tq] / l[:, :tq] - lam * (acc[:HEAD_W, tq:] / l[:, tq:])
        q_rows = pl.ds(pl.multiple_of(i * tq, tq), tq)
        o_ref[0, q_rows, :] = _merge_out(ot.T, gsub_ref, ga_ref[0, q_rows, :], ra_ref[0, q_rows, :], lam_init)
        return carry

    lax.fori_loop(0, n_q, finish, 0, unroll=4)


def _attn_prompt_kernel(bound_ref, *refs, **static):
    bounded = bound_ref[0, 0] <= EXP2_SAFE_RANGE
    pl.when(bounded)(lambda: _attn_prompt_body(*refs, bounded=True, **static))
    pl.when(jnp.logical_not(bounded))(lambda: _attn_prompt_body(*refs, bounded=False, **static))


def _attn_steps(n_q, blocks_per_q):
    diag = [(i * blocks_per_q + d, i) for i in range(n_q) for d in range(blocks_per_q)]
    full = [(t, i) for t in range(n_q * blocks_per_q) for i in range(t // blocks_per_q + 1, n_q)]
    pairs = diag + full
    return jnp.asarray([pairs[0]] + pairs + [pairs[-1]], jnp.int32)


def _attn_prompt(bound, qt, kb, vt, ga, ra, lam_q, lam_k, g_sub, *, tq, lam_init):
    B, S, _ = kb.shape
    nt, tk = vt.shape[1], vt.shape[3]
    n_q = S // tq
    assert qt.shape[1:] == (n_q, D_MODEL, tq)
    blk = pl.BlockSpec((1, S, HEAD_W), lambda b, h: (b, 0, h))
    small = lambda shape: pl.BlockSpec(shape, lambda b, h: (0, 0))
    smem = pl.BlockSpec(memory_space=pltpu.SMEM)
    return pl.pallas_call(
        functools.partial(_attn_prompt_kernel, tq=tq, tk=tk, lam_init=lam_init),
        out_shape=jax.ShapeDtypeStruct((B, S, D_MODEL), BF16),
        grid=(B, HEADS),
        in_specs=[smem, smem,
                  pl.BlockSpec((1, n_q, HEAD_W, tq), lambda b, h: (b, 0, h, 0)),
                  blk,
                  pl.BlockSpec((1, nt, HEAD_W, tk), lambda b, h: (b, 0, h, 0)),
                  blk, blk, small((2, A_DIM)), small((2, A_DIM)), small((1, HEAD_W))],
        out_specs=blk, name="attn_prompt",
        scratch_shapes=[pltpu.VMEM((n_q, HEAD_W, 2 * tq), F32),
                        pltpu.VMEM((n_q, HEAD_W, 2 * tq), BF16),
                        pltpu.VMEM((n_q, 1, 2 * tq), F32),
                        pltpu.VMEM((n_q, 1, 2 * tq), F32),
                        pltpu.VMEM((tk, 2 * tq), jnp.int32),
                        pltpu.VMEM((tk, 2 * tq), F32), pltpu.VMEM((tk, 2 * tq), F32),
                        pltpu.VMEM((tk, 2 * tq), BF16), pltpu.VMEM((tk, 2 * tq), BF16)],
        compiler_params=pltpu.CompilerParams(dimension_semantics=("arbitrary",) * 2,
                                             vmem_limit_bytes=VMEM_LIMIT),
    )(bound, _attn_steps(n_q, tq // tk), qt, kb, vt, ga, ra, lam_q, lam_k, g_sub)


def _attn_sample_kernel(q_ref, kc_ref, vc_ref, kn_ref, vn_ref, ga_ref, ra_ref, lamq_ref, lamk_ref,
                        gsub_ref, o_ref, *, lam_init):
    t = q_ref.shape[1]
    half = HEADS // 2
    n_mixed = kc_ref.shape[1] // half
    lam = _lam(lamq_ref, lamk_ref, lam_init)
    row = lax.broadcasted_iota(jnp.int32, (4 * t, n_mixed), 0)
    col = lax.broadcasted_iota(jnp.int32, (4 * t, n_mixed), 1)
    own_head = (col & 1) == (row >= 2 * t).astype(jnp.int32)
    for h in range(half):
        heads = (h, h + half)
        sls = [slice(hh * HEAD_W, (hh + 1) * HEAD_W) for hh in heads]
        pair_rows = pl.ds(h, n_mixed, stride=half)
        qqs = [_split_maps(q_ref[0, :, sl]) for sl in sls]
        s_c = _dot_nt(jnp.concatenate(qqs, axis=0), kc_ref[0, pair_rows, :].astype(BF16))
        s_c = jnp.where(own_head, s_c, NEG_BIG)
        s_n = jnp.concatenate([_dot_nt(qq, kn_ref[0, :, sl]) for qq, sl in zip(qqs, sls)], axis=0)
        m = jnp.maximum(jnp.max(s_c, axis=-1, keepdims=True), jnp.max(s_n, axis=-1, keepdims=True))
        p_c = jnp.exp2(s_c - m)
        p_n = jnp.exp2(s_n - m)
        l = jnp.sum(p_c, axis=-1, keepdims=True) + jnp.sum(p_n, axis=-1, keepdims=True)
        acc = _dot(p_c.astype(BF16), vc_ref[0, pair_rows, :].astype(BF16))
        for j, sl in enumerate(sls):
            r0 = 2 * t * j
            a = acc[r0:r0 + 2 * t] + _dot(p_n[r0:r0 + 2 * t].astype(BF16), vn_ref[0, :, sl])
            lj = l[r0:r0 + 2 * t]
            o = a[:t] / lj[:t] - lam * (a[t:] / lj[t:])
            o_ref[0, :, sl] = _merge_out(o, gsub_ref, ga_ref[0, :, sl], ra_ref[0, :, sl], lam_init)


def _attn_sample(qa, kc, vc, kn, vn, ga, ra, lam_q, lam_k, g_sub, *, lam_init):
    DB, T, _ = qa.shape
    blk = pl.BlockSpec((1, T, D_MODEL), lambda b: (b, 0, 0))
    cache = pl.BlockSpec((1,) + kc.shape[1:], lambda b: (b, 0, 0))
    small = lambda shape: pl.BlockSpec(shape, lambda b: (0, 0))
    return pl.pallas_call(
        functools.partial(_attn_sample_kernel, lam_init=lam_init),
        out_shape=jax.ShapeDtypeStruct((DB, T, D_MODEL), BF16),
        grid=(DB,),
        in_specs=[blk, cache, cache, blk, blk, blk, blk, small((2, A_DIM)), small((2, A_DIM)),
                  small((1, HEAD_W))],
        out_specs=blk, name="attn_sample",
        compiler_params=pltpu.CompilerParams(dimension_semantics=("arbitrary",),
                                             vmem_limit_bytes=VMEM_LIMIT),
    )(qa, kc, vc, kn, vn, ga, ra, lam_q, lam_k, g_sub)


def _tail_kernel(x_ref, mix_ref, p_ref, wo_ref, gffn_ref, wg_ref, wu_ref, wd_ref, gple_ref, wple_ref,
                 wpg_ref, y_ref, act_ref):
    x1 = x_ref[...] + _dot(mix_ref[...], wo_ref[...])
    h = (_rms_scale(x1) * gffn_ref[...]).astype(BF16)
    for c in range(D_FF // MXU_N):
        sl = slice(c * MXU_N, (c + 1) * MXU_N)
        g = _dot(h, wg_ref[:, sl])
        u = _dot(h, wu_ref[:, sl])
        act_ref[:, sl] = ((g * jax.nn.sigmoid(g)) * u).astype(BF16)
    x2 = x1 + _dot(act_ref[...], wd_ref[...])
    h3 = (_rms_scale(x2) * gple_ref[...]).astype(BF16)
    gate = jax.nn.sigmoid(_dot(h3, wpg_ref[...]))
    y_ref[...] = x2 + gate * _dot(p_ref[...].astype(BF16), wple_ref[...])


def _tail(x2d, mix2d, p2d, w_o, g_ffn, w_g, w_u, w_d, g_ple, w_ple, w_pg, *, tm, name):
    n = x2d.shape[0]
    row = lambda i: (i, 0)
    const = lambda i: (0, 0)
    resident = functools.partial(pl.BlockSpec, pipeline_mode=pl.Buffered(1))
    return pl.pallas_call(
        _tail_kernel, out_shape=jax.ShapeDtypeStruct((n, D_MODEL), F32), grid=(n // tm,),
        in_specs=[pl.BlockSpec((tm, D_MODEL), row), pl.BlockSpec((tm, D_MODEL), row),
                  pl.BlockSpec((tm, PLE_DIM), row),
                  resident((D_MODEL, D_MODEL), const), resident((1, D_MODEL), const),
                  resident((D_MODEL, D_FF), const), resident((D_MODEL, D_FF), const),
                  resident((D_FF, D_MODEL), const), resident((1, D_MODEL), const),
                  resident((PLE_DIM, D_MODEL), const), resident((D_MODEL, D_MODEL), const)],
        out_specs=pl.BlockSpec((tm, D_MODEL), row),
        scratch_shapes=[pltpu.VMEM((tm, D_FF), BF16)], name=name,
        compiler_params=pltpu.CompilerParams(dimension_semantics=("arbitrary",),
                                             vmem_limit_bytes=VMEM_LIMIT),
    )(x2d, mix2d, p2d, w_o, g_ffn, w_g, w_u, w_d, g_ple, w_ple, w_pg)


def kernel(x_prompt, x_sample, cache_attn_k, cache_attn_v, state_ret, p_prompt, p_sample, w_in, g_mix_norm, g_q_norm, g_k_norm, lam_q, lam_k, g_sub_norm, g_ret_norm, w_o, g_ffn_norm, w_ff_gate, w_ff_up, w_ff_down, g_ple_norm, w_ple, w_ple_gate):
    B, S, D = x_prompt.shape
    DB, T, _ = x_sample.shape
    P = cache_attn_k.shape[2]
    depth = w_in.shape[0]
    tm = 512
    tm_s = 256
    assert depth == 1 and D == D_MODEL and S % tm == 0 and tm_s % T == 0 and (DB * T) % tm == 0
    l = 0
    lam_init = 0.8 - 0.6 * math.exp(-0.3 * l)

    w_in_b = w_in[l].astype(BF16)
    tail_w = (w_o[l].astype(BF16), g_ffn_norm[l][None, :], w_ff_gate[l].astype(BF16),
              w_ff_up[l].astype(BF16), w_ff_down[l].astype(BF16), g_ple_norm[l][None, :],
              w_ple[l].astype(BF16), w_ple_gate[l].astype(BF16))
    g_mix = g_mix_norm[l][None, :]
    gq = jnp.tile(g_q_norm[l], PROJ_N // A_DIM)[None, :]
    gk = jnp.tile(g_k_norm[l], PROJ_N // A_DIM)[None, :]
    g_sub = g_sub_norm[l][None, :]
    g_rn = g_ret_norm[l]
    grp = np.arange(MXU_N) // A_DIM
    gmat = (grp[:, None] == grp[None, :]).astype(BF16)

    cos_p, sin_p = _rope_tables(jnp.arange(S))
    xp2 = x_prompt.reshape(B * S, D)
    qt, kf, kb, vf, vt, ra, ga, st_p = _inproj(
        xp2, w_in_b, g_mix, gq, gk, cos_p, sin_p, gmat, _ret_tables(CHUNK), g_rn,
        tm=tm, seq=S, state0=None)
    r3 = lambda a: a.reshape(B, S, D)
    score_bound = ((A_DIM * Q_SCALE * 1.01) * jnp.max(jnp.abs(g_q_norm[l]))
                   * jnp.max(jnp.abs(g_k_norm[l]))).reshape(1, 1)
    mix_p = _attn_prompt(score_bound, qt, r3(kb), vt, r3(ga), r3(ra), lam_q[l], lam_k[l], g_sub,
                         tq=tm, lam_init=lam_init)
    y_p = _tail(xp2, mix_p.reshape(B * S, D), p_prompt[l].reshape(B * S, PLE_DIM), *tail_w,
                tm=tm, name="tail_prompt")

    cos_s, sin_s = _rope_tables(P + jnp.arange(T))
    reps = tm_s // T
    xs2 = x_sample.reshape(DB * T, D)
    st0 = state_ret[l].reshape(DB, HEADS * R_DK, HEAD_W)
    qa_s, kf_s, kb_s, vf_s, vb_s, ra_s, ga_s, st_s = _inproj(
        xs2, w_in_b, g_mix, gq, gk, jnp.tile(cos_s, (reps, 1)), jnp.tile(sin_s, (reps, 1)), gmat,
        _ret_tables(T), g_rn, tm=tm_s, seq=T, state0=st0)
    s3 = lambda a: a.reshape(DB, T, D)
    mix_s = _attn_sample(s3(qa_s), cache_attn_k[l].reshape(DB, P * HEADS, HEAD_W),
                         cache_attn_v[l].reshape(DB, P * HEADS, HEAD_W),
                         s3(kb_s), s3(vb_s), s3(ga_s), s3(ra_s), lam_q[l], lam_k[l], g_sub,
                         lam_init=lam_init)
    y_s = _tail(xs2, mix_s.reshape(DB * T, D), p_sample[l].reshape(DB * T, PLE_DIM), *tail_w,
                tm=tm, name="tail_sample")

    return (y_p.reshape(B, S, D), y_s.reshape(DB, T, D),
            kf.reshape(1, B, S, HEADS, HEAD_W), vf.reshape(1, B, S, HEADS, HEAD_W),
            st_p.reshape(1, B, HEADS, R_DK, HEAD_W),
            kf_s.reshape(1, DB, T, HEADS, HEAD_W), vf_s.reshape(1, DB, T, HEADS, HEAD_W),
            st_s.reshape(1, DB, HEADS, R_DK, HEAD_W))
```

```python
import functools
import math

import numpy as np

import jax
import jax.numpy as jnp
from jax import lax
from jax.experimental import pallas as pl
from jax.experimental.pallas import tpu as pltpu

F32 = jnp.float32
BF16 = jnp.bfloat16

D_MODEL = 1024
CHUNK = 64
PLE_DIM = 256
EPS = 1e-6
HEADS = 8
HEAD_W = 128
A_DIM = 64
R_DK = 64
ROPE_BASE = 10000.0
D_FF = 2816
N_PAIR = HEADS // 2

OFF_QA, OFF_KA, OFF_VA = 0, 1024, 2048
OFF_QR, OFF_KR, OFF_VR = 3072, 3584, 4096
OFF_GRET, OFF_GA, OFF_GR = 5120, 6144, 7168
IN_WIDTH = 8192

MXU_N = 256
PROJ_N = 2 * MXU_N
VMEM_LIMIT = 56 * 1024 * 1024
NEG_BIG = -1e30
ATT_TK = 256
ATT_UNROLL = 8
ATT_UNROLL_FULL = 28
EXP2_SAFE_RANGE = 60.0
Q_SCALE = (A_DIM ** -0.5) * math.log2(math.e)

_NT = (((1,), (1,)), ((), ()))


def _dot(a, b):
    return jnp.dot(a, b, preferred_element_type=F32)


def _dot_nt(a, b):
    return lax.dot_general(a, b, _NT, preferred_element_type=F32)


def _rms_scale(x):
    return x * lax.rsqrt(jnp.mean(x * x, axis=-1, keepdims=True) + EPS)


def _inproj_body(x_ref, w_ref, gmix_ref, gq_ref, gk_ref, cos_ref, sin_ref, gmat_ref,
                 dpair_ref, qdec_ref, kdec_ref, cdec_ref, grn_ref,
                 kf_ref, kb_ref, vf_ref, ra_ref, ga_ref,
                 qs_ref, ks_ref, vs_ref, gs_ref, *, L, get_state, set_state, store_q, store_v,
                 interleave):
    tm = x_ref.shape[0]
    x = x_ref[...]
    h = (_rms_scale(x) * gmix_ref[...]).astype(BF16)
    gmat = gmat_ref[...]

    def proj(off, c):
        return _dot(h, w_ref[:, off + c * PROJ_N: off + (c + 1) * PROJ_N])

    def group_norm(z, g_ref):
        zz = (z * z).astype(BF16)
        ss = jnp.concatenate([_dot(zz[:, j * MXU_N:(j + 1) * MXU_N], gmat)
                              for j in range(PROJ_N // MXU_N)], axis=1)
        return (z * lax.rsqrt(ss * (1.0 / A_DIM) + EPS)) * g_ref[...]

    for c in range(D_MODEL // PROJ_N):
        sl = slice(c * PROJ_N, (c + 1) * PROJ_N)
        vs_ref[:, sl] = proj(OFF_VR, c).astype(BF16)

    lane = lax.broadcasted_iota(jnp.int32, (tm, HEAD_W), 1)
    first_half = (lane % R_DK) < (R_DK // 2)
    cos = cos_ref[...]
    sin = sin_ref[...]

    def rotary(z):
        partner = jnp.where(first_half, pltpu.roll(z, HEAD_W - R_DK // 2, axis=1),
                            pltpu.roll(z, R_DK // 2, axis=1))
        return z * cos + partner * sin

    assert HEADS * R_DK == PROJ_N
    zq = proj(OFF_QR, 0)
    zk = proj(OFF_KR, 0)
    for g in range(N_PAIR):
        hs = slice(g * HEAD_W, (g + 1) * HEAD_W)
        qs_ref[:, hs] = rotary(zq[:, hs]).astype(BF16)
        ks_ref[:, hs] = rotary(zk[:, hs]) * (R_DK ** -0.5)

    def block_diag_cols(x):
        top = lax.broadcasted_iota(jnp.int32, x.shape, 0) < R_DK
        zero = jnp.zeros_like(x)
        return jnp.concatenate([jnp.where(top, x, zero), jnp.where(top, zero, x)], axis=1)

    def block_diag_rows(x):
        left = lax.broadcasted_iota(jnp.int32, x.shape, 1) < HEAD_W
        zero = jnp.zeros_like(x)
        return jnp.concatenate([jnp.where(left, x, zero), jnp.where(left, zero, x)], axis=0)

    def chunk_body(ci, carry):
        r0 = ci * L if isinstance(ci, int) else pl.multiple_of(ci * L, L)
        rows = pl.ds(r0, L)
        for g in range(N_PAIR):
            q2 = qs_ref[rows, g * HEAD_W:(g + 1) * HEAD_W]
            kt = ks_ref[rows, g * HEAD_W:(g + 1) * HEAD_W].T
            v2 = vs_ref[rows, 2 * g * HEAD_W:(2 * g + 2) * HEAD_W]
            a = _dot(q2, block_diag_cols(kt.astype(BF16)))
            a = (a * dpair_ref[g]).astype(BF16)
            intra = _dot(a, block_diag_rows(v2))
            state = get_state(ci, g)
            cross = _dot(q2, block_diag_cols(state.astype(BF16))) * qdec_ref[g]
            kv = _dot((kt * kdec_ref[g]).astype(BF16), v2)
            kv_pair = jnp.concatenate([kv[:R_DK, :HEAD_W], kv[R_DK:, HEAD_W:]], axis=0)
            set_state(ci, g, cdec_ref[g] * state + kv_pair)
            o2 = intra + cross
            for par in range(2):
                hh = 2 * g + par
                cols = slice(hh * HEAD_W, (hh + 1) * HEAD_W)
                on = _rms_scale(o2[:, par * HEAD_W:(par + 1) * HEAD_W]) * grn_ref[hh:hh + 1, :]
                gs_ref[rows, cols] = on
        return carry

    def q_unit(sl, c):
        store_q(sl, group_norm(proj(OFF_QA, c), gq_ref) * Q_SCALE)

    def store_cache_layout(ref, c, val):
        for j in range(PROJ_N // HEAD_W):
            head = c * (PROJ_N // HEAD_W) + j
            ref[pl.ds(head, tm, stride=HEADS), :] = val[:, j * HEAD_W:(j + 1) * HEAD_W]

    def k_unit(sl, c):
        kn = group_norm(proj(OFF_KA, c), gk_ref)
        store_cache_layout(kf_ref, c, kn)
        kb_ref[:, sl] = kn.astype(BF16)

    def v_unit(sl, c):
        v = proj(OFF_VA, c)
        store_cache_layout(vf_ref, c, v)
        store_v(sl, v)

    def gate_unit(sl, c):
        ga_ref[:, sl] = jax.nn.sigmoid(proj(OFF_GA, c)).astype(BF16)

    def retention_gate_unit(sl, c):
        g_ret = proj(OFF_GRET, c)
        gate_r = proj(OFF_GR, c)
        gate = (g_ret * jax.nn.sigmoid(g_ret)) * jax.nn.sigmoid(gate_r)
        ra_ref[:, sl] = (gs_ref[:, sl] * gate).astype(BF16)

    units = [functools.partial(u, slice(c * PROJ_N, (c + 1) * PROJ_N), c)
             for c in range(D_MODEL // PROJ_N) for u in (q_unit, k_unit, v_unit, gate_unit)]
    n_chunks = tm // L
    if interleave:
        per_chunk = -(-len(units) // n_chunks)
        for ci in range(n_chunks):
            chunk_body(ci, 0)
            for u in units[ci * per_chunk:(ci + 1) * per_chunk]:
                u()
    else:
        lax.fori_loop(0, n_chunks, chunk_body, 0)
        for u in units:
            u()
    for c in range(D_MODEL // PROJ_N):
        retention_gate_unit(slice(c * PROJ_N, (c + 1) * PROJ_N), c)


def _inproj_prompt_kernel(*refs):
    ins, outs = refs[:13], refs[13:]
    (qt_ref, kf_ref, kb_ref, vf_ref, vt_ref, ra_ref, ga_ref, st_ref,
     qs_ref, ks_ref, vs_ref, gs_ref) = outs

    @pl.when(pl.program_id(1) == 0)
    def _():
        st_ref[...] = jnp.zeros_like(st_ref)

    def get_state(ci, g):
        return st_ref[0, g * HEAD_W:(g + 1) * HEAD_W, :]

    def set_state(ci, g, val):
        st_ref[0, g * HEAD_W:(g + 1) * HEAD_W, :] = val

    def store_q(sl, val):
        qt_ref[0, 0, sl, :] = val.T.astype(BF16)

    def store_v(sl, val):
        for kb_i in range(val.shape[0] // ATT_TK):
            vt_ref[0, kb_i, sl, :] = val[kb_i * ATT_TK:(kb_i + 1) * ATT_TK].T.astype(BF16)

    _inproj_body(*ins, kf_ref, kb_ref, vf_ref, ra_ref, ga_ref, qs_ref, ks_ref, vs_ref, gs_ref,
                 L=CHUNK, get_state=get_state, set_state=set_state, store_q=store_q, store_v=store_v,
                 interleave=True)


def _inproj_sample_kernel(*refs, L):
    ins, s0_ref, outs = refs[:13], refs[13], refs[14:]
    (qa_ref, kf_ref, kb_ref, vf_ref, vb_ref, ra_ref, ga_ref, st_ref,
     qs_ref, ks_ref, vs_ref, gs_ref) = outs

    def get_state(ci, g):
        return s0_ref[ci, g * HEAD_W:(g + 1) * HEAD_W, :]

    def set_state(ci, g, val):
        st_ref[ci, g * HEAD_W:(g + 1) * HEAD_W, :] = val

    def store_q(sl, val):
        qa_ref[:, sl] = val.astype(BF16)

    def store_v(sl, val):
        vb_ref[:, sl] = val.astype(BF16)

    _inproj_body(*ins, kf_ref, kb_ref, vf_ref, ra_ref, ga_ref, qs_ref, ks_ref, vs_ref, gs_ref,
                 L=L, get_state=get_state, set_state=set_state, store_q=store_q, store_v=store_v,
                 interleave=True)


def _f32(a):
    return np.ascontiguousarray(a, np.float32)


def _ret_tables(T):
    log_g = np.log1p(-np.exp2(-5.0 - np.arange(HEADS, dtype=np.float64)))
    i = np.arange(T, dtype=np.float64)
    diff = i[:, None] - i[None, :]
    d_mat = np.where(diff >= 0, np.exp(log_g[:, None, None] * np.maximum(diff, 0.0)), 0.0)
    q_decay = np.exp(log_g[None, :] * (i[:, None] + 1.0))
    k_decay = np.exp(log_g[None, :] * (T - 1.0 - i[:, None]))
    chunk_decay = np.exp(log_g * T)
    dpair = d_mat.reshape(N_PAIR, 2, T, T).transpose(0, 2, 1, 3).reshape(N_PAIR, T, 2 * T)
    qdec = np.repeat(q_decay.reshape(T, N_PAIR, 2), HEAD_W, axis=2).transpose(1, 0, 2)
    kdec = np.repeat(k_decay.T.reshape(N_PAIR, 2, T), R_DK, axis=1)
    cdec = np.broadcast_to(np.repeat(chunk_decay.reshape(N_PAIR, 2), R_DK, axis=1)[:, :, None],
                           (N_PAIR, 2 * R_DK, HEAD_W))
    return _f32(dpair), _f32(qdec), _f32(kdec), _f32(cdec)


def _rope_tables(pos):
    half = R_DK // 2
    inv_freq = ROPE_BASE ** (-jnp.arange(half, dtype=F32) / half)
    ang = pos.astype(F32)[:, None] * inv_freq[None, :]
    cos, sin = jnp.cos(ang), jnp.sin(ang)
    return jnp.tile(cos, (1, 4)), jnp.tile(jnp.concatenate([-sin, sin], axis=1), (1, 2))


def _inproj(x2d, w_in, g_mix, gq, gk, cos_t, sin_t, gmat, tables, g_rn, *, tm, seq, state0):
    n = x2d.shape[0]
    prompt = state0 is None
    dpair, qdec, kdec, cdec = tables
    nt = seq // tm if prompt else 1
    nb = n // seq if prompt else n // tm
    per_tile = tm // seq if not prompt else 1

    const2 = lambda *_: (0, 0)
    const3 = lambda *_: (0, 0, 0)
    row = lambda b, i: (b * nt + i, 0)
    resident = functools.partial(pl.BlockSpec, pipeline_mode=pl.Buffered(1))
    in_specs = [
        pl.BlockSpec((tm, D_MODEL), row),
        resident((D_MODEL, IN_WIDTH), const2),
        resident((1, D_MODEL), const2),
        resident((1, PROJ_N), const2),
        resident((1, PROJ_N), const2),
        pl.BlockSpec((tm, HEAD_W), (lambda b, i: (i, 0)) if prompt else const2),
        pl.BlockSpec((tm, HEAD_W), (lambda b, i: (i, 0)) if prompt else const2),
        resident((MXU_N, MXU_N), const2),
        resident(dpair.shape, const3),
        resident(qdec.shape, const3),
        resident(kdec.shape, const3),
        resident(cdec.shape, const3),
        resident((HEADS, HEAD_W), const2),
    ]
    args = [x2d, w_in, g_mix, gq, gk, cos_t, sin_t, gmat, dpair, qdec, kdec, cdec, g_rn]
    st_rows = 2 * R_DK * N_PAIR
    if prompt:
        st_spec = pl.BlockSpec((1, st_rows, HEAD_W), lambda b, i: (b, 0, 0))
        st_shape = jax.ShapeDtypeStruct((nb, st_rows, HEAD_W), F32)
        kernel = _inproj_prompt_kernel
        L = CHUNK
    else:
        st_spec = pl.BlockSpec((per_tile, st_rows, HEAD_W), lambda b, i: (b, 0, 0))
        st_shape = jax.ShapeDtypeStruct(state0.shape, F32)
        in_specs.append(st_spec)
        args.append(state0)
        kernel = functools.partial(_inproj_sample_kernel, L=seq)
        L = seq
    tok = lambda dt: jax.ShapeDtypeStruct((n, D_MODEL), dt)
    tok_spec = pl.BlockSpec((tm, D_MODEL), row)
    cache = jax.ShapeDtypeStruct((n * HEADS, HEAD_W), F32)
    cache_spec = pl.BlockSpec((tm * HEADS, HEAD_W), row)
    out_shape = [tok(BF16), cache, tok(BF16), cache, tok(BF16), tok(BF16), tok(BF16), st_shape]
    out_specs = [tok_spec, cache_spec, tok_spec, cache_spec, tok_spec, tok_spec, tok_spec, st_spec]
    if prompt:
        out_shape[0] = jax.ShapeDtypeStruct((nb, nt, D_MODEL, tm), BF16)
        out_specs[0] = pl.BlockSpec((1, 1, D_MODEL, tm), lambda b, i: (b, i, 0, 0))
        out_shape[4] = jax.ShapeDtypeStruct((nb, seq // ATT_TK, D_MODEL, ATT_TK), BF16)
        out_specs[4] = pl.BlockSpec((1, tm // ATT_TK, D_MODEL, ATT_TK), lambda b, i: (b, i, 0, 0))
    scratch = [pltpu.VMEM((tm, HEADS * R_DK), BF16), pltpu.VMEM((tm, HEADS * R_DK), F32),
               pltpu.VMEM((tm, D_MODEL), BF16), pltpu.VMEM((tm, D_MODEL), F32)]
    return pl.pallas_call(
        kernel, out_shape=out_shape, grid=(nb, nt), in_specs=in_specs, out_specs=out_specs,
        scratch_shapes=scratch, name="inproj_prompt" if prompt else "inproj_sample",
        compiler_params=pltpu.CompilerParams(dimension_semantics=("arbitrary", "arbitrary"),
                                             vmem_limit_bytes=VMEM_LIMIT),
    )(*args)


def _lam(lamq_ref, lamk_ref, lam_init):
    e = jnp.exp(jnp.sum(lamq_ref[...] * lamk_ref[...], axis=-1, keepdims=True))
    return e[0:1, :] - e[1:2, :] + lam_init


def _split_maps(q):
    lane = lax.broadcasted_iota(jnp.int32, q.shape, 1)
    zero = jnp.zeros_like(q)
    return jnp.concatenate([jnp.where(lane < A_DIM, q, zero), jnp.where(lane < A_DIM, zero, q)], axis=0)


def _merge_out(o, gsub_ref, ga, ra, lam_init):
    on = (_rms_scale(o) * gsub_ref[...]) * (1.0 - lam_init)
    return (ga.astype(F32) * on + ra.astype(F32)).astype(BF16)


def _attn_prompt_body(steps_ref, qt_ref, k_ref, vt_ref, ga_ref, ra_ref, lamq_ref, lamk_ref, gsub_ref, o_ref,
                      acc_ref, qq_ref, m_ref, l_ref, d_ref, s0_ref, s1_ref, p0_ref, p1_ref,
                      *, tq, tk, lam_init, bounded):
    n_q = qt_ref.shape[1]
    n_steps = steps_ref.shape[0] - 2
    assert n_steps % ATT_UNROLL == 0 and tq % tk == 0
    s_refs, p_refs = (s0_ref, s1_ref), (p0_ref, p1_ref)

    def scores(s, slot):
        t, i = steps_ref[s, 0], steps_ref[s, 1]
        rows = pl.ds(pl.multiple_of(t * tk, tk), tk)
        s_refs[slot][...] = _dot(k_ref[0, rows, :], qq_ref[i])

    def softmax(s, slot, alpha, diagonal):
        t, i = steps_ref[s, 0], steps_ref[s, 1]
        st = s_refs[slot][...]
        if diagonal:
            st = jnp.where(d_ref[...] <= (tq // CHUNK) * i - (tk // CHUNK) * t, st, NEG_BIG)
        if bounded:
            p = jnp.exp2(st)
            l_ref[i] = l_ref[i] + jnp.sum(p, axis=0, keepdims=True)
            p_refs[slot][...] = p.astype(BF16)
            return alpha
        m_old = m_ref[i]
        m_new = jnp.maximum(m_old, jnp.max(st, axis=0, keepdims=True))
        m_ref[i] = m_new
        p = jnp.exp2(st - m_new)
        alpha_new = jnp.exp2(m_old - m_new)
        l_ref[i] = alpha_new * l_ref[i] + jnp.sum(p, axis=0, keepdims=True)
        p_refs[slot][...] = p.astype(BF16)
        return alpha_new

    def fold_values(s, slot, alpha):
        t, i = steps_ref[s, 0], steps_ref[s, 1]
        pv = _dot(vt_ref[0, t], p_refs[slot][...])
        acc_ref[i] = acc_ref[i] + pv if bounded else alpha * acc_ref[i] + pv

    def step(s, slot, alpha, diagonal):
        fold_values(s - 1, 1 - slot, alpha)
        scores(s + 1, 1 - slot)
        return softmax(s, slot, alpha, diagonal)

    feat = lax.broadcasted_iota(jnp.int32, (HEAD_W, tq), 0)
    for i in range(n_q):
        qt = qt_ref[0, i]
        zero = jnp.zeros_like(qt)
        qq_ref[i] = jnp.concatenate([jnp.where(feat < A_DIM, qt, zero), jnp.where(feat < A_DIM, zero, qt)],
                                    axis=1)
    @pl.when((pl.program_id(0) == 0) & (pl.program_id(1) == 0))
    def _():
        key = lax.broadcasted_iota(jnp.int32, (tk, 2 * tq), 0)
        qry = lax.broadcasted_iota(jnp.int32, (tk, 2 * tq), 1) % tq
        d_ref[...] = key // CHUNK - qry // CHUNK
    acc_ref[...] = jnp.zeros_like(acc_ref)
    m_ref[...] = jnp.full_like(m_ref, NEG_BIG)
    l_ref[...] = jnp.zeros_like(l_ref)
    p0_ref[...] = jnp.zeros_like(p0_ref)
    scores(1, 1)

    def body(jj, alpha, diagonal, first, unroll):
        for k in range(unroll):
            alpha = step(first + unroll * jj + k, (1 + k) & 1, alpha, diagonal)
        return alpha

    n_diag = n_q * (tq // tk)
    assert n_diag % ATT_UNROLL == 0
    alpha = lax.fori_loop(0, n_diag // ATT_UNROLL,
                          functools.partial(body, diagonal=True, first=1, unroll=ATT_UNROLL),
                          jnp.ones((1, 2 * tq), F32))
    n_full = n_steps - n_diag
    assert n_full % ATT_UNROLL_FULL == 0 and n_diag % 2 == 0 and ATT_UNROLL_FULL % 2 == 0
    alpha = lax.fori_loop(0, n_full // ATT_UNROLL_FULL,
                          functools.partial(body, diagonal=False, first=n_diag + 1, unroll=ATT_UNROLL_FULL),
                          alpha)
    fold_values(n_steps, n_steps & 1, alpha)

    lam = _lam(lamq_ref, lamk_ref, lam_init)

    def finish(i, carry):
        acc = acc_ref[i]
        l = l_ref[i]
        ot = acc[:HEAD_W, :tq] / l[:, :tq] - lam * (acc[:HEAD_W, tq:] / l[:, tq:])
        q_rows = pl.ds(pl.multiple_of(i * tq, tq), tq)
        o_ref[0, q_rows, :] = _merge_out(ot.T, gsub_ref, ga_ref[0, q_rows, :], ra_ref[0, q_rows, :], lam_init)
        return carry

    lax.fori_loop(0, n_q, finish, 0, unroll=4)


def _attn_prompt_kernel(bound_ref, *refs, **static):
    bounded = bound_ref[0, 0] <= EXP2_SAFE_RANGE
    pl.when(bounded)(lambda: _attn_prompt_body(*refs, bounded=True, **static))
    pl.when(jnp.logical_not(bounded))(lambda: _attn_prompt_body(*refs, bounded=False, **static))


def _attn_steps(n_q, blocks_per_q):
    diag = [(i * blocks_per_q + d, i) for i in range(n_q) for d in range(blocks_per_q)]
    full = [(t, i) for t in range(n_q * blocks_per_q) for i in range(t // blocks_per_q + 1, n_q)]
    pairs = diag + full
    return jnp.asarray([pairs[0]] + pairs + [pairs[-1]], jnp.int32)


def _attn_prompt(bound, qt, kb, vt, ga, ra, lam_q, lam_k, g_sub, *, tq, lam_init):
    B, S, _ = kb.shape
    nt, tk = vt.shape[1], vt.shape[3]
    n_q = S // tq
    assert qt.shape[1:] == (n_q, D_MODEL, tq)
    blk = pl.BlockSpec((1, S, HEAD_W), lambda b, h: (b, 0, h))
    small = lambda shape: pl.BlockSpec(shape, lambda b, h: (0, 0))
    smem = pl.BlockSpec(memory_space=pltpu.SMEM)
    return pl.pallas_call(
        functools.partial(_attn_prompt_kernel, tq=tq, tk=tk, lam_init=lam_init),
        out_shape=jax.ShapeDtypeStruct((B, S, D_MODEL), BF16),
        grid=(B, HEADS),
        in_specs=[smem, smem,
                  pl.BlockSpec((1, n_q, HEAD_W, tq), lambda b, h: (b, 0, h, 0)),
                  blk,
                  pl.BlockSpec((1, nt, HEAD_W, tk), lambda b, h: (b, 0, h, 0)),
                  blk, blk, small((2, A_DIM)), small((2, A_DIM)), small((1, HEAD_W))],
        out_specs=blk, name="attn_prompt",
        scratch_shapes=[pltpu.VMEM((n_q, HEAD_W, 2 * tq), F32),
                        pltpu.VMEM((n_q, HEAD_W, 2 * tq), BF16),
                        pltpu.VMEM((n_q, 1, 2 * tq), F32),
                        pltpu.VMEM((n_q, 1, 2 * tq), F32),
                        pltpu.VMEM((tk, 2 * tq), jnp.int32),
                        pltpu.VMEM((tk, 2 * tq), F32), pltpu.VMEM((tk, 2 * tq), F32),
                        pltpu.VMEM((tk, 2 * tq), BF16), pltpu.VMEM((tk, 2 * tq), BF16)],
        compiler_params=pltpu.CompilerParams(dimension_semantics=("arbitrary",) * 2,
                                             vmem_limit_bytes=VMEM_LIMIT),
    )(bound, _attn_steps(n_q, tq // tk), qt, kb, vt, ga, ra, lam_q, lam_k, g_sub)


def _attn_sample_kernel(q_ref, kc_ref, vc_ref, kn_ref, vn_ref, ga_ref, ra_ref, lamq_ref, lamk_ref,
                        gsub_ref, o_ref, *, lam_init):
    t = q_ref.shape[1]
    half = HEADS // 2
    n_mixed = kc_ref.shape[1] // half
    lam = _lam(lamq_ref, lamk_ref, lam_init)
    row = lax.broadcasted_iota(jnp.int32, (4 * t, n_mixed), 0)
    col = lax.broadcasted_iota(jnp.int32, (4 * t, n_mixed), 1)
    own_head = (col & 1) == (row >= 2 * t).astype(jnp.int32)
    for h in range(half):
        heads = (h, h + half)
        sls = [slice(hh * HEAD_W, (hh + 1) * HEAD_W) for hh in heads]
        pair_rows = pl.ds(h, n_mixed, stride=half)
        qqs = [_split_maps(q_ref[0, :, sl]) for sl in sls]
        s_c = _dot_nt(jnp.concatenate(qqs, axis=0), kc_ref[0, pair_rows, :].astype(BF16))
        s_c = jnp.where(own_head, s_c, NEG_BIG)
        s_n = jnp.concatenate([_dot_nt(qq, kn_ref[0, :, sl]) for qq, sl in zip(qqs, sls)], axis=0)
        m = jnp.maximum(jnp.max(s_c, axis=-1, keepdims=True), jnp.max(s_n, axis=-1, keepdims=True))
        p_c = jnp.exp2(s_c - m)
        p_n = jnp.exp2(s_n - m)
        l = jnp.sum(p_c, axis=-1, keepdims=True) + jnp.sum(p_n, axis=-1, keepdims=True)
        acc = _dot(p_c.astype(BF16), vc_ref[0, pair_rows, :].astype(BF16))
        for j, sl in enumerate(sls):
            r0 = 2 * t * j
            a = acc[r0:r0 + 2 * t] + _dot(p_n[r0:r0 + 2 * t].astype(BF16), vn_ref[0, :, sl])
            lj = l[r0:r0 + 2 * t]
            o = a[:t] / lj[:t] - lam * (a[t:] / lj[t:])
            o_ref[0, :, sl] = _merge_out(o, gsub_ref, ga_ref[0, :, sl], ra_ref[0, :, sl], lam_init)


def _attn_sample(qa, kc, vc, kn, vn, ga, ra, lam_q, lam_k, g_sub, *, lam_init):
    DB, T, _ = qa.shape
    blk = pl.BlockSpec((1, T, D_MODEL), lambda b: (b, 0, 0))
    cache = pl.BlockSpec((1,) + kc.shape[1:], lambda b: (b, 0, 0))
    small = lambda shape: pl.BlockSpec(shape, lambda b: (0, 0))
    return pl.pallas_call(
        functools.partial(_attn_sample_kernel, lam_init=lam_init),
        out_shape=jax.ShapeDtypeStruct((DB, T, D_MODEL), BF16),
        grid=(DB,),
        in_specs=[blk, cache, cache, blk, blk, blk, blk, small((2, A_DIM)), small((2, A_DIM)),
                  small((1, HEAD_W))],
        out_specs=blk, name="attn_sample",
        compiler_params=pltpu.CompilerParams(dimension_semantics=("arbitrary",),
                                             vmem_limit_bytes=VMEM_LIMIT),
    )(qa, kc, vc, kn, vn, ga, ra, lam_q, lam_k, g_sub)


def _tail_kernel(x_ref, mix_ref, p_ref, wo_ref, gffn_ref, wg_ref, wu_ref, wd_ref, gple_ref, wple_ref,
                 wpg_ref, y_ref, act_ref):
    x1 = x_ref[...] + _dot(mix_ref[...], wo_ref[...])
    h = (_rms_scale(x1) * gffn_ref[...]).astype(BF16)
    for c in range(D_FF // MXU_N):
        sl = slice(c * MXU_N, (c + 1) * MXU_N)
        g = _dot(h, wg_ref[:, sl])
        u = _dot(h, wu_ref[:, sl])
        act_ref[:, sl] = ((g * jax.nn.sigmoid(g)) * u).astype(BF16)
    x2 = x1 + _dot(act_ref[...], wd_ref[...])
    h3 = (_rms_scale(x2) * gple_ref[...]).astype(BF16)
    gate = jax.nn.sigmoid(_dot(h3, wpg_ref[...]))
    y_ref[...] = x2 + gate * _dot(p_ref[...].astype(BF16), wple_ref[...])


def _tail(x2d, mix2d, p2d, w_o, g_ffn, w_g, w_u, w_d, g_ple, w_ple, w_pg, *, tm, name):
    n = x2d.shape[0]
    row = lambda i: (i, 0)
    const = lambda i: (0, 0)
    resident = functools.partial(pl.BlockSpec, pipeline_mode=pl.Buffered(1))
    return pl.pallas_call(
        _tail_kernel, out_shape=jax.ShapeDtypeStruct((n, D_MODEL), F32), grid=(n // tm,),
        in_specs=[pl.BlockSpec((tm, D_MODEL), row), pl.BlockSpec((tm, D_MODEL), row),
                  pl.BlockSpec((tm, PLE_DIM), row),
                  resident((D_MODEL, D_MODEL), const), resident((1, D_MODEL), const),
                  resident((D_MODEL, D_FF), const), resident((D_MODEL, D_FF), const),
                  resident((D_FF, D_MODEL), const), resident((1, D_MODEL), const),
                  resident((PLE_DIM, D_MODEL), const), resident((D_MODEL, D_MODEL), const)],
        out_specs=pl.BlockSpec((tm, D_MODEL), row),
        scratch_shapes=[pltpu.VMEM((tm, D_FF), BF16)], name=name,
        compiler_params=pltpu.CompilerParams(dimension_semantics=("arbitrary",),
                                             vmem_limit_bytes=VMEM_LIMIT),
    )(x2d, mix2d, p2d, w_o, g_ffn, w_g, w_u, w_d, g_ple, w_ple, w_pg)


def kernel(x_prompt, x_sample, cache_attn_k, cache_attn_v, state_ret, p_prompt, p_sample, w_in, g_mix_norm, g_q_norm, g_k_norm, lam_q, lam_k, g_sub_norm, g_ret_norm, w_o, g_ffn_norm, w_ff_gate, w_ff_up, w_ff_down, g_ple_norm, w_ple, w_ple_gate):
    B, S, D = x_prompt.shape
    DB, T, _ = x_sample.shape
    P = cache_attn_k.shape[2]
    depth = w_in.shape[0]
    tm = 512
    tm_s = 256
    assert depth == 1 and D == D_MODEL and S % tm == 0 and tm_s % T == 0 and (DB * T) % tm == 0
    l = 0
    lam_init = 0.8 - 0.6 * math.exp(-0.3 * l)

    w_in_b = w_in[l].astype(BF16)
    tail_w = (w_o[l].astype(BF16), g_ffn_norm[l][None, :], w_ff_gate[l].astype(BF16),
              w_ff_up[l].astype(BF16), w_ff_down[l].astype(BF16), g_ple_norm[l][None, :],
              w_ple[l].astype(BF16), w_ple_gate[l].astype(BF16))
    g_mix = g_mix_norm[l][None, :]
    gq = jnp.tile(g_q_norm[l], PROJ_N // A_DIM)[None, :]
    gk = jnp.tile(g_k_norm[l], PROJ_N // A_DIM)[None, :]
    g_sub = g_sub_norm[l][None, :]
    g_rn = g_ret_norm[l]
    grp = np.arange(MXU_N) // A_DIM
    gmat = (grp[:, None] == grp[None, :]).astype(BF16)

    cos_p, sin_p = _rope_tables(jnp.arange(S))
    xp2 = x_prompt.reshape(B * S, D)
    qt, kf, kb, vf, vt, ra, ga, st_p = _inproj(
        xp2, w_in_b, g_mix, gq, gk, cos_p, sin_p, gmat, _ret_tables(CHUNK), g_rn,
        tm=tm, seq=S, state0=None)
    r3 = lambda a: a.reshape(B, S, D)
    score_bound = ((A_DIM * Q_SCALE * 1.01) * jnp.max(jnp.abs(g_q_norm[l]))
                   * jnp.max(jnp.abs(g_k_norm[l]))).reshape(1, 1)
    mix_p = _attn_prompt(score_bound, qt, r3(kb), vt, r3(ga), r3(ra), lam_q[l], lam_k[l], g_sub,
                         tq=tm, lam_init=lam_init)
    y_p = _tail(xp2, mix_p.reshape(B * S, D), p_prompt[l].reshape(B * S, PLE_DIM), *tail_w,
                tm=tm, name="tail_prompt")

    cos_s, sin_s = _rope_tables(P + jnp.arange(T))
    reps = tm_s // T
    xs2 = x_sample.reshape(DB * T, D)
    st0 = state_ret[l].reshape(DB, HEADS * R_DK, HEAD_W)
    qa_s, kf_s, kb_s, vf_s, vb_s, ra_s, ga_s, st_s = _inproj(
        xs2, w_in_b, g_mix, gq, gk, jnp.tile(cos_s, (reps, 1)), jnp.tile(sin_s, (reps, 1)), gmat,
        _ret_tables(T), g_rn, tm=tm_s, seq=T, state0=st0)
    s3 = lambda a: a.reshape(DB, T, D)
    mix_s = _attn_sample(s3(qa_s), cache_attn_k[l].reshape(DB, P * HEADS, HEAD_W),
                         cache_attn_v[l].reshape(DB, P * HEADS, HEAD_W),
                         s3(kb_s), s3(vb_s), s3(ga_s), s3(ra_s), lam_q[l], lam_k[l], g_sub,
                         lam_init=lam_init)
    y_s = _tail(xs2, mix_s.reshape(DB * T, D), p_sample[l].reshape(DB * T, PLE_DIM), *tail_w,
                tm=tm, name="tail_sample")

    return (y_p.reshape(B, S, D), y_s.reshape(DB, T, D),
            kf.reshape(1, B, S, HEADS, HEAD_W), vf.reshape(1, B, S, HEADS, HEAD_W),
            st_p.reshape(1, B, HEADS, R_DK, HEAD_W),
            kf_s.reshape(1, DB, T, HEADS, HEAD_W), vf_s.reshape(1, DB, T, HEADS, HEAD_W),
            st_s.reshape(1, DB, HEADS, R_DK, HEAD_W))
```
